```python
import jax, jax.numpy as jnp
from jax import lax
import numpy as np

D_MODEL = 1024
BATCH = 8
SEQ = 8192
DEPTH = 1
DEC_BATCH = 128
DEC_SEQ = 1
PAST_LEN = 8192
PAGE_SIZE = 128

N_HEADS = 8
N_KV_HEADS = 2
HEAD_DIM = 64
GROUP = N_HEADS // N_KV_HEADS
NSA_DIM = N_HEADS * HEAD_DIM
KV_DIM = 2 * N_KV_HEADS * HEAD_DIM
L_CMP = 32
L_SEL = 64
CMP_PER_SEL = L_SEL // L_CMP
CMP_HIDDEN = 2 * HEAD_DIM
TOP_N = 16
WINDOW = 512
Q_BLOCK = 128
ROPE_THETA = 10000.0
POOL_WINDOWS = (2, 4, 8, 16)
N_POOL_GROUPS = len(POOL_WINDOWS)
POOL_DIM = D_MODEL // 2
POOL_GROUP_DIM = POOL_DIM // N_POOL_GROUPS
POOL_BUF = max(POOL_WINDOWS) - 1
N_EXPERTS = 32
TOP_K = 4
D_EXPERT = D_MODEL
SWIGLU_LIMIT = 7.0
SWIGLU_ALPHA = 1.702
MOE_CHUNK = 256
IN_SPLITS = (NSA_DIM, KV_DIM, KV_DIM, KV_DIM, 3 * N_HEADS, POOL_DIM, 2 * D_MODEL)
IN_DIM = sum(IN_SPLITS)
RMS_EPS = 1e-6
NEG = -1e30
FORCED = 1e9

kernel_name = 'nsa_pool_moe_hybrid_step'


def rmsnorm(x, g):
    xf = x.astype(jnp.float32)
    r = lax.rsqrt(jnp.mean(xf * xf, axis=-1, keepdims=True) + RMS_EPS)
    return (xf * r * g.astype(jnp.float32)).astype(x.dtype)


def rope(x, pos):
    half = HEAD_DIM // 2
    inv = ROPE_THETA ** (-jnp.arange(half, dtype=jnp.float32) / half)
    ang = pos.astype(jnp.float32)[:, None] * inv[None, :]
    cos, sin = jnp.cos(ang)[None, :, None, :], jnp.sin(ang)[None, :, None, :]
    x1, x2 = x[..., :half].astype(jnp.float32), x[..., half:].astype(jnp.float32)
    return jnp.concatenate([x1 * cos - x2 * sin, x2 * cos + x1 * sin], axis=-1).astype(x.dtype)


def project_inputs(xn, pos, w_in):
    B, S = xn.shape[:2]
    offsets = np.cumsum(IN_SPLITS)[:-1].tolist()
    q, kv_c, kv_s, kv_w, g_nsa, u, g_merge = jnp.split(xn @ w_in, offsets, axis=-1)
    q = rope(q.reshape(B, S, N_HEADS, HEAD_DIM), pos)

    def kv_rows(t):
        t = t.reshape(B, S, 2, N_KV_HEADS, HEAD_DIM)
        return jnp.stack([rope(t[:, :, 0], pos), t[:, :, 1]], axis=2)

    gates = jax.nn.sigmoid(g_nsa.reshape(B, S, N_HEADS, 3))
    return q, kv_rows(kv_c), kv_rows(kv_s), kv_rows(kv_w), gates, u, g_merge


def compress(kv, cmp_pe, cmp_w1, cmp_w2):
    B, T = kv.shape[:2]
    nc = T // L_CMP
    blocks = kv[:, :nc * L_CMP].reshape(B, nc, L_CMP, 2, N_KV_HEADS, HEAD_DIM)
    blocks = blocks + cmp_pe[None, None, :, :, None, :]
    hid = jax.nn.silu(jnp.einsum('bnlcgd,cldh->bncgh', blocks, cmp_w1))
    out = jnp.einsum('bncgh,chd->bncgd', hid, cmp_w2)
    return out[:, :, 0], out[:, :, 1]


def to_blocks(rows):
    B, T = rows.shape[:2]
    nsel = -(-T // L_SEL)
    rows = jnp.pad(rows, ((0, 0), (0, nsel * L_SEL - T), (0, 0), (0, 0)))
    return rows.reshape(B, nsel, L_SEL, N_KV_HEADS, HEAD_DIM).transpose(0, 3, 1, 2, 4)


def masked_softmax(s, mask):
    s = jnp.where(mask, s.astype(jnp.float32), NEG)
    return jax.nn.softmax(s, axis=-1) * mask


def nsa_core(q, pos, kc, vc, ks_blk, vs_blk, kw, vw, kw_pos, gates):
    B, Q = q.shape[:2]
    nc, nsel = kc.shape[1], ks_blk.shape[2]
    scale = HEAD_DIM ** -0.5
    qg = q.reshape(B, Q, N_KV_HEADS, GROUP, HEAD_DIM)
    c_end = jnp.arange(nc) * L_CMP + (L_CMP - 1)
    mask_c = (c_end[None, :] <= pos[:, None])[None, :, None, None, :]
    p_c = masked_softmax(jnp.einsum('bqgrd,bngd->bqgrn', qg, kc) * scale, mask_c)
    o_c = jnp.einsum('bqgrn,bngd->bqgrd', p_c.astype(vc.dtype), vc)
    imp = p_c.sum(axis=3)
    imp = jnp.pad(imp, ((0, 0), (0, 0), (0, 0), (0, nsel * CMP_PER_SEL - nc)))
    imp = imp.reshape(B, Q, N_KV_HEADS, nsel, CMP_PER_SEL).sum(-1)
    blk = jnp.arange(nsel)[None, :]
    cur = (pos // L_SEL)[:, None]
    forced = (blk == 0) | (blk == cur) | (blk == cur - 1)
    score = jnp.where(forced[None, :, None, :], FORCED,
                      jnp.where((blk < cur)[None, :, None, :], imp, NEG))
    n_top = min(TOP_N, nsel)
    top_val, top_idx = lax.top_k(score, n_top)
    b_ix = jnp.arange(B)[:, None, None, None]
    g_ix = jnp.arange(N_KV_HEADS)[None, None, :, None]
    k_sel = ks_blk[b_ix, g_ix, top_idx]
    v_sel = vs_blk[b_ix, g_ix, top_idx]
    k_pos = top_idx[..., None] * L_SEL + jnp.arange(L_SEL)
    mask_s = (top_val > NEG / 2)[..., None] & (k_pos <= pos[None, :, None, None, None])
    s_s = jnp.einsum('bqgrd,bqgnld->bqgrnl', qg, k_sel) * scale
    p_s = masked_softmax(s_s.reshape(B, Q, N_KV_HEADS, GROUP, n_top * L_SEL),
                         mask_s.reshape(B, Q, N_KV_HEADS, 1, n_top * L_SEL))
    o_s = jnp.einsum('bqgrm,bqgmd->bqgrd', p_s.astype(v_sel.dtype),
                     v_sel.reshape(B, Q, N_KV_HEADS, n_top * L_SEL, HEAD_DIM))
    rel = pos[:, None] - kw_pos[None, :]
    mask_w = ((rel >= 0) & (rel < WINDOW) & (kw_pos[None, :] >= 0))[None, :, None, None, :]
    p_w = masked_softmax(jnp.einsum('bqgrd,bkgd->bqgrk', qg, kw) * scale, mask_w)
    o_w = jnp.einsum('bqgrk,bkgd->bqgrd', p_w.astype(vw.dtype), vw)
    g = gates.reshape(B, Q, N_KV_HEADS, GROUP, 3)
    o = g[..., 0:1] * o_c + g[..., 1:2] * o_s + g[..., 2:3] * o_w
    return o.reshape(B, Q, NSA_DIM)


def nsa_prompt(q, kv_c, kv_s, kv_w, gates, cmp_pe, cmp_w1, cmp_w2):
    B, S = q.shape[:2]
    kc, vc = compress(kv_c, cmp_pe, cmp_w1, cmp_w2)
    ks_blk, vs_blk = to_blocks(kv_s[:, :, 0]), to_blocks(kv_s[:, :, 1])
    kw_pad = jnp.pad(kv_w, ((0, 0), (WINDOW, 0), (0, 0), (0, 0), (0, 0)))

    def one_block(i):
        start = i * Q_BLOCK
        qb = lax.dynamic_slice_in_dim(q, start, Q_BLOCK, axis=1)
        gb = lax.dynamic_slice_in_dim(gates, start, Q_BLOCK, axis=1)
        wb = lax.dynamic_slice_in_dim(kw_pad, start, Q_BLOCK + WINDOW, axis=1)
        pos = start + jnp.arange(Q_BLOCK, dtype=jnp.int32)
        kw_pos = start - WINDOW + jnp.arange(Q_BLOCK + WINDOW, dtype=jnp.int32)
        return nsa_core(qb, pos, kc, vc, ks_blk, vs_blk, wb[:, :, 0], wb[:, :, 1], kw_pos, gb)

    out = lax.map(one_block, jnp.arange(S // Q_BLOCK, dtype=jnp.int32))
    return out.transpose(1, 0, 2, 3).reshape(B, S, NSA_DIM)


def nsa_sample(q, pos, kv_c, kv_s, kv_w, gates, cache_c, cache_s, win_buf, page_table,
               cmp_pe, cmp_w1, cmp_w2):
    n_seq = page_table.shape[0]

    def gather_past(cache):
        return cache[page_table].reshape(n_seq, -1, 2, N_KV_HEADS, HEAD_DIM)

    full_c = jnp.concatenate([gather_past(cache_c), kv_c], axis=1)
    kc, vc = compress(full_c, cmp_pe, cmp_w1, cmp_w2)
    full_s = jnp.concatenate([gather_past(cache_s), kv_s], axis=1)
    ks_blk, vs_blk = to_blocks(full_s[:, :, 0]), to_blocks(full_s[:, :, 1])
    w_buf = win_buf.shape[1]
    kw_all = jnp.concatenate([win_buf, kv_w], axis=1)
    kw_pos = PAST_LEN - w_buf + jnp.arange(w_buf + kv_w.shape[1], dtype=jnp.int32)
    out = nsa_core(q, pos, kc, vc, ks_blk, vs_blk, kw_all[:, :, 0], kw_all[:, :, 1], kw_pos, gates)
    return out, kw_all[:, -w_buf:]


def pool_mix(u_ext, pos, w_pool, pool_scale):
    n_new = pos.shape[0]
    u_new = u_ext[:, POOL_BUF:]
    cs = jnp.pad(jnp.cumsum(u_ext.astype(jnp.float32), axis=1), ((0, 0), (1, 0), (0, 0)))
    outs = []
    for g, w in enumerate(POOL_WINDOWS):
        sl = slice(g * POOL_GROUP_DIM, (g + 1) * POOL_GROUP_DIM)
        hi = cs[:, POOL_BUF + 1:, sl]
        lo = cs[:, POOL_BUF + 1 - w:POOL_BUF + 1 - w + n_new, sl]
        cnt = jnp.minimum(w, pos + 1).astype(jnp.float32)[None, :, None]
        d = (hi - lo) / cnt - u_new[..., sl].astype(jnp.float32)
        outs.append(d.astype(u_ext.dtype) @ w_pool[g])
    return jnp.concatenate(outs, axis=-1) * pool_scale


def moe(x, w_router, b_router, w_gate, b_gate, w_up, b_up, w_down, b_down):
    shp = x.shape
    xt = x.reshape(-1, D_MODEL)
    n = xt.shape[0]
    xt = jnp.pad(xt, ((0, (-n) % MOE_CHUNK), (0, 0))).reshape(-1, MOE_CHUNK, D_MODEL)

    def chunk(xc):
        logits = (xc @ w_router + b_router).astype(jnp.float32)
        top_v, top_i = lax.top_k(logits, TOP_K)
        wts = jax.nn.softmax(top_v, axis=-1)
        comb = jnp.sum(jax.nn.one_hot(top_i, N_EXPERTS, dtype=jnp.float32) * wts[..., None], axis=-2)
        gate = jnp.minimum(jnp.einsum('td,edf->tef', xc, w_gate) + b_gate, SWIGLU_LIMIT)
        up = jnp.clip(jnp.einsum('td,edf->tef', xc, w_up) + b_up, -SWIGLU_LIMIT, SWIGLU_LIMIT)
        hid = (up + 1.0) * gate * jax.nn.sigmoid(SWIGLU_ALPHA * gate)
        y = jnp.einsum('tef,efd->ted', hid, w_down) + b_down
        return jnp.einsum('te,ted->td', comb.astype(y.dtype), y)

    return lax.map(chunk, xt).reshape(-1, D_MODEL)[:n].reshape(shp)


def layer_tail(h, attn, pool_out, g_merge, w_br_nsa, w_br_pool, w_out, g_ffn,
               w_router, b_router, w_gate, b_gate, w_up, b_up, w_down, b_down):
    gm = jax.nn.sigmoid(g_merge)
    merged = gm[..., :D_MODEL] * (attn @ w_br_nsa) + gm[..., D_MODEL:] * (pool_out @ w_br_pool)
    h = h + merged @ w_out
    return h + moe(rmsnorm(h, g_ffn), w_router, b_router, w_gate, b_gate, w_up, b_up, w_down, b_down)


def setup_inputs(seed: int = 0) -> dict:
    key = jax.random.key(seed)
    ks = jax.random.split(key, 32)

    def nrm(k, shape, scale):
        return jax.random.normal(k, shape, jnp.float32) * scale

    n_pages = PAST_LEN // PAGE_SIZE
    n_used = DEC_BATCH * n_pages
    n_phys = n_used + max(1, n_used // 4)
    w_buf = min(WINDOW, PAST_LEN)
    page_table = jax.random.permutation(ks[0], n_phys)[:n_used].reshape(DEC_BATCH, n_pages).astype(jnp.int32)
    kv_row = (2, N_KV_HEADS, HEAD_DIM)
    return {
        'x_prompt': nrm(ks[1], (BATCH, SEQ, D_MODEL), 1.0),
        'x_sample': nrm(ks[2], (DEC_BATCH, DEC_SEQ, D_MODEL), 1.0),
        'cache_kv_cmp': nrm(ks[3], (DEPTH, n_phys, PAGE_SIZE) + kv_row, 1.0),
        'cache_kv_sel': nrm(ks[4], (DEPTH, n_phys, PAGE_SIZE) + kv_row, 1.0),
        'state_kv_win': nrm(ks[5], (DEPTH, DEC_BATCH, w_buf) + kv_row, 1.0),
        'state_pool': nrm(ks[6], (DEPTH, DEC_BATCH, POOL_BUF, POOL_DIM), 1.0),
        'page_table': page_table,
        'g_mix': 1.0 + nrm(ks[7], (DEPTH, D_MODEL), 0.02),
        'w_in': nrm(ks[8], (DEPTH, D_MODEL, IN_DIM), D_MODEL ** -0.5),
        'cmp_pe': nrm(ks[9], (DEPTH, L_CMP, 2, HEAD_DIM), 0.1),
        'cmp_w1': nrm(ks[10], (DEPTH, 2, L_CMP, HEAD_DIM, CMP_HIDDEN), (L_CMP * HEAD_DIM) ** -0.5),
        'cmp_w2': nrm(ks[11], (DEPTH, 2, CMP_HIDDEN, HEAD_DIM), CMP_HIDDEN ** -0.5),
        'w_pool': nrm(ks[12], (DEPTH, N_POOL_GROUPS, POOL_GROUP_DIM, POOL_GROUP_DIM), POOL_GROUP_DIM ** -0.5),
        'pool_scale': 1.0 + nrm(ks[13], (DEPTH, POOL_DIM), 0.1),
        'w_br_nsa': nrm(ks[14], (DEPTH, NSA_DIM, D_MODEL), NSA_DIM ** -0.5),
        'w_br_pool': nrm(ks[15], (DEPTH, POOL_DIM, D_MODEL), POOL_DIM ** -0.5),
        'w_out': nrm(ks[16], (DEPTH, D_MODEL, D_MODEL), D_MODEL ** -0.5),
        'g_ffn': 1.0 + nrm(ks[17], (DEPTH, D_MODEL), 0.02),
        'w_router': nrm(ks[18], (DEPTH, D_MODEL, N_EXPERTS), D_MODEL ** -0.5),
        'b_router': nrm(ks[19], (DEPTH, N_EXPERTS), 0.01),
        'w_gate': nrm(ks[20], (DEPTH, N_EXPERTS, D_MODEL, D_EXPERT), D_MODEL ** -0.5),
        'b_gate': nrm(ks[21], (DEPTH, N_EXPERTS, D_EXPERT), 0.01),
        'w_up': nrm(ks[22], (DEPTH, N_EXPERTS, D_MODEL, D_EXPERT), D_MODEL ** -0.5),
        'b_up': nrm(ks[23], (DEPTH, N_EXPERTS, D_EXPERT), 0.01),
        'w_down': nrm(ks[24], (DEPTH, N_EXPERTS, D_EXPERT, D_MODEL), D_EXPERT ** -0.5),
        'b_down': nrm(ks[25], (DEPTH, N_EXPERTS, D_MODEL), 0.01),
        'g_final': 1.0 + nrm(ks[26], (D_MODEL,), 0.02),
    }


def reference(x_prompt, x_sample, cache_kv_cmp, cache_kv_sel, state_kv_win, state_pool, page_table,
              g_mix, w_in, cmp_pe, cmp_w1, cmp_w2, w_pool, pool_scale, w_br_nsa, w_br_pool, w_out,
              g_ffn, w_router, b_router, w_gate, b_gate, w_up, b_up, w_down, b_down, g_final):
    pos_p = jnp.arange(SEQ, dtype=jnp.int32)
    pos_s = PAST_LEN + jnp.arange(DEC_SEQ, dtype=jnp.int32)
    hp, hs = x_prompt, x_sample
    kvc_p, kvc_s, kvs_p, kvs_s, kvw_p, kvw_s, pl_p, pl_s = [], [], [], [], [], [], [], []
    for l in range(DEPTH):
        q, kv_c, kv_s, kv_w, gts, u, g_merge = project_inputs(rmsnorm(hp, g_mix[l]), pos_p, w_in[l])
        attn = nsa_prompt(q, kv_c, kv_s, kv_w, gts, cmp_pe[l], cmp_w1[l], cmp_w2[l])
        u_ext = jnp.pad(u, ((0, 0), (POOL_BUF, 0), (0, 0)))
        pool_out = pool_mix(u_ext, pos_p, w_pool[l], pool_scale[l])
        hp = layer_tail(hp, attn, pool_out, g_merge, w_br_nsa[l], w_br_pool[l], w_out[l], g_ffn[l],
                        w_router[l], b_router[l], w_gate[l], b_gate[l], w_up[l], b_up[l], w_down[l], b_down[l])
        kvc_p.append(kv_c)
        kvs_p.append(kv_s)
        kvw_p.append(kv_w[:, -min(WINDOW, SEQ):])
        pl_p.append(u[:, -POOL_BUF:])
        q, kv_c, kv_s, kv_w, gts, u, g_merge = project_inputs(rmsnorm(hs, g_mix[l]), pos_s, w_in[l])
        attn, win_new = nsa_sample(q, pos_s, kv_c, kv_s, kv_w, gts, cache_kv_cmp[l], cache_kv_sel[l],
                                   state_kv_win[l], page_table, cmp_pe[l], cmp_w1[l], cmp_w2[l])
        u_ext = jnp.concatenate([state_pool[l], u], axis=1)
        pool_out = pool_mix(u_ext, pos_s, w_pool[l], pool_scale[l])
        hs = layer_tail(hs, attn, pool_out, g_merge, w_br_nsa[l], w_br_pool[l], w_out[l], g_ffn[l],
                        w_router[l], b_router[l], w_gate[l], b_gate[l], w_up[l], b_up[l], w_down[l], b_down[l])
        kvc_s.append(kv_c)
        kvs_s.append(kv_s)
        kvw_s.append(win_new)
        pl_s.append(u_ext[:, -POOL_BUF:])
    y_prompt = rmsnorm(hp, g_final)
    y_sample = rmsnorm(hs, g_final)
    return (y_prompt, y_sample,
            jnp.stack(kvc_p), jnp.stack(kvc_s),
            jnp.stack(kvs_p), jnp.stack(kvs_s),
            jnp.stack(kvw_p), jnp.stack(kvw_s),
            jnp.stack(pl_p), jnp.stack(pl_s))
```

```python
import functools

import jax
import jax.numpy as jnp
import numpy as np
from jax import lax
from jax.experimental import pallas as pl
from jax.experimental.pallas import tpu as pltpu

D_MODEL = 1024
N_HEADS = 8
N_KV_HEADS = 2
HEAD_DIM = 64
GROUP = N_HEADS // N_KV_HEADS
NSA_DIM = N_HEADS * HEAD_DIM
KV_DIM = 2 * N_KV_HEADS * HEAD_DIM
L_CMP = 32
L_SEL = 64
CMP_HIDDEN = 2 * HEAD_DIM
TOP_N = 16
WINDOW = 512
ROPE_THETA = 10000.0
POOL_WINDOWS = (2, 4, 8, 16)
POOL_DIM = D_MODEL // 2
POOL_GROUP_DIM = POOL_DIM // len(POOL_WINDOWS)
POOL_BUF = max(POOL_WINDOWS) - 1
POOL_HALO = POOL_BUF + 1
TOP_K = 4
SWIGLU_LIMIT = 7.0
SWIGLU_ALPHA = 1.702
RMS_EPS = 1e-6
NEG = -1e30
FORCED = 1e9

LANES = 128
QPAD_DIM = N_HEADS * LANES
VMEM_LIMIT = 56 * 1024 * 1024

BF16 = jnp.bfloat16
F32 = jnp.float32


def _params(n_grid, vmem=VMEM_LIMIT):
    return pltpu.CompilerParams(dimension_semantics=("arbitrary",) * n_grid, vmem_limit_bytes=vmem)


def _rms(x, g):
    r = lax.rsqrt(jnp.mean(x * x, axis=-1, keepdims=True) + RMS_EPS)
    return x * r * g


def _dot(a, b):
    return jnp.dot(a, b, preferred_element_type=F32)


def _dot_nt(a, b):
    return lax.dot_general(a, b, (((1,), (1,)), ((), ())), preferred_element_type=F32)


def _rope_chunk(x, cos, sin_signed):
    lane = lax.broadcasted_iota(jnp.int32, x.shape, 1)
    first = (lane % HEAD_DIM) < (HEAD_DIM // 2)
    swapped = jnp.where(first, pltpu.roll(x, LANES - HEAD_DIM // 2, 1), pltpu.roll(x, HEAD_DIM // 2, 1))
    return x * cos + swapped * sin_signed


def _project(x, g, cos, sin, wq_ref, wkv_ref, wg_ref, wu_ref,
             q_ref, kvc_ref, kvs_ref, kvw_ref, kvsb_ref, kvwb_ref, gates_ref, u_ref):
    xn = _rms(x, g).astype(BF16)
    q = _dot(xn, wq_ref[...])
    for h in range(N_HEADS):
        sl = slice(h * LANES, (h + 1) * LANES)
        q_ref[:, sl] = (_rope_chunk(q[:, sl], cos, sin) * (HEAD_DIM ** -0.5)).astype(BF16)
    kv = _dot(xn, wkv_ref[...])
    half = KV_DIM // 2
    for j, (f_ref, b_ref) in enumerate(((kvc_ref, None), (kvs_ref, kvsb_ref), (kvw_ref, kvwb_ref))):
        k = _rope_chunk(kv[:, j * KV_DIM:j * KV_DIM + half], cos, sin)
        v = kv[:, j * KV_DIM + half:(j + 1) * KV_DIM]
        f_ref[:, :half] = k
        f_ref[:, half:] = v
        if b_ref is not None:
            b_ref[:, :half] = k.astype(BF16)
            b_ref[:, half:] = v.astype(BF16)
    gates_ref[...] = jax.nn.sigmoid(_dot(xn, wg_ref[...]))
    u = _dot(xn, wu_ref[...])
    u_ref[...] = u
    return u


def _inproj_prompt_kernel(x_ref, xh_ref, cos_ref, sin_ref, g_ref, wq_ref, wkv_ref, wg_ref, wu_ref,
                          q_ref, kvc_ref, kvs_ref, kvw_ref, kvsb_ref, kvwb_ref, gates_ref, u_ref, d_ref,
                          *, tiles_per_seq):
    i = pl.program_id(0)
    tm = x_ref.shape[0]
    g = g_ref[...]
    u = _project(x_ref[...], g, cos_ref[...], sin_ref[...], wq_ref, wkv_ref, wg_ref, wu_ref,
                 q_ref, kvc_ref, kvs_ref, kvw_ref, kvsb_ref, kvwb_ref, gates_ref, u_ref)
    uh = _dot(_rms(xh_ref[...], g).astype(BF16), wu_ref[...])
    seq_tile = i % tiles_per_seq
    uh = jnp.where(seq_tile == 0, 0.0, uh)
    ext = jnp.concatenate([uh, u], axis=0)
    pos = seq_tile * tm + lax.broadcasted_iota(jnp.int32, (tm, 1), 0)
    for gi, w in enumerate(POOL_WINDOWS):
        sl = slice(gi * POOL_GROUP_DIM, (gi + 1) * POOL_GROUP_DIM)
        s = ext[:, sl]
        k = 1
        while k < w:
            s = s + pltpu.roll(s, k, 0)
            k *= 2
        cnt = jnp.minimum(w, pos + 1).astype(F32)
        d_ref[:, sl] = (s[POOL_HALO:, :] / cnt - u[:, sl]).astype(BF16)


def _inproj_sample_kernel(x_ref, sp_ref, cos_ref, sin_ref, g_ref, wq_ref, wkv_ref, wg_ref, wu_ref,
                          q_ref, kvc_ref, kvs_ref, kvw_ref, kvsb_ref, kvwb_ref, gates_ref, u_ref, d_ref,
                          *, past_len):
    u = _project(x_ref[...], g_ref[...], cos_ref[...], sin_ref[...], wq_ref, wkv_ref, wg_ref, wu_ref,
                 q_ref, kvc_ref, kvs_ref, kvw_ref, kvsb_ref, kvwb_ref, gates_ref, u_ref)
    for gi, w in enumerate(POOL_WINDOWS):
        sl = slice(gi * POOL_GROUP_DIM, (gi + 1) * POOL_GROUP_DIM)
        s = u[:, sl]
        for k in range(1, w):
            s = s + sp_ref[POOL_BUF - k][:, sl]
        cnt = float(min(w, past_len + 1))
        d_ref[:, sl] = (s / cnt - u[:, sl]).astype(BF16)


def _inproj_out_shapes(t):
    return (
        jax.ShapeDtypeStruct((t, QPAD_DIM), BF16),
        jax.ShapeDtypeStruct((t, KV_DIM), F32),
        jax.ShapeDtypeStruct((t, KV_DIM), F32),
        jax.ShapeDtypeStruct((t, KV_DIM), F32),
        jax.ShapeDtypeStruct((t, KV_DIM), BF16),
        jax.ShapeDtypeStruct((t, KV_DIM), BF16),
        jax.ShapeDtypeStruct((t, LANES), F32),
        jax.ShapeDtypeStruct((t, POOL_DIM), F32),
        jax.ShapeDtypeStruct((t, POOL_DIM), BF16),
    )


def _inproj_out_specs(tm):
    row = lambda i: (i, 0)
    return tuple(pl.BlockSpec((tm, w), row) for w in
                 (QPAD_DIM, KV_DIM, KV_DIM, KV_DIM, KV_DIM, KV_DIM, LANES, POOL_DIM, POOL_DIM))


def _weight_specs(shapes):
    return [pl.BlockSpec(s, lambda i, _n=len(s): (0,) * _n) for s in shapes]


def _inproj_prompt(x, cos, sin, g_mix, wq, wkv, wg, wu, seq, tm):
    t = x.shape[0]
    tiles_per_seq = seq // tm
    halo_blocks = tm // POOL_HALO
    in_specs = [
        pl.BlockSpec((tm, D_MODEL), lambda i: (i, 0)),
        pl.BlockSpec((POOL_HALO, D_MODEL), lambda i: (jnp.maximum(i * halo_blocks - 1, 0), 0)),
        pl.BlockSpec((tm, LANES), lambda i: (i % tiles_per_seq, 0)),
        pl.BlockSpec((tm, LANES), lambda i: (i % tiles_per_seq, 0)),
    ] + _weight_specs([g_mix.shape, wq.shape, wkv.shape, wg.shape, wu.shape])
    return pl.pallas_call(
        functools.partial(_inproj_prompt_kernel, tiles_per_seq=tiles_per_seq),
        grid=(t // tm,), in_specs=in_specs, out_specs=_inproj_out_specs(tm),
        out_shape=_inproj_out_shapes(t), compiler_params=_params(1), name="inproj_prompt",
    )(x, x, cos, sin, g_mix, wq, wkv, wg, wu)


def _inproj_sample(x, sp_t, cos, sin, g_mix, wq, wkv, wg, wu, past_len):
    t = x.shape[0]
    in_specs = [
        pl.BlockSpec((t, D_MODEL), lambda i: (0, 0)),
        pl.BlockSpec(sp_t.shape, lambda i: (0, 0, 0)),
        pl.BlockSpec((t, LANES), lambda i: (0, 0)),
        pl.BlockSpec((t, LANES), lambda i: (0, 0)),
    ] + _weight_specs([g_mix.shape, wq.shape, wkv.shape, wg.shape, wu.shape])
    return pl.pallas_call(
        functools.partial(_inproj_sample_kernel, past_len=past_len),
        grid=(1,), in_specs=in_specs, out_specs=_inproj_out_specs(t),
        out_shape=_inproj_out_shapes(t), compiler_params=_params(1), name="inproj_sample",
    )(x, sp_t, cos, sin, g_mix, wq, wkv, wg, wu)


def _compress_kernel(xk_ref, xv_ref, pe_ref, w1_ref, w2_ref, o_ref):
    nb = o_ref.shape[0]
    half = KV_DIM // 2
    for c, x_ref in enumerate((xk_ref, xv_ref)):
        acc = jnp.zeros((nb, w1_ref.shape[3]), F32)
        for l in range(L_CMP):
            xl = x_ref[pl.ds(l, nb, stride=L_CMP), :] + pe_ref[c, l]
            acc = acc + _dot(xl.astype(BF16), w1_ref[c, l])
        hid = acc * jax.nn.sigmoid(acc)
        o_ref[:, c * half:(c + 1) * half] = _dot(hid.astype(BF16), w2_ref[c]).astype(o_ref.dtype)


def _compress(rows, pe, w1, w2, nb_tile):
    n_rows = rows.shape[0]
    n_blocks = n_rows // L_CMP
    nb_tile = int(np.gcd(nb_tile, n_blocks))
    half = KV_DIM // 2
    return pl.pallas_call(
        _compress_kernel,
        grid=(n_blocks // nb_tile,),
        in_specs=[pl.BlockSpec((nb_tile * L_CMP, half), lambda i: (i, 0)),
                  pl.BlockSpec((nb_tile * L_CMP, half), lambda i: (i, 1))]
        + _weight_specs([pe.shape, w1.shape, w2.shape]),
        out_specs=pl.BlockSpec((nb_tile, KV_DIM), lambda i: (i, 0)),
        out_shape=jax.ShapeDtypeStruct((n_blocks, KV_DIM), BF16),
        compiler_params=_params(1), name="compress",
    )(rows, rows, pe, w1, w2)


def _stack_heads(q, g):
    return jnp.concatenate([q[:, (GROUP * g + r) * LANES:(GROUP * g + r + 1) * LANES] for r in range(GROUP)], axis=0)


def _masked_softmax(s, mask):
    s = jnp.where(mask, s, NEG)
    e = jnp.exp(s - jnp.max(s, axis=-1, keepdims=True))
    return e / jnp.sum(e, axis=-1, keepdims=True) * mask.astype(F32)


def _assemble_heads(o_heads):
    chunks = []
    for c in range(N_HEADS // 2):
        g = (2 * c) // GROUP
        a, b = o_heads[2 * c], o_heads[2 * c + 1]
        if g == 1:
            a = pltpu.roll(a, HEAD_DIM, 1)
        else:
            b = pltpu.roll(b, HEAD_DIM, 1)
        lane = lax.broadcasted_iota(jnp.int32, a.shape, 1)
        chunks.append(jnp.where(lane < HEAD_DIM, a, b))
    return jnp.concatenate(chunks, axis=1)


def _select_blocks(score, n_pick):
    n_blk = score.shape[1]
    lane = lax.broadcasted_iota(jnp.int32, score.shape, 1)
    work = score
    picks = []
    for _ in range(n_pick):
        m = jnp.max(work, axis=-1, keepdims=True)
        idx = jnp.min(jnp.where(work == m, lane, n_blk), axis=-1, keepdims=True)
        picks.append(idx)
        work = jnp.where(lane == idx, -jnp.inf, work)
    return picks


def _cmp_attention(qg, kc, vc, pos_rows, n_sel):
    s = _dot_nt(qg, kc)
    j = lax.broadcasted_iota(jnp.int32, s.shape, 1)
    n = 2 * (j % n_sel) + j // n_sel
    mask = (n * L_CMP + (L_CMP - 1)) <= pos_rows
    p = _masked_softmax(s, mask)
    return _dot(p.astype(BF16), vc), p


def _cmp_topk_kernel(q_ref, c_ref, gates_ref, oc_ref, sel_ref, *, n_sel, n_top):
    i = pl.program_id(1)
    tq = q_ref.shape[0]
    q = q_ref[...]
    comp = c_ref[0]
    kc, vc = comp[:, :LANES], comp[:, LANES:]
    gates = gates_ref[...]
    pos = i * tq + lax.broadcasted_iota(jnp.int32, (tq, 1), 0)
    pos_rows = jnp.concatenate([pos] * GROUP, axis=0)
    blk = lax.broadcasted_iota(jnp.int32, (tq, n_sel), 1)
    cur = pos // L_SEL
    forced = (blk == 0) | (blk == cur) | (blk == cur - 1)
    o_heads = []
    for g in range(N_KV_HEADS):
        o, p = _cmp_attention(_stack_heads(q, g), kc, vc, pos_rows, n_sel)
        imp = p[0:tq]
        for r in range(1, GROUP):
            imp = imp + p[r * tq:(r + 1) * tq]
        imp = imp[:, :n_sel] + imp[:, n_sel:]
        score = jnp.where(forced, FORCED, jnp.where(blk < cur, imp, NEG))
        sel = jnp.zeros((tq, n_sel), F32)
        for idx in _select_blocks(score, n_top):
            sel = jnp.where(blk == idx, 1.0, sel)
        sel_ref[:, g * n_sel:(g + 1) * n_sel] = jnp.where(score > NEG / 2, sel, 0.0).astype(BF16)
        for r in range(GROUP):
            h = GROUP * g + r
            o_heads.append(o[r * tq:(r + 1) * tq] * gates[:, h:h + 1])
    oc_ref[...] = _assemble_heads(o_heads)


def _cmp_topk_prompt(q, comp, gates, batch, seq, tq):
    n_sel = seq // L_SEL
    n_c = comp.shape[1]
    tiles = seq // tq
    row = lambda b, i: (b * tiles + i, 0)
    return pl.pallas_call(
        functools.partial(_cmp_topk_kernel, n_sel=n_sel, n_top=min(TOP_N, n_sel)),
        grid=(batch, tiles),
        in_specs=[pl.BlockSpec((tq, QPAD_DIM), row),
                  pl.BlockSpec((1, n_c, KV_DIM), lambda b, i: (b, 0, 0)),
                  pl.BlockSpec((tq, LANES), row)],
        out_specs=(pl.BlockSpec((tq, NSA_DIM), row), pl.BlockSpec((tq, N_KV_HEADS * n_sel), row)),
        out_shape=(jax.ShapeDtypeStruct((batch * seq, NSA_DIM), F32),
                   jax.ShapeDtypeStruct((batch * seq, N_KV_HEADS * n_sel), BF16)),
        compiler_params=_params(2), name="cmp_topk_prompt",
    )(q, comp, gates)


def _sel_win_kernel(q_ref, kvs_ref, kvw_ref, sel_ref, gates_ref, oc_ref, o_ref, *, n_sel, tk, span):
    i = pl.program_id(1)
    tq = q_ref.shape[0]
    seq = kvs_ref.shape[0]
    rows = GROUP * tq
    q = q_ref[...]
    gates = gates_ref[...]
    pos = i * tq + lax.broadcasted_iota(jnp.int32, (tq, 1), 0)
    n_kt = ((i + 1) * tq + tk - 1) // tk
    w0 = pl.multiple_of(jnp.clip((i + 1) * tq - span, 0, seq - span), tq)
    kw_pos = w0 + lax.broadcasted_iota(jnp.int32, (1, span), 1)
    rel = pos - kw_pos
    bias_w = jnp.where((rel >= 0) & (rel < WINDOW), 0.0, NEG)
    bias_w = jnp.concatenate([bias_w] * GROUP, axis=0)
    kw = kvw_ref[pl.ds(w0, span), :LANES]
    vw = kvw_ref[pl.ds(w0, span), LANES:]
    o_heads = []
    for g in range(N_KV_HEADS):
        qg = _stack_heads(q, g)
        sel_g = sel_ref[:, g * n_sel:(g + 1) * n_sel]

        def body(kt, carry, qg=qg, sel_g=sel_g):
            m, l, acc = carry
            k0 = pl.multiple_of(kt * tk, tk)
            k = kvs_ref[pl.ds(k0, tk), :LANES]
            v = kvs_ref[pl.ds(k0, tk), LANES:]
            s = _dot_nt(qg, k)
            key = k0 + lax.broadcasted_iota(jnp.int32, (n_sel, tk), 1)
            expand = jnp.where(key // L_SEL == lax.broadcasted_iota(jnp.int32, (n_sel, tk), 0), 1.0, 0.0)
            chosen = _dot(sel_g, expand.astype(BF16))
            kpos = k0 + lax.broadcasted_iota(jnp.int32, (1, tk), 1)
            bias = jnp.where((chosen > 0.5) & (kpos <= pos), 0.0, NEG)
            s = s + jnp.concatenate([bias] * GROUP, axis=0)
            m_new = jnp.maximum(m, jnp.max(s, axis=-1, keepdims=True))
            alpha = jnp.exp(m - m_new)
            p = jnp.exp(s - m_new)
            l = alpha * l + jnp.sum(p, axis=-1, keepdims=True)
            acc = alpha * acc + _dot(p.astype(BF16), v)
            return m_new, l, acc

        init = (jnp.full((rows, 1), NEG, F32), jnp.zeros((rows, 1), F32), jnp.zeros((rows, LANES), F32))
        _, l, acc = lax.fori_loop(0, n_kt, body, init)
        o_s = acc / l
        s_w = _dot_nt(qg, kw) + bias_w
        e = jnp.exp(s_w - jnp.max(s_w, axis=-1, keepdims=True))
        p_w = e / jnp.sum(e, axis=-1, keepdims=True)
        o_w = _dot(p_w.astype(BF16), vw)
        for r in range(GROUP):
            h = GROUP * g + r
            rs = slice(r * tq, (r + 1) * tq)
            o_heads.append(o_s[rs] * gates[:, N_HEADS + h:N_HEADS + h + 1]
                           + o_w[rs] * gates[:, 2 * N_HEADS + h:2 * N_HEADS + h + 1])
    o_ref[...] = (_assemble_heads(o_heads) + oc_ref[...]).astype(o_ref.dtype)


def _sel_win_prompt(q, kvs_b, kvw_b, sel, gates, oc, batch, seq, tq, tk):
    n_sel = seq // L_SEL
    tiles = seq // tq
    tk = min(tk, seq)
    span = min(tq + WINDOW, seq)
    row = lambda b, i: (b * tiles + i, 0)
    whole = lambda b, i: (b, 0)
    return pl.pallas_call(
        functools.partial(_sel_win_kernel, n_sel=n_sel, tk=tk, span=span),
        grid=(batch, tiles),
        in_specs=[pl.BlockSpec((tq, QPAD_DIM), row),
                  pl.BlockSpec((seq, KV_DIM), whole),
                  pl.BlockSpec((seq, KV_DIM), whole),
                  pl.BlockSpec((tq, N_KV_HEADS * n_sel), row),
                  pl.BlockSpec((tq, LANES), row),
                  pl.BlockSpec((tq, NSA_DIM), row)],
        out_specs=pl.BlockSpec((tq, NSA_DIM), row),
        out_shape=jax.ShapeDtypeStruct((batch * seq, NSA_DIM), BF16),
        compiler_params=_params(2), name="sel_win_prompt",
    )(q, kvs_b, kvw_b, sel, gates, oc)


def _sample_cmp_kernel(q_ref, c_ref, gates_ref, oc_ref, idx_ref, *, n_blk, n_pick, past_len):
    q = q_ref[0]
    comp = c_ref[0]
    gates = gates_ref[0]
    pos_rows = jnp.full((N_HEADS, 1), past_len, jnp.int32)
    o, p = _cmp_attention(q, comp[:, :LANES], comp[:, LANES:], pos_rows, n_blk)
    oc_ref[0] = o * gates[:, 0:1]
    cur = past_len // L_SEL
    blk = lax.broadcasted_iota(jnp.int32, (N_KV_HEADS, n_blk), 1)
    imp = jnp.concatenate(
        [jnp.sum(p[GROUP * g:GROUP * (g + 1)], axis=0, keepdims=True) for g in range(N_KV_HEADS)], axis=0)
    imp = imp[:, :n_blk] + imp[:, n_blk:]
    forced = (blk == 0) | (blk == cur) | (blk == cur - 1)
    score = jnp.where(forced, FORCED, jnp.where(blk < cur, imp, NEG))
    lane = lax.broadcasted_iota(jnp.int32, (N_KV_HEADS, LANES), 1)
    out = jnp.zeros((N_KV_HEADS, LANES), jnp.int32)
    for k, idx in enumerate(_select_blocks(score, n_pick)):
        out = jnp.where(lane == k, idx, out)
    idx_ref[0] = jnp.concatenate([out, jnp.zeros((N_HEADS - N_KV_HEADS, LANES), jnp.int32)], axis=0)


def _sample_cmp(q3, comp, gates3, n_pick, past_len):
    n_seq = q3.shape[0]
    n_c = comp.shape[1]
    blk3 = lambda b: (b, 0, 0)
    return pl.pallas_call(
        functools.partial(_sample_cmp_kernel, n_blk=n_c // 2, n_pick=n_pick, past_len=past_len),
        grid=(n_seq,),
        in_specs=[pl.BlockSpec((1, N_HEADS, LANES), blk3),
                  pl.BlockSpec((1, n_c, KV_DIM), blk3),
                  pl.BlockSpec((1, N_HEADS, LANES), blk3)],
        out_specs=(pl.BlockSpec((1, N_HEADS, LANES), blk3), pl.BlockSpec((1, N_HEADS, LANES), blk3)),
        out_shape=(jax.ShapeDtypeStruct((n_seq, N_HEADS, LANES), F32),
                   jax.ShapeDtypeStruct((n_seq, N_HEADS, LANES), jnp.int32)),
        compiler_params=_params(1), name="sample_cmp",
    )(q3, comp, gates3)


def _sample_sel_win_kernel(q_ref, kvsel_ref, win_ref, gates_ref, oc_ref, o_ref, *, n_valid):
    q = q_ref[0]
    gates = gates_ref[0]
    n_keys = kvsel_ref.shape[2]
    valid = lax.broadcasted_iota(jnp.int32, (N_HEADS, n_keys), 1) < n_valid
    win = win_ref[0].astype(BF16)
    s_w = _dot_nt(q, win[:, :LANES])
    p_w = _masked_softmax(s_w, jnp.ones(s_w.shape, jnp.bool_))
    o_w = _dot(p_w.astype(BF16), win[:, LANES:])
    row = lax.broadcasted_iota(jnp.int32, (N_HEADS, LANES), 0)
    o_s = jnp.zeros((N_HEADS, LANES), F32)
    for g in range(N_KV_HEADS):
        kv = kvsel_ref[0, g].astype(BF16)
        p = _masked_softmax(_dot_nt(q, kv[:, :LANES]), valid)
        o_s = jnp.where(row // GROUP == g, _dot(p.astype(BF16), kv[:, LANES:]), o_s)
    o_ref[0] = oc_ref[0] + o_s * gates[:, 1:2] + o_w * gates[:, 2:3]


def _sample_sel_win(q3, kvsel, win, gates3, oc, n_valid):
    n_seq = q3.shape[0]
    blk3 = lambda b: (b, 0, 0)
    return pl.pallas_call(
        functools.partial(_sample_sel_win_kernel, n_valid=n_valid),
        grid=(n_seq,),
        in_specs=[pl.BlockSpec((1, N_HEADS, LANES), blk3),
                  pl.BlockSpec((1,) + kvsel.shape[1:], lambda b: (b, 0, 0, 0)),
                  pl.BlockSpec((1,) + win.shape[1:], blk3),
                  pl.BlockSpec((1, N_HEADS, LANES), blk3),
                  pl.BlockSpec((1, N_HEADS, LANES), blk3)],
        out_specs=pl.BlockSpec((1, N_HEADS, LANES), blk3),
        out_shape=jax.ShapeDtypeStruct((n_seq, N_HEADS, LANES), F32),
        compiler_params=_params(1), name="sample_sel_win",
    )(q3, kvsel, win, gates3, oc)


def _tail_kernel(x_ref, attn_ref, d_ref, gmix_ref, wgm_ref, wpool_ref, pscale_ref, wbn_ref, wbp_ref, wout_ref,
                 gffn_ref, wr_ref, br_ref, h_ref, xn2_ref, route_ref, *, n_experts):
    x = x_ref[...]
    tm = x.shape[0]
    xn = _rms(x, gmix_ref[...]).astype(BF16)
    gm = jax.nn.sigmoid(_dot(xn, wgm_ref[...]))
    d = d_ref[...]
    pool = jnp.concatenate(
        [_dot(d[:, gi * POOL_GROUP_DIM:(gi + 1) * POOL_GROUP_DIM], wpool_ref[gi]) for gi in range(len(POOL_WINDOWS))],
        axis=1) * pscale_ref[...]
    merged = (gm[:, :D_MODEL] * _dot(attn_ref[...], wbn_ref[...])
              + gm[:, D_MODEL:] * _dot(pool.astype(BF16), wbp_ref[...]))
    h = x + _dot(merged.astype(BF16), wout_ref[...])
    h_ref[...] = h
    xn2 = _rms(h, gffn_ref[...]).astype(BF16)
    xn2_ref[...] = xn2
    logits = _dot(xn2, wr_ref[...]) + br_ref[...]
    lane = lax.broadcasted_iota(jnp.int32, (tm, LANES), 1)
    work = jnp.where(lane < n_experts, logits, -jnp.inf)
    vals, idxs = [], []
    for _ in range(TOP_K):
        m = jnp.max(work, axis=-1, keepdims=True)
        idx = jnp.min(jnp.where(work == m, lane, LANES), axis=-1, keepdims=True)
        vals.append(m)
        idxs.append(idx)
        work = jnp.where(lane == idx, -jnp.inf, work)
    es = [jnp.exp(v - vals[0]) for v in vals]
    tot = es[0] + es[1] + es[2] + es[3]
    route = jnp.zeros((tm, LANES), F32)
    for k in range(TOP_K):
        route = jnp.where(lane == k, idxs[k].astype(F32), route)
        route = jnp.where(lane == TOP_K + k, es[k] / tot, route)
    route_ref[...] = route


def _tail(x, attn, d, weights, n_experts, tm):
    t = x.shape[0]
    tm = min(tm, t)
    row = lambda i: (i, 0)
    return pl.pallas_call(
        functools.partial(_tail_kernel, n_experts=n_experts),
        grid=(t // tm,),
        in_specs=[pl.BlockSpec((tm, D_MODEL), row), pl.BlockSpec((tm, NSA_DIM), row),
                  pl.BlockSpec((tm, POOL_DIM), row)] + _weight_specs([w.shape for w in weights]),
        out_specs=(pl.BlockSpec((tm, D_MODEL), row), pl.BlockSpec((tm, D_MODEL), row),
                   pl.BlockSpec((tm, LANES), row)),
        out_shape=(jax.ShapeDtypeStruct((t, D_MODEL), F32), jax.ShapeDtypeStruct((t, D_MODEL), BF16),
                   jax.ShapeDtypeStruct((t, LANES), F32)),
        compiler_params=_params(1), name="tail",
    )(x, attn, d, *weights)


def _moe_kernel(te_ref, tv_ref, x_ref, wg_ref, bg_ref, wu_ref, bu_ref, wd_ref, bd_ref, y_ref):
    t = pl.program_id(0)

    @pl.when(tv_ref[t] > 0)
    def _():
        x = x_ref[...]
        gate = jnp.minimum(_dot(x, wg_ref[0]) + bg_ref[0], SWIGLU_LIMIT)
        up = jnp.clip(_dot(x, wu_ref[0]) + bu_ref[0], -SWIGLU_LIMIT, SWIGLU_LIMIT)
        hid = (up + 1.0) * gate * jax.nn.sigmoid(SWIGLU_ALPHA * gate)
        y_ref[...] = (_dot(hid.astype(BF16), wd_ref[0]) + bd_ref[0]).astype(y_ref.dtype)

    @pl.when(tv_ref[t] == 0)
    def _():
        y_ref[...] = jnp.zeros(y_ref.shape, y_ref.dtype)


def _moe(tile_e, tile_v, xs, wg, bg, wu, bu, wd, bd, tme):
    p_pad = xs.shape[0]
    d_e = wg.shape[2]
    wspec = lambda shape: pl.BlockSpec((1,) + shape, lambda t, te, tv: (te[t], 0, 0))
    grid_spec = pltpu.PrefetchScalarGridSpec(
        num_scalar_prefetch=2, grid=(p_pad // tme,),
        in_specs=[pl.BlockSpec((tme, D_MODEL), lambda t, te, tv: (t, 0)),
                  wspec((D_MODEL, d_e)), wspec((1, d_e)), wspec((D_MODEL, d_e)), wspec((1, d_e)),
                  wspec((d_e, D_MODEL)), wspec((1, D_MODEL))],
        out_specs=pl.BlockSpec((tme, D_MODEL), lambda t, te, tv: (t, 0)),
    )
    return pl.pallas_call(
        _moe_kernel, grid_spec=grid_spec,
        out_shape=jax.ShapeDtypeStruct((p_pad, D_MODEL), BF16),
        compiler_params=_params(1), name="moe_experts",
    )(tile_e, tile_v, xs, wg, bg, wu, bu, wd, bd)


def _final_kernel(h_ref, y_ref, route_ref, g_ref, o_ref):
    route = route_ref[...]
    acc = h_ref[...]
    for k in range(TOP_K):
        acc = acc + y_ref[k].astype(F32) * route[:, TOP_K + k:TOP_K + k + 1]
    o_ref[...] = _rms(acc, g_ref[...])


def _final(h, yg, route, g_final, tm):
    t = h.shape[0]
    tm = min(tm, t)
    row = lambda i: (i, 0)
    return pl.pallas_call(
        _final_kernel, grid=(t // tm,),
        in_specs=[pl.BlockSpec((tm, D_MODEL), row), pl.BlockSpec((TOP_K, tm, D_MODEL), lambda i: (0, i, 0)),
                  pl.BlockSpec((tm, LANES), row), pl.BlockSpec((1, D_MODEL), lambda i: (0, 0))],
        out_specs=pl.BlockSpec((tm, D_MODEL), row),
        out_shape=jax.ShapeDtypeStruct((t, D_MODEL), F32),
        compiler_params=_params(1), name="final_norm",
    )(h, yg, route, g_final)


def _rope_tables(pos):
    half = HEAD_DIM // 2
    inv = ROPE_THETA ** (-jnp.arange(half, dtype=F32) / half)
    ang = pos.astype(F32)[:, None] * inv[None, :]
    cos, sin = jnp.cos(ang), jnp.sin(ang)
    cos = jnp.concatenate([cos, cos] * (LANES // HEAD_DIM), axis=1)
    sin = jnp.concatenate([-sin, sin] * (LANES // HEAD_DIM), axis=1)
    return cos, sin


def _split_w_in(w_in):
    o_q = NSA_DIM
    o_g = o_q + 3 * KV_DIM
    o_u = o_g + 3 * N_HEADS
    o_m = o_u + POOL_DIM
    wq = w_in[:, :o_q].reshape(D_MODEL, N_HEADS, HEAD_DIM)
    zeros = jnp.zeros_like(wq)
    lo = jnp.concatenate([wq, zeros], axis=2)
    hi = jnp.concatenate([zeros, wq], axis=2)
    in_hi = (jnp.arange(N_HEADS) // GROUP == 1)[None, :, None]
    wq_pad = jnp.where(in_hi, hi, lo).reshape(D_MODEL, QPAD_DIM)
    wkv = w_in[:, o_q:o_g]
    wg = w_in[:, o_g:o_u].reshape(D_MODEL, N_HEADS, 3).transpose(0, 2, 1).reshape(D_MODEL, 3 * N_HEADS)
    wg = jnp.pad(wg, ((0, 0), (0, LANES - 3 * N_HEADS)))
    wu = w_in[:, o_u:o_m]
    wgm = w_in[:, o_m:]
    return [w.astype(BF16) for w in (wq_pad, wkv, wg, wu, wgm)]


def _compress_weights(cmp_pe, cmp_w1, cmp_w2):
    eye = jnp.eye(N_KV_HEADS, dtype=F32)
    half = KV_DIM // 2
    w1 = jnp.einsum('cldh,gf->clgdfh', cmp_w1, eye).reshape(2, L_CMP, half, N_KV_HEADS * CMP_HIDDEN)
    w2 = jnp.einsum('chd,gf->cghfd', cmp_w2, eye).reshape(2, N_KV_HEADS * CMP_HIDDEN, half)
    pe = jnp.broadcast_to(cmp_pe.transpose(1, 0, 2)[:, :, None, :], (2, L_CMP, N_KV_HEADS, HEAD_DIM))
    return pe.reshape(2, L_CMP, 1, half), w1.astype(BF16), w2.astype(BF16)


def _even_odd(comp, n_seq):
    n_c = comp.shape[0] // n_seq
    return comp.reshape(n_seq, n_c // 2, 2, KV_DIM).transpose(0, 2, 1, 3).reshape(n_seq, n_c, KV_DIM)


def _fix_sample_heads(o3):
    n_seq = o3.shape[0]
    o4 = o3.reshape(n_seq, N_KV_HEADS, GROUP, N_KV_HEADS, HEAD_DIM)
    return jnp.concatenate([o4[:, g, :, g, :] for g in range(N_KV_HEADS)], axis=1).reshape(n_seq, NSA_DIM)


def _gates_by_head(gates):
    g = gates[:, :3 * N_HEADS].reshape(-1, 3, N_HEADS).transpose(0, 2, 1)
    return jnp.pad(g, ((0, 0), (0, 0), (0, LANES - 3)))


def _expert_order(route, n_experts, tme):
    t = route.shape[0]
    n_pairs = t * TOP_K
    e_flat = route[:, :TOP_K].astype(jnp.int32).reshape(n_pairs)
    order = jnp.argsort(e_flat, stable=True)
    counts = jnp.sum((e_flat[:, None] == jnp.arange(n_experts)[None, :]).astype(jnp.int32), axis=0)
    padded = ((counts + tme - 1) // tme) * tme
    ends = jnp.cumsum(padded)
    off = ends - padded
    start = jnp.cumsum(counts) - counts
    e_sorted = e_flat[order]
    slot_sorted = off[e_sorted] + (jnp.arange(n_pairs, dtype=jnp.int32) - start[e_sorted])
    n_tiles = -(-n_pairs // tme) + n_experts
    p_pad = n_tiles * tme
    pair_slot = jnp.zeros((n_pairs,), jnp.int32).at[order].set(slot_sorted)
    src_tok = jnp.zeros((p_pad,), jnp.int32).at[slot_sorted].set((order // TOP_K).astype(jnp.int32))
    tile_start = jnp.arange(n_tiles, dtype=jnp.int32) * tme
    tile_e = jnp.minimum(jnp.searchsorted(ends, tile_start, side='right'), n_experts - 1).astype(jnp.int32)
    tile_v = (tile_start < ends[-1]).astype(jnp.int32)
    return pair_slot.reshape(t, TOP_K), src_tok, tile_e, tile_v


def kernel(x_prompt, x_sample, cache_kv_cmp, cache_kv_sel, state_kv_win, state_pool, page_table, g_mix, w_in, cmp_pe, cmp_w1, cmp_w2, w_pool, pool_scale, w_br_nsa, w_br_pool, w_out, g_ffn, w_router, b_router, w_gate, b_gate, w_up, b_up, w_down, b_down, g_final):
    batch, seq, _ = x_prompt.shape
    n_seq, dec_seq, _ = x_sample.shape
    depth = g_mix.shape[0]
    assert depth == 1 and dec_seq == 1
    n_pages = page_table.shape[1]
    page_size = cache_kv_cmp.shape[2]
    past_len = n_pages * page_size
    n_experts = w_router.shape[2]
    t_p = batch * seq
    kv_row = (2, N_KV_HEADS, HEAD_DIM)
    tm = min(512, seq)

    wq, wkv, wg, wu, wgm = _split_w_in(w_in[0])
    gmix = g_mix[0][None, :]
    pe, w1, w2 = _compress_weights(cmp_pe[0], cmp_w1[0], cmp_w2[0])
    cos_p, sin_p = _rope_tables(jnp.arange(seq, dtype=jnp.int32))
    cos_s, sin_s = _rope_tables(jnp.full((n_seq,), past_len, jnp.int32))
    wr = jnp.pad(w_router[0], ((0, 0), (0, LANES - n_experts))).astype(BF16)
    br = jnp.pad(b_router[0], (0, LANES - n_experts))[None, :]
    tail_w = [gmix, wgm, w_pool[0].astype(BF16), pool_scale[0][None, :], w_br_nsa[0].astype(BF16),
              w_br_pool[0].astype(BF16), w_out[0].astype(BF16), g_ffn[0][None, :], wr, br]

    xp = x_prompt.reshape(t_p, D_MODEL)
    q_p, kvc_p, kvs_p, kvw_p, kvsb_p, kvwb_p, gates_p, u_p, d_p = _inproj_prompt(
        xp, cos_p, sin_p, gmix, wq, wkv, wg, wu, seq, tm)
    comp_p = _even_odd(_compress(kvc_p, pe, w1, w2, 256), batch)
    tq = min(256, seq)
    oc_p, sel_p = _cmp_topk_prompt(q_p, comp_p, gates_p, batch, seq, tq)
    attn_p = _sel_win_prompt(q_p, kvsb_p, kvwb_p, sel_p, gates_p, oc_p, batch, seq, tq, 512)
    h_p, xn2_p, route_p = _tail(xp, attn_p, d_p, tail_w, n_experts, tm)

    xs = x_sample.reshape(n_seq, D_MODEL)
    sp_t = state_pool[0].transpose(1, 0, 2)
    q_s, kvc_s, kvs_s, kvw_s, _, _, gates_s, u_s, d_s = _inproj_sample(
        xs, sp_t, cos_s, sin_s, gmix, wq, wkv, wg, wu, past_len)
    n_phys = cache_kv_cmp.shape[1]
    blocks_per_page = page_size // L_CMP
    comp_all = _compress(cache_kv_cmp[0].reshape(n_phys * page_size, KV_DIM), pe, w1, w2, 256)
    comp_s = comp_all.reshape(n_phys, blocks_per_page, KV_DIM)[page_table].reshape(n_seq * n_pages * blocks_per_page, KV_DIM)
    comp_s = _even_odd(comp_s, n_seq)
    q3 = q_s.reshape(n_seq, N_HEADS, LANES)
    gates3 = _gates_by_head(gates_s)
    n_blk = past_len // L_SEL
    n_pick = min(TOP_N, n_blk + 1) - 1
    oc_s, idx_s = _sample_cmp(q3, comp_s, gates3, n_pick, past_len)
    top_idx = idx_s[:, :N_KV_HEADS, :n_pick]
    sel_per_page = page_size // L_SEL
    pages_b = jnp.broadcast_to(page_table[:, None, :], (n_seq, N_KV_HEADS, n_pages))
    phys = jnp.take_along_axis(pages_b, top_idx // sel_per_page, axis=2)
    sel_blocks = cache_kv_sel[0].reshape(n_phys * sel_per_page, L_SEL, KV_DIM)[phys * sel_per_page + top_idx % sel_per_page]
    new_blk = jnp.pad(kvs_s[:, None, None, None, :], ((0, 0), (0, 0), (0, 0), (0, L_SEL - 1), (0, 0)))
    new_blk = jnp.broadcast_to(new_blk, (n_seq, N_KV_HEADS, 1, L_SEL, KV_DIM))
    kvsel = jnp.concatenate([sel_blocks, new_blk], axis=2).reshape(n_seq, N_KV_HEADS, (n_pick + 1) * L_SEL, KV_DIM)
    w_buf = state_kv_win.shape[2]
    assert w_buf == WINDOW
    win_new = jnp.concatenate([state_kv_win[0].reshape(n_seq, w_buf, KV_DIM), kvw_s[:, None, :]], axis=1)[:, -w_buf:]
    o3 = _sample_sel_win(q3, kvsel, win_new, gates3, oc_s, n_pick * L_SEL + 1)
    attn_s = _fix_sample_heads(o3).astype(BF16)
    h_s, xn2_s, route_s = _tail(xs, attn_s, d_s, tail_w, n_experts, tm)

    tme = 512
    route = jnp.concatenate([route_p, route_s], axis=0)
    pair_slot, src_tok, tile_e, tile_v = _expert_order(route, n_experts, tme)
    x_sorted = jnp.concatenate([xn2_p, xn2_s], axis=0)[src_tok]
    y_sorted = _moe(tile_e, tile_v, x_sorted, w_gate[0].astype(BF16), b_gate[0][:, None, :],
                    w_up[0].astype(BF16), b_up[0][:, None, :], w_down[0].astype(BF16), b_down[0][:, None, :], tme)
    slot_t = pair_slot.T
    gfin = g_final[None, :]
    y_p = _final(h_p, y_sorted[slot_t[:, :t_p]], route_p, gfin, tm)
    y_s = _final(h_s, y_sorted[slot_t[:, t_p:]], route_s, gfin, tm)

    w_p = min(WINDOW, seq)
    return (
        y_p.reshape(batch, seq, D_MODEL),
        y_s.reshape(n_seq, 1, D_MODEL),
        kvc_p.reshape((1, batch, seq) + kv_row),
        kvc_s.reshape((1, n_seq, 1) + kv_row),
        kvs_p.reshape((1, batch, seq) + kv_row),
        kvs_s.reshape((1, n_seq, 1) + kv_row),
        kvw_p.reshape((batch, seq) + kv_row)[None, :, seq - w_p:],
        win_new.reshape((1, n_seq, w_buf) + kv_row),
        u_p.reshape(batch, seq, POOL_DIM)[None, :, seq - POOL_BUF:],
        jnp.concatenate([state_pool[0], u_s[:, None, :]], axis=1)[None, :, 1:],
    )
```

```python
import functools
import math

import jax
import jax.numpy as jnp
import numpy as np
from jax import lax
from jax.experimental import pallas as pl
from jax.experimental.pallas import tpu as pltpu

D_MODEL = 1024
N_HEADS = 8
N_KV_HEADS = 2
HEAD_DIM = 64
GROUP = N_HEADS // N_KV_HEADS
NSA_DIM = N_HEADS * HEAD_DIM
KV_DIM = 2 * N_KV_HEADS * HEAD_DIM
KV_HALF = KV_DIM // 2
L_CMP = 32
L_SEL = 64
CMP_HIDDEN = 2 * HEAD_DIM
TOP_N = 16
WINDOW = 512
ROPE_THETA = 10000.0
POOL_WINDOWS = (2, 4, 8, 16)
POOL_DIM = D_MODEL // 2
POOL_GROUP_DIM = POOL_DIM // len(POOL_WINDOWS)
POOL_BUF = max(POOL_WINDOWS) - 1
POOL_HALO = POOL_BUF + 1
TOP_K = 4
SWIGLU_LIMIT = 7.0
SWIGLU_ALPHA = 1.702
RMS_EPS = 1e-6
NEG = -1e30
FORCED = 1e9

LANES = 128
QPAD_DIM = N_HEADS * LANES
VMEM_LIMIT = 56 * 1024 * 1024
Q_SCALE = HEAD_DIM ** -0.5 * math.log2(math.e)
HEADS_PER_CHAIN = GROUP

BF16 = jnp.bfloat16
F32 = jnp.float32


def _params(n_grid, vmem=VMEM_LIMIT):
    return pltpu.CompilerParams(dimension_semantics=("arbitrary",) * n_grid, vmem_limit_bytes=vmem)


def _rms(x, g):
    r = lax.rsqrt(jnp.mean(x * x, axis=-1, keepdims=True) + RMS_EPS)
    return x * r * g


def _dot(a, b):
    return jnp.dot(a, b, preferred_element_type=F32)


def _dot_nt(a, b):
    return lax.dot_general(a, b, (((1,), (1,)), ((), ())), preferred_element_type=F32)


def _rope_chunk(x, cos, sin_signed):
    lane = lax.broadcasted_iota(jnp.int32, x.shape, 1)
    first = (lane % HEAD_DIM) < (HEAD_DIM // 2)
    swapped = jnp.where(first, pltpu.roll(x, LANES - HEAD_DIM // 2, 1), pltpu.roll(x, HEAD_DIM // 2, 1))
    return x * cos + swapped * sin_signed


def _project(x, g, cos, sin, pos, wq_ref, wkv_ref, wg_ref, wu_ref, outs):
    (q_ref, kvc_ref, kvct_ref, kvst_ref, kvwt_ref, ksb_ref, vsb_ref, kwb_ref, vwb_ref, gates_ref, u_ref) = outs
    xn = _rms(x, g).astype(BF16)
    q = _dot(xn, wq_ref[...])
    for h in range(N_HEADS):
        sl = slice(h * LANES, (h + 1) * LANES)
        q_ref[:, sl] = (_rope_chunk(q[:, sl], cos, sin) * Q_SCALE).astype(BF16)
    kv = _dot(xn, wkv_ref[...])
    n_sel = ksb_ref.shape[1] - KV_HALF
    blk = lax.broadcasted_iota(jnp.int32, (x.shape[0], n_sel), 1)
    for j, t_ref in enumerate((kvct_ref, kvst_ref, kvwt_ref)):
        k = _rope_chunk(kv[:, j * KV_DIM:j * KV_DIM + KV_HALF], cos, sin)
        v = kv[:, j * KV_DIM + KV_HALF:(j + 1) * KV_DIM]
        t_ref[0, 0] = k.T
        t_ref[0, 1] = v.T
        if j == 0:
            kvc_ref[:, :KV_HALF] = k
            kvc_ref[:, KV_HALF:] = v
        elif j == 1:
            ksb_ref[:, :KV_HALF] = k.astype(BF16)
            ksb_ref[:, KV_HALF:] = jnp.where(blk == pos // L_SEL, 1.0, 0.0).astype(BF16)
            vsb_ref[...] = v.astype(BF16)
        else:
            kwb_ref[...] = k.astype(BF16)
            vwb_ref[...] = v.astype(BF16)
    gates_ref[...] = jax.nn.sigmoid(_dot(xn, wg_ref[...]))
    u = _dot(xn, wu_ref[...])
    u_ref[...] = u
    return u


def _inproj_prompt_kernel(x_ref, xh_ref, cos_ref, sin_ref, g_ref, wq_ref, wkv_ref, wg_ref, wu_ref, *outs,
                          tiles_per_seq):
    d_ref = outs[-1]
    i = pl.program_id(0)
    tm = x_ref.shape[0]
    g = g_ref[...]
    seq_tile = i % tiles_per_seq
    pos = seq_tile * tm + lax.broadcasted_iota(jnp.int32, (tm, 1), 0)
    u = _project(x_ref[...], g, cos_ref[...], sin_ref[...], pos, wq_ref, wkv_ref, wg_ref, wu_ref, outs[:-1])
    uh = _dot(_rms(xh_ref[...], g).astype(BF16), wu_ref[...])
    uh = jnp.where(seq_tile == 0, 0.0, uh)
    ext = jnp.concatenate([uh, u], axis=0)
    for gi, w in enumerate(POOL_WINDOWS):
        sl = slice(gi * POOL_GROUP_DIM, (gi + 1) * POOL_GROUP_DIM)
        s = ext[:, sl]
        k = 1
        while k < w:
            s = s + pltpu.roll(s, k, 0)
            k *= 2
        cnt = jnp.minimum(w, pos + 1).astype(F32)
        d_ref[:, sl] = (s[POOL_HALO:, :] / cnt - u[:, sl]).astype(BF16)


def _inproj_sample_kernel(x_ref, sp_ref, cos_ref, sin_ref, g_ref, wq_ref, wkv_ref, wg_ref, wu_ref, *outs,
                          past_len):
    d_ref = outs[-1]
    pos = jnp.full((x_ref.shape[0], 1), past_len, jnp.int32)
    u = _project(x_ref[...], g_ref[...], cos_ref[...], sin_ref[...], pos, wq_ref, wkv_ref, wg_ref, wu_ref,
                 outs[:-1])
    for gi, w in enumerate(POOL_WINDOWS):
        sl = slice(gi * POOL_GROUP_DIM, (gi + 1) * POOL_GROUP_DIM)
        s = u[:, sl]
        for k in range(1, w):
            s = s + sp_ref[POOL_BUF - k][:, sl]
        cnt = float(min(w, past_len + 1))
        d_ref[:, sl] = (s / cnt - u[:, sl]).astype(BF16)


def _inproj_outs(n_seq, rows, tm, n_sel, row_map, t_map):
    t = n_seq * rows
    shapes, specs = [], []

    def add(shape, dtype, block, index_map):
        shapes.append(jax.ShapeDtypeStruct(shape, dtype))
        specs.append(pl.BlockSpec(block, index_map))

    add((t, QPAD_DIM), BF16, (tm, QPAD_DIM), row_map)
    add((t, KV_DIM), F32, (tm, KV_DIM), row_map)
    for _ in range(3):
        add((n_seq, 2, KV_HALF, rows), F32, (1, 2, KV_HALF, tm), t_map)
    add((t, KV_HALF + n_sel), BF16, (tm, KV_HALF + n_sel), row_map)
    for _ in range(3):
        add((t, KV_HALF), BF16, (tm, KV_HALF), row_map)
    add((t, LANES), F32, (tm, LANES), row_map)
    add((t, POOL_DIM), F32, (tm, POOL_DIM), row_map)
    add((t, POOL_DIM), BF16, (tm, POOL_DIM), row_map)
    return tuple(shapes), tuple(specs)


def _weight_specs(shapes):
    return [pl.BlockSpec(s, lambda i, _n=len(s): (0,) * _n) for s in shapes]


def _inproj_prompt(x, cos, sin, g_mix, wq, wkv, wg, wu, batch, seq, tm):
    tiles_per_seq = seq // tm
    halo_blocks = tm // POOL_HALO
    in_specs = [
        pl.BlockSpec((tm, D_MODEL), lambda i: (i, 0)),
        pl.BlockSpec((POOL_HALO, D_MODEL), lambda i: (jnp.maximum(i * halo_blocks - 1, 0), 0)),
        pl.BlockSpec((tm, LANES), lambda i: (i % tiles_per_seq, 0)),
        pl.BlockSpec((tm, LANES), lambda i: (i % tiles_per_seq, 0)),
    ] + _weight_specs([g_mix.shape, wq.shape, wkv.shape, wg.shape, wu.shape])
    out_shape, out_specs = _inproj_outs(batch, seq, tm, seq // L_SEL, lambda i: (i, 0),
                                        lambda i: (i // tiles_per_seq, 0, 0, i % tiles_per_seq))
    return pl.pallas_call(
        functools.partial(_inproj_prompt_kernel, tiles_per_seq=tiles_per_seq),
        grid=(batch * tiles_per_seq,), in_specs=in_specs, out_specs=out_specs,
        out_shape=out_shape, compiler_params=_params(1), name="inproj_prompt",
    )(x, x, cos, sin, g_mix, wq, wkv, wg, wu)


def _inproj_sample(x, sp_t, cos, sin, g_mix, wq, wkv, wg, wu, past_len):
    t = x.shape[0]
    in_specs = [
        pl.BlockSpec((t, D_MODEL), lambda i: (0, 0)),
        pl.BlockSpec(sp_t.shape, lambda i: (0, 0, 0)),
        pl.BlockSpec((t, LANES), lambda i: (0, 0)),
        pl.BlockSpec((t, LANES), lambda i: (0, 0)),
    ] + _weight_specs([g_mix.shape, wq.shape, wkv.shape, wg.shape, wu.shape])
    out_shape, out_specs = _inproj_outs(1, t, t, LANES, lambda i: (0, 0), lambda i: (0, 0, 0, 0))
    return pl.pallas_call(
        functools.partial(_inproj_sample_kernel, past_len=past_len),
        grid=(1,), in_specs=in_specs, out_specs=out_specs,
        out_shape=out_shape, compiler_params=_params(1), name="inproj_sample",
    )(x, sp_t, cos, sin, g_mix, wq, wkv, wg, wu)


def _compress_rows(x_refs, pe_ref, w1_ref, w2_ref, o_ref):
    nb = o_ref.shape[0]
    for c, x_ref in enumerate(x_refs):
        acc = jnp.zeros((nb, w1_ref.shape[3]), F32)
        for l in range(L_CMP):
            xl = x_ref[pl.ds(l, nb, stride=L_CMP), :] + pe_ref[c, l]
            acc = acc + _dot(xl.astype(BF16), w1_ref[c, l])
        hid = acc * jax.nn.sigmoid(acc)
        o_ref[:, c * KV_HALF:(c + 1) * KV_HALF] = _dot(hid.astype(BF16), w2_ref[c]).astype(o_ref.dtype)


def _compress_kernel(xk_ref, xv_ref, pe_ref, w1_ref, w2_ref, o_ref):
    _compress_rows((xk_ref, xv_ref), pe_ref, w1_ref, w2_ref, o_ref)


def _compress(rows, pe, w1, w2, nb_tile):
    n_blocks = rows.shape[0] // L_CMP
    nb_tile = int(np.gcd(nb_tile, n_blocks))
    return pl.pallas_call(
        _compress_kernel,
        grid=(n_blocks // nb_tile,),
        in_specs=[pl.BlockSpec((nb_tile * L_CMP, KV_HALF), lambda i: (i, 0)),
                  pl.BlockSpec((nb_tile * L_CMP, KV_HALF), lambda i: (i, 1))]
        + _weight_specs([pe.shape, w1.shape, w2.shape]),
        out_specs=pl.BlockSpec((nb_tile, KV_DIM), lambda i: (i, 0)),
        out_shape=jax.ShapeDtypeStruct((n_blocks, KV_DIM), BF16),
        compiler_params=_params(1), name="compress",
    )(rows, rows, pe, w1, w2)


def _compress_pages_kernel(x_ref, pe_ref, w1_ref, w2_ref, o_ref, xk_ref, xv_ref):
    n_pages, _, _, page = x_ref.shape

    def body(p, carry):
        r0 = pl.multiple_of(p * page, page)
        xk_ref[pl.ds(r0, page), :] = x_ref[p, 0].T
        xv_ref[pl.ds(r0, page), :] = x_ref[p, 1].T
        return carry

    lax.fori_loop(0, n_pages, body, 0)
    _compress_rows((xk_ref, xv_ref), pe_ref, w1_ref, w2_ref, o_ref)


def _compress_pages(cache_t, pe, w1, w2, pages_tile):
    n_phys, _, _, page = cache_t.shape
    pages_tile = int(np.gcd(pages_tile, n_phys))
    per_page = page // L_CMP
    return pl.pallas_call(
        _compress_pages_kernel,
        grid=(n_phys // pages_tile,),
        in_specs=[pl.BlockSpec((pages_tile, 2, KV_HALF, page), lambda i: (i, 0, 0, 0))]
        + _weight_specs([pe.shape, w1.shape, w2.shape]),
        out_specs=pl.BlockSpec((pages_tile * per_page, KV_DIM), lambda i: (i, 0)),
        out_shape=jax.ShapeDtypeStruct((n_phys * per_page, KV_DIM), BF16),
        scratch_shapes=[pltpu.VMEM((pages_tile * page, KV_HALF), F32)] * 2,
        compiler_params=_params(1), name="compress_pages",
    )(cache_t, pe, w1, w2)


def _stack_heads(q, g):
    return jnp.concatenate([q[:, (GROUP * g + r) * LANES:(GROUP * g + r + 1) * LANES] for r in range(GROUP)], axis=0)


def _masked_softmax(s, mask):
    s = jnp.where(mask, s, NEG)
    e = jnp.exp2(s - jnp.max(s, axis=-1, keepdims=True))
    return e / jnp.sum(e, axis=-1, keepdims=True) * mask.astype(F32)


def _assemble_heads(o_heads):
    chunks = []
    for c in range(N_HEADS // 2):
        g = (2 * c) // GROUP
        a, b = o_heads[2 * c], o_heads[2 * c + 1]
        if g == 1:
            a = pltpu.roll(a, HEAD_DIM, 1)
        else:
            b = pltpu.roll(b, HEAD_DIM, 1)
        lane = lax.broadcasted_iota(jnp.int32, a.shape, 1)
        chunks.append(jnp.where(lane < HEAD_DIM, a, b))
    return jnp.concatenate(chunks, axis=1)


def _select_blocks(score, n_pick):
    n_blk = score.shape[1]
    lane = lax.broadcasted_iota(jnp.int32, score.shape, 1)
    work = score
    picks = []
    for _ in range(n_pick):
        m = jnp.max(work, axis=-1, keepdims=True)
        idx = jnp.min(jnp.where(work == m, lane, n_blk), axis=-1, keepdims=True)
        picks.append(idx)
        work = jnp.where(lane == idx, -jnp.inf, work)
    return picks


def _cmp_attention(qg, kc, vc, pos_rows, n_sel):
    s = _dot_nt(qg, kc)
    j = lax.broadcasted_iota(jnp.int32, s.shape, 1)
    n = 2 * (j % n_sel) + j // n_sel
    mask = (n * L_CMP + (L_CMP - 1)) <= pos_rows
    p = _masked_softmax(s, mask)
    return _dot(p.astype(BF16), vc), p


def _cmp_topk_kernel(q_ref, c_ref, gates_ref, oc_ref, msel_ref, *, n_sel, n_top):
    i = pl.program_id(1)
    tq = q_ref.shape[0]
    q = q_ref[...]
    comp = c_ref[0]
    kc, vc = comp[:, :LANES], comp[:, LANES:]
    gates = gates_ref[...]
    pos = i * tq + lax.broadcasted_iota(jnp.int32, (tq, 1), 0)
    pos_rows = jnp.concatenate([pos] * GROUP, axis=0)
    blk = lax.broadcasted_iota(jnp.int32, (tq, n_sel), 1)
    cur = pos // L_SEL
    forced = (blk == 0) | (blk == cur) | (blk == cur - 1)
    o_heads = []
    for g in range(N_KV_HEADS):
        o, p = _cmp_attention(_stack_heads(q, g), kc, vc, pos_rows, n_sel)
        imp = p[0:tq]
        for r in range(1, GROUP):
            imp = imp + p[r * tq:(r + 1) * tq]
        imp = imp[:, :n_sel] + imp[:, n_sel:]
        score = jnp.where(forced, FORCED, jnp.where(blk < cur, imp, NEG))
        sel = jnp.zeros((tq, n_sel), F32)
        for idx in _select_blocks(score, n_top):
            sel = jnp.where(blk == idx, 1.0, sel)
        sel = jnp.where(score > NEG / 2, sel, 0.0)
        msel_ref[:, g * n_sel:(g + 1) * n_sel] = ((1.0 - sel) * NEG).astype(BF16)
        for r in range(GROUP):
            h = GROUP * g + r
            o_heads.append(o[r * tq:(r + 1) * tq] * gates[:, h:h + 1])
    oc_ref[...] = _assemble_heads(o_heads)


def _cmp_topk_prompt(q, comp, gates, batch, seq, tq):
    n_sel = seq // L_SEL
    n_c = comp.shape[1]
    tiles = seq // tq
    row = lambda b, i: (b * tiles + i, 0)
    return pl.pallas_call(
        functools.partial(_cmp_topk_kernel, n_sel=n_sel, n_top=min(TOP_N, n_sel)),
        grid=(batch, tiles),
        in_specs=[pl.BlockSpec((tq, QPAD_DIM), row),
                  pl.BlockSpec((1, n_c, KV_DIM), lambda b, i: (b, 0, 0)),
                  pl.BlockSpec((tq, LANES), row)],
        out_specs=(pl.BlockSpec((tq, NSA_DIM), row), pl.BlockSpec((tq, N_KV_HEADS * n_sel), row)),
        out_shape=(jax.ShapeDtypeStruct((batch * seq, NSA_DIM), F32),
                   jax.ShapeDtypeStruct((batch * seq, N_KV_HEADS * n_sel), BF16)),
        compiler_params=_params(2), name="cmp_topk_prompt",
    )(q, comp, gates)


def _flash_update(carry, q, k, v, bias):
    m, l, acc = carry
    s = _dot_nt(q, k)
    if bias is not None:
        s = s + bias
    m_new = jnp.maximum(m, jnp.max(s, axis=-1, keepdims=True))
    alpha = jnp.exp2(m - m_new)
    p = jnp.exp2(s - m_new)
    l = alpha * l + jnp.sum(p, axis=-1, keepdims=True)
    acc = alpha * acc + _dot(p.astype(BF16), v)
    return m_new, l, acc


def _sel_win_kernel(q_ref, ks_ref, vs_ref, kw_ref, vw_ref, msel_ref, gates_ref, oc_ref, o_ref, *, n_sel):
    i = pl.program_id(1)
    tq = q_ref.shape[0]
    hpc = HEADS_PER_CHAIN
    n_chain = N_HEADS // hpc
    rows = hpc * tq
    q = q_ref[...]
    gates = gates_ref[...]
    pos = i * tq + lax.broadcasted_iota(jnp.int32, (tq, 1), 0)
    key = lax.broadcasted_iota(jnp.int32, (1, tq), 1)
    d0 = pl.multiple_of(i * tq, tq)
    p0 = pl.multiple_of(jnp.maximum(i - 1, 0) * tq, tq)
    bias_d = jnp.concatenate([jnp.where(i * tq + key <= pos, 0.0, NEG)] * hpc, axis=0)
    in_win = (pos - ((i - 1) * tq + key) < WINDOW) & (i > 0)
    bias_p = jnp.concatenate([jnp.where(in_win, 0.0, NEG)] * hpc, axis=0)
    qcs, qas = [], []
    for c in range(n_chain):
        g = (c * hpc) // GROUP
        qc = jnp.concatenate([q[:, h * LANES:(h + 1) * LANES] for h in range(c * hpc, (c + 1) * hpc)], axis=0)
        msel = jnp.concatenate([msel_ref[:, g * n_sel:(g + 1) * n_sel]] * hpc, axis=0)
        qcs.append(qc)
        qas.append(jnp.concatenate([qc, msel], axis=1))
    init = (jnp.full((rows, 1), NEG, F32), jnp.zeros((rows, 1), F32), jnp.zeros((rows, LANES), F32))

    kd, vd = ks_ref[pl.ds(d0, tq), :], vs_ref[pl.ds(d0, tq), :]
    sel = tuple(_flash_update(init, qas[c], kd, vd, bias_d) for c in range(n_chain))

    def body(kt, carries):
        k0 = pl.multiple_of(kt * tq, tq)
        k, v = ks_ref[pl.ds(k0, tq), :], vs_ref[pl.ds(k0, tq), :]
        return tuple(_flash_update(carries[c], qas[c], k, v, None) for c in range(n_chain))

    sel = lax.fori_loop(0, i, body, sel)
    kd, vd = kw_ref[pl.ds(d0, tq), :], vw_ref[pl.ds(d0, tq), :]
    kp, vp = kw_ref[pl.ds(p0, tq), :], vw_ref[pl.ds(p0, tq), :]
    o_heads = []
    for c in range(n_chain):
        win = _flash_update(_flash_update(init, qcs[c], kd, vd, bias_d), qcs[c], kp, vp, bias_p)
        o_s = sel[c][2] / sel[c][1]
        o_w = win[2] / win[1]
        for r in range(hpc):
            h = c * hpc + r
            rs = slice(r * tq, (r + 1) * tq)
            o_heads.append(o_s[rs] * gates[:, N_HEADS + h:N_HEADS + h + 1]
                           + o_w[rs] * gates[:, 2 * N_HEADS + h:2 * N_HEADS + h + 1])
    o_ref[...] = (_assemble_heads(o_heads) + oc_ref[...]).astype(o_ref.dtype)


def _sel_win_prompt(q, ksb, vsb, kwb, vwb, msel, gates, oc, batch, seq, tq):
    assert tq == WINDOW
    n_sel = seq // L_SEL
    tiles = seq // tq
    row = lambda b, i: (b * tiles + i, 0)
    whole = lambda b, i: (b, 0)
    return pl.pallas_call(
        functools.partial(_sel_win_kernel, n_sel=n_sel),
        grid=(batch, tiles),
        in_specs=[pl.BlockSpec((tq, QPAD_DIM), row),
                  pl.BlockSpec((seq, KV_HALF + n_sel), whole),
                  pl.BlockSpec((seq, KV_HALF), whole),
                  pl.BlockSpec((seq, KV_HALF), whole),
                  pl.BlockSpec((seq, KV_HALF), whole),
                  pl.BlockSpec((tq, N_KV_HEADS * n_sel), row),
                  pl.BlockSpec((tq, LANES), row),
                  pl.BlockSpec((tq, NSA_DIM), row)],
        out_specs=pl.BlockSpec((tq, NSA_DIM), row),
        out_shape=jax.ShapeDtypeStruct((batch * seq, NSA_DIM), BF16),
        compiler_params=_params(2), name="sel_win_prompt",
    )(q, ksb, vsb, kwb, vwb, msel, gates, oc)


def _sample_cmp_kernel(q_ref, c_ref, gates_ref, oc_ref, idx_ref, *, n_blk, n_pick, past_len):
    q = q_ref[0]
    comp = c_ref[0]
    gates = gates_ref[0]
    pos_rows = jnp.full((N_HEADS, 1), past_len, jnp.int32)
    o, p = _cmp_attention(q, comp[:, :LANES], comp[:, LANES:], pos_rows, n_blk)
    oc_ref[0] = o * gates[:, 0:1]
    cur = past_len // L_SEL
    blk = lax.broadcasted_iota(jnp.int32, (N_KV_HEADS, n_blk), 1)
    imp = jnp.concatenate(
        [jnp.sum(p[GROUP * g:GROUP * (g + 1)], axis=0, keepdims=True) for g in range(N_KV_HEADS)], axis=0)
    imp = imp[:, :n_blk] + imp[:, n_blk:]
    forced = (blk == 0) | (blk == cur) | (blk == cur - 1)
    score = jnp.where(forced, FORCED, jnp.where(blk < cur, imp, NEG))
    lane = lax.broadcasted_iota(jnp.int32, (N_KV_HEADS, LANES), 1)
    out = jnp.zeros((N_KV_HEADS, LANES), jnp.int32)
    for k, idx in enumerate(_select_blocks(score, n_pick)):
        out = jnp.where(lane == k, idx, out)
    idx_ref[0] = jnp.concatenate([out, jnp.zeros((N_HEADS - N_KV_HEADS, LANES), jnp.int32)], axis=0)


def _sample_cmp(q3, comp, gates3, n_pick, past_len):
    n_seq = q3.shape[0]
    n_c = comp.shape[1]
    blk3 = lambda b: (b, 0, 0)
    return pl.pallas_call(
        functools.partial(_sample_cmp_kernel, n_blk=n_c // 2, n_pick=n_pick, past_len=past_len),
        grid=(n_seq,),
        in_specs=[pl.BlockSpec((1, N_HEADS, LANES), blk3),
                  pl.BlockSpec((1, n_c, KV_DIM), blk3),
                  pl.BlockSpec((1, N_HEADS, LANES), blk3)],
        out_specs=(pl.BlockSpec((1, N_HEADS, LANES), blk3), pl.BlockSpec((1, N_HEADS, LANES), blk3)),
        out_shape=(jax.ShapeDtypeStruct((n_seq, N_HEADS, LANES), F32),
                   jax.ShapeDtypeStruct((n_seq, N_HEADS, LANES), jnp.int32)),
        compiler_params=_params(1), name="sample_cmp",
    )(q3, comp, gates3)


def _softmax_with_new(s, valid, s_new):
    s = jnp.where(valid, s, NEG)
    m = jnp.maximum(jnp.max(s, axis=-1, keepdims=True), s_new)
    e = jnp.exp2(s - m) * valid.astype(F32)
    e_new = jnp.exp2(s_new - m)
    den = jnp.sum(e, axis=-1, keepdims=True) + e_new
    return e / den, e_new / den


def _sample_sel_win_kernel(pg_ref, hf_ref, q_ref, win_ref, kvs_ref, kvw_ref, gates_ref, oc_ref, cache_ref,
                           o_ref, buf_ref, sem_ref, *, n_pick):
    b = pl.program_id(0)
    n_b = pl.num_programs(0)
    n_dma = N_KV_HEADS * n_pick
    page = buf_ref.shape[-1]

    def page_copy(bb, slot, j):
        return pltpu.make_async_copy(cache_ref.at[pg_ref[bb * n_dma + j]], buf_ref.at[slot, j], sem_ref.at[slot])

    @pl.when(b == 0)
    def _():
        for j in range(n_dma):
            page_copy(0, 0, j).start()

    @pl.when(b + 1 < n_b)
    def _():
        for j in range(n_dma):
            page_copy(b + 1, (b + 1) % 2, j).start()

    q = q_ref[0]
    qf = q.astype(F32)
    gates = gates_ref[0]
    row = lax.broadcasted_iota(jnp.int32, (N_HEADS, LANES), 0)

    def new_key(kv_ref):
        kv = kv_ref[0].astype(BF16).astype(F32)
        return jnp.sum(qf * kv[:, :KV_HALF], axis=-1, keepdims=True), kv[:, KV_HALF:]

    win = win_ref[0]
    kt, vt = win[0].astype(BF16), win[1].astype(BF16)
    s_new, v_new = new_key(kvw_ref)
    lane_w = lax.broadcasted_iota(jnp.int32, (N_HEADS, kt.shape[1]), 1)
    p, p_new = _softmax_with_new(_dot(q, kt), lane_w >= 1, s_new)
    o_w = _dot_nt(p.astype(BF16), vt) + p_new.astype(BF16).astype(F32) * v_new

    slot = b % 2
    for j in range(n_dma):
        page_copy(b, slot, j).wait()
    s_new, v_new = new_key(kvs_ref)
    lane_p = lax.broadcasted_iota(jnp.int32, (N_HEADS, page), 1)
    o_s = jnp.zeros((N_HEADS, LANES), F32)
    for g in range(N_KV_HEADS):
        kts, vts, valids = [], [], []
        for k in range(n_pick):
            j = g * n_pick + k
            pg = buf_ref[slot, j]
            kts.append(pg[0].astype(BF16))
            vts.append(pg[1].astype(BF16))
            half = hf_ref[b * n_dma + j]
            valids.append((lane_p // L_SEL) == half)
        kt_all = jnp.concatenate(kts, axis=1)
        vt_all = jnp.concatenate(vts, axis=1)
        p, p_new = _softmax_with_new(_dot(q, kt_all), jnp.concatenate(valids, axis=1), s_new)
        o_g = _dot_nt(p.astype(BF16), vt_all) + p_new.astype(BF16).astype(F32) * v_new
        o_s = jnp.where(row // GROUP == g, o_g, o_s)
    o_ref[0] = oc_ref[0] + o_s * gates[:, 1:2] + o_w * gates[:, 2:3]


def _sample_sel_win(pages, halves, q3, win_t, kvs, kvw, gates3, oc, cache_t, n_pick):
    n_seq = q3.shape[0]
    n_dma = N_KV_HEADS * n_pick
    blk3 = lambda b, pg, hf: (b, 0, 0)
    grid_spec = pltpu.PrefetchScalarGridSpec(
        num_scalar_prefetch=2, grid=(n_seq,),
        in_specs=[pl.BlockSpec((1, N_HEADS, LANES), blk3),
                  pl.BlockSpec((1,) + win_t.shape[1:], lambda b, pg, hf: (b, 0, 0, 0)),
                  pl.BlockSpec((1, 1, KV_DIM), blk3),
                  pl.BlockSpec((1, 1, KV_DIM), blk3),
                  pl.BlockSpec((1, N_HEADS, LANES), blk3),
                  pl.BlockSpec((1, N_HEADS, LANES), blk3),
                  pl.BlockSpec(memory_space=pl.ANY)],
        out_specs=pl.BlockSpec((1, N_HEADS, LANES), blk3),
        scratch_shapes=[pltpu.VMEM((2, n_dma) + cache_t.shape[1:], F32), pltpu.SemaphoreType.DMA((2,))],
    )
    return pl.pallas_call(
        functools.partial(_sample_sel_win_kernel, n_pick=n_pick),
        grid_spec=grid_spec,
        out_shape=jax.ShapeDtypeStruct((n_seq, N_HEADS, LANES), F32),
        compiler_params=_params(1), name="sample_sel_win",
    )(pages, halves, q3, win_t, kvs, kvw, gates3, oc, cache_t)


def _tail_kernel(x_ref, attn_ref, d_ref, xs_ref, attns_ref, ds_ref, gmix_ref, wgm_ref, wpool_ref, pscale_ref,
                 wbn_ref, wbp_ref, wout_ref, gffn_ref, wr_ref, br_ref, h_ref, xn2_ref, route_ref, cout_ref, run_ref,
                 *, n_experts):
    i = pl.program_id(0)
    is_sample = i == pl.num_programs(0) - 1
    x = jnp.where(is_sample, xs_ref[...], x_ref[...])
    attn = jnp.where(is_sample, attns_ref[...], attn_ref[...])
    d = jnp.where(is_sample, ds_ref[...], d_ref[...])
    tm = x.shape[0]
    xn = _rms(x, gmix_ref[...]).astype(BF16)
    gm = jax.nn.sigmoid(_dot(xn, wgm_ref[...]))
    pool = jnp.concatenate(
        [_dot(d[:, gi * POOL_GROUP_DIM:(gi + 1) * POOL_GROUP_DIM], wpool_ref[gi]) for gi in range(len(POOL_WINDOWS))],
        axis=1) * pscale_ref[...]
    merged = (gm[:, :D_MODEL] * _dot(attn, wbn_ref[...])
              + gm[:, D_MODEL:] * _dot(pool.astype(BF16), wbp_ref[...]))
    h = x + _dot(merged.astype(BF16), wout_ref[...])
    h_ref[...] = h
    xn2 = _rms(h, gffn_ref[...]).astype(BF16)
    xn2_ref[...] = xn2
    logits = _dot(xn2, wr_ref[...]) + br_ref[...]
    lane = lax.broadcasted_iota(jnp.int32, (tm, LANES), 1)
    work = jnp.where(lane < n_experts, logits, -jnp.inf)
    vals, onehots = [], []
    for _ in range(TOP_K):
        m = jnp.max(work, axis=-1, keepdims=True)
        idx = jnp.min(jnp.where(work == m, lane, LANES), axis=-1, keepdims=True)
        vals.append(m)
        onehots.append(lane == idx)
        work = jnp.where(lane == idx, -jnp.inf, work)
    es = [jnp.exp(v - vals[0]) for v in vals]
    tot = es[0] + es[1] + es[2] + es[3]

    @pl.when(i == 0)
    def _():
        run_ref[...] = jnp.zeros(run_ref.shape, F32)

    chosen = jnp.zeros((tm, LANES), F32)
    for k in range(TOP_K):
        chosen = jnp.where(onehots[k], 1.0, chosen)
    earlier = lax.broadcasted_iota(jnp.int32, (tm, tm), 0) > lax.broadcasted_iota(jnp.int32, (tm, tm), 1)
    before = run_ref[...] + _dot(jnp.where(earlier, 1.0, 0.0).astype(BF16), chosen.astype(BF16))
    route = jnp.zeros((tm, LANES), F32)
    lane_f = lane.astype(F32)
    for k in range(TOP_K):
        e_k = jnp.sum(jnp.where(onehots[k], lane_f, 0.0), axis=-1, keepdims=True)
        r_k = jnp.sum(jnp.where(onehots[k], before, 0.0), axis=-1, keepdims=True)
        route = jnp.where(lane == k, e_k, route)
        route = jnp.where(lane == TOP_K + k, es[k] / tot, route)
        route = jnp.where(lane == 2 * TOP_K + k, r_k, route)
    route_ref[...] = route
    run_ref[...] = run_ref[...] + jnp.sum(chosen, axis=0, keepdims=True)
    cout_ref[...] = run_ref[...]


def _tail(x, attn, d, xs, attn_s, d_s, weights, n_experts, tm):
    n_p = x.shape[0] // tm
    t = (n_p + 1) * tm
    row = lambda i: (i, 0)
    prompt = lambda i: (jnp.minimum(i, n_p - 1), 0)
    fixed = lambda i: (0, 0)
    return pl.pallas_call(
        functools.partial(_tail_kernel, n_experts=n_experts),
        grid=(n_p + 1,),
        in_specs=[pl.BlockSpec((tm, D_MODEL), prompt), pl.BlockSpec((tm, NSA_DIM), prompt),
                  pl.BlockSpec((tm, POOL_DIM), prompt), pl.BlockSpec((tm, D_MODEL), fixed),
                  pl.BlockSpec((tm, NSA_DIM), fixed), pl.BlockSpec((tm, POOL_DIM), fixed)]
        + _weight_specs([w.shape for w in weights]),
        out_specs=(pl.BlockSpec((tm, D_MODEL), row), pl.BlockSpec((tm, D_MODEL), row),
                   pl.BlockSpec((tm, LANES), row), pl.BlockSpec((1, LANES), fixed)),
        out_shape=(jax.ShapeDtypeStruct((t, D_MODEL), F32), jax.ShapeDtypeStruct((t, D_MODEL), BF16),
                   jax.ShapeDtypeStruct((t, LANES), F32), jax.ShapeDtypeStruct((1, LANES), F32)),
        scratch_shapes=[pltpu.VMEM((1, LANES), F32)],
        compiler_params=_params(1), name="tail",
    )(x, attn, d, xs, attn_s, d_s, *weights)


def _moe_kernel(te_ref, tv_ref, x_ref, wg_ref, bg_ref, wu_ref, bu_ref, wd_ref, bd_ref, y_ref):
    t = pl.program_id(0)

    @pl.when(tv_ref[t] > 0)
    def _():
        x = x_ref[...]
        gate = jnp.minimum(_dot(x, wg_ref[0]) + bg_ref[0], SWIGLU_LIMIT)
        up = jnp.clip(_dot(x, wu_ref[0]) + bu_ref[0], -SWIGLU_LIMIT, SWIGLU_LIMIT)
        hid = (up + 1.0) * gate * jax.nn.sigmoid(SWIGLU_ALPHA * gate)
        y_ref[...] = (_dot(hid.astype(BF16), wd_ref[0]) + bd_ref[0]).astype(y_ref.dtype)

    @pl.when(tv_ref[t] == 0)
    def _():
        y_ref[...] = jnp.zeros(y_ref.shape, y_ref.dtype)


def _moe(tile_e, tile_v, xs, wg, bg, wu, bu, wd, bd, tme):
    p_pad = xs.shape[0]
    d_e = wg.shape[2]
    wspec = lambda shape: pl.BlockSpec((1,) + shape, lambda t, te, tv: (te[t], 0, 0))
    grid_spec = pltpu.PrefetchScalarGridSpec(
        num_scalar_prefetch=2, grid=(p_pad // tme,),
        in_specs=[pl.BlockSpec((tme, D_MODEL), lambda t, te, tv: (t, 0)),
                  wspec((D_MODEL, d_e)), wspec((1, d_e)), wspec((D_MODEL, d_e)), wspec((1, d_e)),
                  wspec((d_e, D_MODEL)), wspec((1, D_MODEL))],
        out_specs=pl.BlockSpec((tme, D_MODEL), lambda t, te, tv: (t, 0)),
    )
    return pl.pallas_call(
        _moe_kernel, grid_spec=grid_spec,
        out_shape=jax.ShapeDtypeStruct((p_pad, D_MODEL), BF16),
        compiler_params=_params(1), name="moe_experts",
    )(tile_e, tile_v, xs, wg, bg, wu, bu, wd, bd)


def _final_kernel(h_ref, y_ref, route_ref, g_ref, o_ref):
    route = route_ref[...]
    acc = h_ref[...]
    for k in range(TOP_K):
        acc = acc + y_ref[k].astype(F32) * route[:, TOP_K + k:TOP_K + k + 1]
    o_ref[...] = _rms(acc, g_ref[...])


def _final(h, yg, route, g_final, tm, first, n_tiles):
    src = lambda i: (i + first, 0)
    return pl.pallas_call(
        _final_kernel, grid=(n_tiles,),
        in_specs=[pl.BlockSpec((tm, D_MODEL), src), pl.BlockSpec((TOP_K, tm, D_MODEL), lambda i: (0, i + first, 0)),
                  pl.BlockSpec((tm, LANES), src), pl.BlockSpec((1, D_MODEL), lambda i: (0, 0))],
        out_specs=pl.BlockSpec((tm, D_MODEL), lambda i: (i, 0)),
        out_shape=jax.ShapeDtypeStruct((n_tiles * tm, D_MODEL), F32),
        compiler_params=_params(1), name="final_norm",
    )(h, yg, route, g_final)


def _rope_tables(pos):
    half = HEAD_DIM // 2
    inv = ROPE_THETA ** (-jnp.arange(half, dtype=F32) / half)
    ang = pos.astype(F32)[:, None] * inv[None, :]
    cos, sin = jnp.cos(ang), jnp.sin(ang)
    cos = jnp.concatenate([cos, cos] * (LANES // HEAD_DIM), axis=1)
    sin = jnp.concatenate([-sin, sin] * (LANES // HEAD_DIM), axis=1)
    return cos, sin


def _split_w_in(w_in):
    o_q = NSA_DIM
    o_g = o_q + 3 * KV_DIM
    o_u = o_g + 3 * N_HEADS
    o_m = o_u + POOL_DIM
    wq = w_in[:, :o_q].reshape(D_MODEL, N_HEADS, HEAD_DIM)
    zeros = jnp.zeros_like(wq)
    lo = jnp.concatenate([wq, zeros], axis=2)
    hi = jnp.concatenate([zeros, wq], axis=2)
    in_hi = (jnp.arange(N_HEADS) // GROUP == 1)[None, :, None]
    wq_pad = jnp.where(in_hi, hi, lo).reshape(D_MODEL, QPAD_DIM)
    wkv = w_in[:, o_q:o_g]
    wg = w_in[:, o_g:o_u].reshape(D_MODEL, N_HEADS, 3).transpose(0, 2, 1).reshape(D_MODEL, 3 * N_HEADS)
    wg = jnp.pad(wg, ((0, 0), (0, LANES - 3 * N_HEADS)))
    wu = w_in[:, o_u:o_m]
    wgm = w_in[:, o_m:]
    return [w.astype(BF16) for w in (wq_pad, wkv, wg, wu, wgm)]


def _compress_weights(cmp_pe, cmp_w1, cmp_w2):
    eye = jnp.eye(N_KV_HEADS, dtype=F32)
    w1 = jnp.einsum('cldh,gf->clgdfh', cmp_w1, eye).reshape(2, L_CMP, KV_HALF, N_KV_HEADS * CMP_HIDDEN)
    w2 = jnp.einsum('chd,gf->cghfd', cmp_w2, eye).reshape(2, N_KV_HEADS * CMP_HIDDEN, KV_HALF)
    pe = jnp.broadcast_to(cmp_pe.transpose(1, 0, 2)[:, :, None, :], (2, L_CMP, N_KV_HEADS, HEAD_DIM))
    return pe.reshape(2, L_CMP, 1, KV_HALF), w1.astype(BF16), w2.astype(BF16)


def _even_odd(comp, n_seq):
    n_c = comp.shape[0] // n_seq
    return comp.reshape(n_seq, n_c // 2, 2, KV_DIM).transpose(0, 2, 1, 3).reshape(n_seq, n_c, KV_DIM)


def _fix_sample_heads(o3):
    n_seq = o3.shape[0]
    o4 = o3.reshape(n_seq, N_KV_HEADS, GROUP, N_KV_HEADS, HEAD_DIM)
    return jnp.concatenate([o4[:, g, :, g, :] for g in range(N_KV_HEADS)], axis=1).reshape(n_seq, NSA_DIM)


def _gates_by_head(gates):
    g = gates[:, :3 * N_HEADS].reshape(-1, 3, N_HEADS).transpose(0, 2, 1)
    return jnp.pad(g, ((0, 0), (0, 0), (0, LANES - 3)))


def _key_minor(x):
    n, p = x.shape[:2]
    return x.transpose(0, 2, 3, 4, 1).reshape(n, 2, KV_HALF, p)


def _from_key_minor(xt):
    n, _, _, p = xt.shape
    return xt.reshape(n, 2, N_KV_HEADS, HEAD_DIM, p).transpose(0, 4, 1, 2, 3)


def _expert_order(route, counts, n_experts, tme):
    t = route.shape[0]
    n_pairs = t * TOP_K
    e = route[:, :TOP_K].astype(jnp.int32)
    rank = route[:, 2 * TOP_K:3 * TOP_K].astype(jnp.int32)
    counts = counts[0, :n_experts].astype(jnp.int32)
    padded = ((counts + tme - 1) // tme) * tme
    ends = jnp.cumsum(padded)
    off = ends - padded
    pad_before = off - (jnp.cumsum(counts) - counts)
    experts = jnp.arange(n_experts, dtype=jnp.int32)
    pair_slot = rank + jnp.sum(jnp.where(e[:, :, None] == experts, off, 0), axis=-1)
    n_tiles = -(-n_pairs // tme) + n_experts
    p_pad = n_tiles * tme
    tok = jnp.broadcast_to(jnp.arange(t, dtype=jnp.int32)[:, None], (t, TOP_K))
    _, tok_sorted = lax.sort((pair_slot.reshape(n_pairs), tok.reshape(n_pairs)), num_keys=1)
    slot = jnp.arange(p_pad, dtype=jnp.int32)
    slot_e = jnp.sum((slot[:, None] >= ends[None, :]).astype(jnp.int32), axis=-1)
    slot_e = jnp.minimum(slot_e, n_experts - 1)
    slot_pad = jnp.sum(jnp.where(slot_e[:, None] == experts, pad_before, 0), axis=-1)
    src_tok = tok_sorted[jnp.clip(slot - slot_pad, 0, n_pairs - 1)]
    tile_start = jnp.arange(n_tiles, dtype=jnp.int32) * tme
    tile_e = jnp.minimum(jnp.sum((tile_start[:, None] >= ends[None, :]).astype(jnp.int32), axis=-1), n_experts - 1)
    tile_v = (tile_start < ends[-1]).astype(jnp.int32)
    return pair_slot, src_tok, tile_e, tile_v


def kernel(x_prompt, x_sample, cache_kv_cmp, cache_kv_sel, state_kv_win, state_pool, page_table, g_mix, w_in, cmp_pe, cmp_w1, cmp_w2, w_pool, pool_scale, w_br_nsa, w_br_pool, w_out, g_ffn, w_router, b_router, w_gate, b_gate, w_up, b_up, w_down, b_down, g_final):
    batch, seq, _ = x_prompt.shape
    n_seq, dec_seq, _ = x_sample.shape
    depth = g_mix.shape[0]
    assert depth == 1 and dec_seq == 1
    n_phys, page_size = cache_kv_cmp.shape[1:3]
    n_pages = page_table.shape[1]
    past_len = n_pages * page_size
    n_experts = w_router.shape[2]
    t_p = batch * seq
    tm = min(512, seq)

    wq, wkv, wg, wu, wgm = _split_w_in(w_in[0])
    gmix = g_mix[0][None, :]
    pe, w1, w2 = _compress_weights(cmp_pe[0], cmp_w1[0], cmp_w2[0])
    cos_p, sin_p = _rope_tables(jnp.arange(seq, dtype=jnp.int32))
    cos_s, sin_s = _rope_tables(jnp.full((n_seq,), past_len, jnp.int32))
    wr = jnp.pad(w_router[0], ((0, 0), (0, LANES - n_experts))).astype(BF16)
    br = jnp.pad(b_router[0], (0, LANES - n_experts))[None, :]
    tail_w = [gmix, wgm, w_pool[0].astype(BF16), pool_scale[0][None, :], w_br_nsa[0].astype(BF16),
              w_br_pool[0].astype(BF16), w_out[0].astype(BF16), g_ffn[0][None, :], wr, br]

    xp = x_prompt.reshape(t_p, D_MODEL)
    (q_p, kvc_p, kvct_p, kvst_p, kvwt_p, ksb_p, vsb_p, kwb_p, vwb_p, gates_p, u_p, d_p) = _inproj_prompt(
        xp, cos_p, sin_p, gmix, wq, wkv, wg, wu, batch, seq, tm)
    comp_p = _even_odd(_compress(kvc_p, pe, w1, w2, 256), batch)
    tq = min(256, seq)
    oc_p, msel_p = _cmp_topk_prompt(q_p, comp_p, gates_p, batch, seq, tq)
    attn_p = _sel_win_prompt(q_p, ksb_p, vsb_p, kwb_p, vwb_p, msel_p, gates_p, oc_p, batch, seq, WINDOW)

    xs = x_sample.reshape(n_seq, D_MODEL)
    sp_t = state_pool[0].transpose(1, 0, 2)
    (q_s, kvc_s, _, kvst_s, kvwt_s, _, _, _, _, gates_s, u_s, d_s) = _inproj_sample(
        xs, sp_t, cos_s, sin_s, gmix, wq, wkv, wg, wu, past_len)
    kvs_s = kvst_s.reshape(KV_DIM, n_seq).T
    kvw_s = kvwt_s.reshape(KV_DIM, n_seq).T
    blocks_per_page = page_size // L_CMP
    cache_cmp_t = _key_minor(cache_kv_cmp[0])
    comp_all = _compress_pages(cache_cmp_t, pe, w1, w2, 64)
    comp_s = comp_all.reshape(n_phys, blocks_per_page, KV_DIM)[page_table].reshape(n_seq * n_pages * blocks_per_page, KV_DIM)
    comp_s = _even_odd(comp_s, n_seq)
    q3 = q_s.reshape(n_seq, N_HEADS, LANES)
    gates3 = _gates_by_head(gates_s)
    n_blk = past_len // L_SEL
    n_pick = min(TOP_N, n_blk + 1) - 1
    oc_s, idx_s = _sample_cmp(q3, comp_s, gates3, n_pick, past_len)
    top_idx = idx_s[:, :N_KV_HEADS, :n_pick]
    sel_per_page = page_size // L_SEL
    pages_b = jnp.broadcast_to(page_table[:, None, :], (n_seq, N_KV_HEADS, n_pages))
    phys = jnp.take_along_axis(pages_b, top_idx // sel_per_page, axis=2).reshape(-1)
    halves = (top_idx % sel_per_page).reshape(-1)
    w_buf = state_kv_win.shape[2]
    assert w_buf == WINDOW
    win_t = _key_minor(state_kv_win[0])
    o3 = _sample_sel_win(phys, halves, q3, win_t, kvs_s[:, None, :], kvw_s[:, None, :], gates3, oc_s,
                         _key_minor(cache_kv_sel[0]), n_pick)
    attn_s = _fix_sample_heads(o3).astype(BF16)

    assert n_seq <= tm
    pad_rows = lambda a: jnp.pad(a, ((0, tm - n_seq), (0, 0)))
    h_all, xn2_all, route, counts = _tail(xp, attn_p, d_p, pad_rows(xs), pad_rows(attn_s), pad_rows(d_s),
                                          tail_w, n_experts, tm)
    tme = 512
    pair_slot, src_tok, tile_e, tile_v = _expert_order(route, counts, n_experts, tme)
    y_sorted = _moe(tile_e, tile_v, xn2_all[src_tok], w_gate[0].astype(BF16), b_gate[0][:, None, :],
                    w_up[0].astype(BF16), b_up[0][:, None, :], w_down[0].astype(BF16), b_down[0][:, None, :], tme)
    y_pairs = y_sorted[pair_slot.T]
    gfin = g_final[None, :]
    n_p = t_p // tm
    y_p = _final(h_all, y_pairs, route, gfin, tm, 0, n_p)
    y_s = _final(h_all, y_pairs, route, gfin, tm, n_p, 1)[:n_seq]

    w_p = min(WINDOW, seq)
    kv_row = (2, N_KV_HEADS, HEAD_DIM)
    new_col = kvwt_s.reshape(2, KV_HALF, n_seq).transpose(2, 0, 1)[..., None]
    win_new_t = jnp.concatenate([win_t[..., 1:], new_col], axis=-1)
    return (
        y_p.reshape(batch, seq, D_MODEL),
        y_s.reshape(n_seq, 1, D_MODEL),
        _from_key_minor(kvct_p)[None],
        kvc_s.reshape((1, n_seq, 1) + kv_row),
        _from_key_minor(kvst_p)[None],
        kvs_s.reshape((1, n_seq, 1) + kv_row),
        _from_key_minor(kvwt_p[..., seq - w_p:])[None],
        _from_key_minor(win_new_t)[None],
        u_p.reshape(batch, seq, POOL_DIM)[None, :, seq - POOL_BUF:],
        jnp.concatenate([state_pool[0], u_s[:, None, :]], axis=1)[None, :, 1:],
    )
```

```python
import functools
import math

import jax
import jax.numpy as jnp
import numpy as np
from jax import lax
from jax.experimental import pallas as pl
from jax.experimental.pallas import tpu as pltpu

D_MODEL = 1024
N_HEADS = 8
N_KV_HEADS = 2
HEAD_DIM = 64
GROUP = N_HEADS // N_KV_HEADS
NSA_DIM = N_HEADS * HEAD_DIM
KV_DIM = 2 * N_KV_HEADS * HEAD_DIM
KV_HALF = KV_DIM // 2
L_CMP = 32
L_SEL = 64
CMP_HIDDEN = 2 * HEAD_DIM
TOP_N = 16
WINDOW = 512
ROPE_THETA = 10000.0
POOL_WINDOWS = (2, 4, 8, 16)
POOL_DIM = D_MODEL // 2
POOL_GROUP_DIM = POOL_DIM // len(POOL_WINDOWS)
POOL_BUF = max(POOL_WINDOWS) - 1
POOL_HALO = POOL_BUF + 1
TOP_K = 4
SWIGLU_LIMIT = 7.0
SWIGLU_ALPHA = 1.702
RMS_EPS = 1e-6
NEG = -1e30
FORCED = 1e9

LANES = 128
QPAD_DIM = N_HEADS * LANES
VMEM_LIMIT = 56 * 1024 * 1024
Q_SCALE = HEAD_DIM ** -0.5 * math.log2(math.e)
PAIR_UNROLL = 8
HEADS_PER_CHAIN = GROUP

BF16 = jnp.bfloat16
F32 = jnp.float32


def _params(n_grid, vmem=VMEM_LIMIT):
    return pltpu.CompilerParams(dimension_semantics=("arbitrary",) * n_grid, vmem_limit_bytes=vmem)


def _rms(x, g):
    r = lax.rsqrt(jnp.mean(x * x, axis=-1, keepdims=True) + RMS_EPS)
    return x * r * g


def _dot(a, b):
    return jnp.dot(a, b, preferred_element_type=F32)


def _dot_nt(a, b):
    return lax.dot_general(a, b, (((1,), (1,)), ((), ())), preferred_element_type=F32)


def _rope_chunk(x, cos, sin_signed):
    lane = lax.broadcasted_iota(jnp.int32, x.shape, 1)
    first = (lane % HEAD_DIM) < (HEAD_DIM // 2)
    swapped = jnp.where(first, pltpu.roll(x, LANES - HEAD_DIM // 2, 1), pltpu.roll(x, HEAD_DIM // 2, 1))
    return x * cos + swapped * sin_signed


def _project(x, g, cos, sin, pos, wq_ref, wkv_ref, wg_ref, wu_ref, outs):
    (q_ref, kvc_ref, kvct_ref, kvst_ref, kvwt_ref, ksb_ref, vsb_ref, kwb_ref, vwb_ref, gates_ref, u_ref) = outs
    xn = _rms(x, g).astype(BF16)
    q = _dot(xn, wq_ref[...])
    for h in range(N_HEADS):
        sl = slice(h * LANES, (h + 1) * LANES)
        q_ref[:, sl] = (_rope_chunk(q[:, sl], cos, sin) * Q_SCALE).astype(BF16)
    kv = _dot(xn, wkv_ref[...])
    n_sel = ksb_ref.shape[1] - KV_HALF
    blk = lax.broadcasted_iota(jnp.int32, (x.shape[0], n_sel), 1)
    for j, t_ref in enumerate((kvct_ref, kvst_ref, kvwt_ref)):
        k = _rope_chunk(kv[:, j * KV_DIM:j * KV_DIM + KV_HALF], cos, sin)
        v = kv[:, j * KV_DIM + KV_HALF:(j + 1) * KV_DIM]
        t_ref[0, 0] = k.T
        t_ref[0, 1] = v.T
        if j == 0:
            kvc_ref[:, :KV_HALF] = k
            kvc_ref[:, KV_HALF:] = v
        elif j == 1:
            ksb_ref[:, :KV_HALF] = k.astype(BF16)
            ksb_ref[:, KV_HALF:] = jnp.where(blk == pos // L_SEL, 1.0, 0.0).astype(BF16)
            vsb_ref[...] = v.astype(BF16)
        else:
            kwb_ref[...] = k.astype(BF16)
            vwb_ref[...] = v.astype(BF16)
    gates_ref[...] = jax.nn.sigmoid(_dot(xn, wg_ref[...]))
    u = _dot(xn, wu_ref[...])
    u_ref[...] = u
    return u


def _inproj_prompt_kernel(x_ref, xh_ref, cos_ref, sin_ref, g_ref, wq_ref, wkv_ref, wg_ref, wu_ref, *outs,
                          tiles_per_seq):
    d_ref = outs[-1]
    i = pl.program_id(0)
    tm = x_ref.shape[0]
    g = g_ref[...]
    seq_tile = i % tiles_per_seq
    pos = seq_tile * tm + lax.broadcasted_iota(jnp.int32, (tm, 1), 0)
    u = _project(x_ref[...], g, cos_ref[...], sin_ref[...], pos, wq_ref, wkv_ref, wg_ref, wu_ref, outs[:-1])
    uh = _dot(_rms(xh_ref[...], g).astype(BF16), wu_ref[...])
    uh = jnp.where(seq_tile == 0, 0.0, uh)
    ext = jnp.concatenate([uh, u], axis=0)
    for gi, w in enumerate(POOL_WINDOWS):
        sl = slice(gi * POOL_GROUP_DIM, (gi + 1) * POOL_GROUP_DIM)
        s = ext[:, sl]
        k = 1
        while k < w:
            s = s + pltpu.roll(s, k, 0)
            k *= 2
        cnt = jnp.minimum(w, pos + 1).astype(F32)
        d_ref[:, sl] = (s[POOL_HALO:, :] / cnt - u[:, sl]).astype(BF16)


def _inproj_sample_kernel(x_ref, sp_ref, cos_ref, sin_ref, g_ref, wq_ref, wkv_ref, wg_ref, wu_ref, *outs,
                          past_len):
    d_ref = outs[-1]
    pos = jnp.full((x_ref.shape[0], 1), past_len, jnp.int32)
    u = _project(x_ref[...], g_ref[...], cos_ref[...], sin_ref[...], pos, wq_ref, wkv_ref, wg_ref, wu_ref,
                 outs[:-1])
    for gi, w in enumerate(POOL_WINDOWS):
        sl = slice(gi * POOL_GROUP_DIM, (gi + 1) * POOL_GROUP_DIM)
        s = u[:, sl]
        for k in range(1, w):
            s = s + sp_ref[POOL_BUF - k][:, sl]
        cnt = float(min(w, past_len + 1))
        d_ref[:, sl] = (s / cnt - u[:, sl]).astype(BF16)


def _inproj_outs(n_seq, rows, tm, n_sel, row_map, t_map):
    t = n_seq * rows
    shapes, specs = [], []

    def add(shape, dtype, block, index_map):
        shapes.append(jax.ShapeDtypeStruct(shape, dtype))
        specs.append(pl.BlockSpec(block, index_map))

    add((t, QPAD_DIM), BF16, (tm, QPAD_DIM), row_map)
    add((t, KV_DIM), F32, (tm, KV_DIM), row_map)
    for _ in range(3):
        add((n_seq, 2, KV_HALF, rows), F32, (1, 2, KV_HALF, tm), t_map)
    add((t, KV_HALF + n_sel), BF16, (tm, KV_HALF + n_sel), row_map)
    for _ in range(3):
        add((t, KV_HALF), BF16, (tm, KV_HALF), row_map)
    add((t, LANES), F32, (tm, LANES), row_map)
    add((t, POOL_DIM), F32, (tm, POOL_DIM), row_map)
    add((t, POOL_DIM), BF16, (tm, POOL_DIM), row_map)
    return tuple(shapes), tuple(specs)


def _weight_specs(shapes):
    return [pl.BlockSpec(s, lambda i, _n=len(s): (0,) * _n) for s in shapes]


def _inproj_prompt(x, cos, sin, g_mix, wq, wkv, wg, wu, batch, seq, tm):
    tiles_per_seq = seq // tm
    halo_blocks = tm // POOL_HALO
    in_specs = [
        pl.BlockSpec((tm, D_MODEL), lambda i: (i, 0)),
        pl.BlockSpec((POOL_HALO, D_MODEL), lambda i: (jnp.maximum(i * halo_blocks - 1, 0), 0)),
        pl.BlockSpec((tm, LANES), lambda i: (i % tiles_per_seq, 0)),
        pl.BlockSpec((tm, LANES), lambda i: (i % tiles_per_seq, 0)),
    ] + _weight_specs([g_mix.shape, wq.shape, wkv.shape, wg.shape, wu.shape])
    out_shape, out_specs = _inproj_outs(batch, seq, tm, seq // L_SEL, lambda i: (i, 0),
                                        lambda i: (i // tiles_per_seq, 0, 0, i % tiles_per_seq))
    return pl.pallas_call(
        functools.partial(_inproj_prompt_kernel, tiles_per_seq=tiles_per_seq),
        grid=(batch * tiles_per_seq,), in_specs=in_specs, out_specs=out_specs,
        out_shape=out_shape, compiler_params=_params(1), name="inproj_prompt",
    )(x, x, cos, sin, g_mix, wq, wkv, wg, wu)


def _inproj_sample(x, sp_t, cos, sin, g_mix, wq, wkv, wg, wu, past_len):
    t = x.shape[0]
    in_specs = [
        pl.BlockSpec((t, D_MODEL), lambda i: (0, 0)),
        pl.BlockSpec(sp_t.shape, lambda i: (0, 0, 0)),
        pl.BlockSpec((t, LANES), lambda i: (0, 0)),
        pl.BlockSpec((t, LANES), lambda i: (0, 0)),
    ] + _weight_specs([g_mix.shape, wq.shape, wkv.shape, wg.shape, wu.shape])
    out_shape, out_specs = _inproj_outs(1, t, t, LANES, lambda i: (0, 0), lambda i: (0, 0, 0, 0))
    return pl.pallas_call(
        functools.partial(_inproj_sample_kernel, past_len=past_len),
        grid=(1,), in_specs=in_specs, out_specs=out_specs,
        out_shape=out_shape, compiler_params=_params(1), name="inproj_sample",
    )(x, sp_t, cos, sin, g_mix, wq, wkv, wg, wu)


def _compress_rows(x_refs, pe_ref, w1_ref, w2_ref, o_ref):
    nb = o_ref.shape[0]
    for c, x_ref in enumerate(x_refs):
        acc = jnp.zeros((nb, w1_ref.shape[3]), F32)
        for l in range(L_CMP):
            xl = x_ref[pl.ds(l, nb, stride=L_CMP), :] + pe_ref[c, l]
            acc = acc + _dot(xl.astype(BF16), w1_ref[c, l])
        hid = acc * jax.nn.sigmoid(acc)
        o_ref[:, c * KV_HALF:(c + 1) * KV_HALF] = _dot(hid.astype(BF16), w2_ref[c]).astype(o_ref.dtype)


def _compress_kernel(xk_ref, xv_ref, pe_ref, w1_ref, w2_ref, o_ref):
    _compress_rows((xk_ref, xv_ref), pe_ref, w1_ref, w2_ref, o_ref)


def _compress(rows, pe, w1, w2, nb_tile):
    n_blocks = rows.shape[0] // L_CMP
    nb_tile = int(np.gcd(nb_tile, n_blocks))
    return pl.pallas_call(
        _compress_kernel,
        grid=(n_blocks // nb_tile,),
        in_specs=[pl.BlockSpec((nb_tile * L_CMP, KV_HALF), lambda i: (i, 0)),
                  pl.BlockSpec((nb_tile * L_CMP, KV_HALF), lambda i: (i, 1))]
        + _weight_specs([pe.shape, w1.shape, w2.shape]),
        out_specs=pl.BlockSpec((nb_tile, KV_DIM), lambda i: (i, 0)),
        out_shape=jax.ShapeDtypeStruct((n_blocks, KV_DIM), BF16),
        compiler_params=_params(1), name="compress",
    )(rows, rows, pe, w1, w2)


def _compress_pages_kernel(x_ref, pet_ref, perm_ref, w1_ref, w2_ref, o_ref, xs_ref):
    n_pairs = x_ref.shape[0] // 2
    nb = o_ref.shape[0]
    perm = perm_ref[...]

    unroll = int(np.gcd(n_pairs, PAIR_UNROLL))

    def body(it, carry):
        for u in range(unroll):
            pp = it * unroll + u
            for c in range(2):
                xt = jnp.concatenate([x_ref[2 * pp, c], x_ref[2 * pp + 1, c]], axis=1) + pet_ref[c]
                xs_ref[c, pp] = _dot_nt(perm, xt.astype(BF16))
        return carry

    lax.fori_loop(0, n_pairs // unroll, body, 0)
    rows_per_l = 2 * LANES // L_CMP
    for c in range(2):
        acc = jnp.zeros((nb, w1_ref.shape[3]), F32)
        for l in range(L_CMP):
            xl = xs_ref[c, :, l * rows_per_l:(l + 1) * rows_per_l, :].reshape(nb, KV_HALF)
            acc = acc + _dot(xl.astype(BF16), w1_ref[c, l])
        hid = acc * jax.nn.sigmoid(acc)
        o_ref[:, c * KV_HALF:(c + 1) * KV_HALF] = _dot(hid.astype(BF16), w2_ref[c]).astype(o_ref.dtype)


def _compress_pages(cache_t, pe, w1, w2, pages_tile):
    n_phys, _, _, page = cache_t.shape
    assert page == LANES and n_phys % 2 == 0
    pages_tile = int(np.gcd(pages_tile, n_phys))
    per_page = page // L_CMP
    pet = jnp.tile(pe[:, :, 0, :].transpose(0, 2, 1), (1, 1, 2 * per_page))
    src = np.arange(2 * page)
    dst = (src % L_CMP) * (2 * per_page) + src // L_CMP
    perm = np.zeros((2 * page, 2 * page), np.float32)
    perm[dst, src] = 1.0
    perm = jnp.asarray(perm, BF16)
    return pl.pallas_call(
        _compress_pages_kernel,
        grid=(n_phys // pages_tile,),
        in_specs=[pl.BlockSpec((pages_tile, 2, KV_HALF, page), lambda i: (i, 0, 0, 0))]
        + _weight_specs([pet.shape, perm.shape, w1.shape, w2.shape]),
        out_specs=pl.BlockSpec((pages_tile * per_page, KV_DIM), lambda i: (i, 0)),
        out_shape=jax.ShapeDtypeStruct((n_phys * per_page, KV_DIM), BF16),
        scratch_shapes=[pltpu.VMEM((2, pages_tile // 2, 2 * page, KV_HALF), F32)],
        compiler_params=_params(1), name="compress_pages",
    )(cache_t, pet, perm, w1, w2)


def _stack_heads(q, g):
    return jnp.concatenate([q[:, (GROUP * g + r) * LANES:(GROUP * g + r + 1) * LANES] for r in range(GROUP)], axis=0)


def _masked_softmax(s, mask):
    s = jnp.where(mask, s, NEG)
    e = jnp.exp2(s - jnp.max(s, axis=-1, keepdims=True))
    return e / jnp.sum(e, axis=-1, keepdims=True) * mask.astype(F32)


def _assemble_heads(o_heads):
    chunks = []
    for c in range(N_HEADS // 2):
        g = (2 * c) // GROUP
        a, b = o_heads[2 * c], o_heads[2 * c + 1]
        if g == 1:
            a = pltpu.roll(a, HEAD_DIM, 1)
        else:
            b = pltpu.roll(b, HEAD_DIM, 1)
        lane = lax.broadcasted_iota(jnp.int32, a.shape, 1)
        chunks.append(jnp.where(lane < HEAD_DIM, a, b))
    return jnp.concatenate(chunks, axis=1)


def _select_blocks(score, n_pick):
    n_blk = score.shape[1]
    lane = lax.broadcasted_iota(jnp.int32, score.shape, 1)
    work = score
    picks = []
    for _ in range(n_pick):
        m = jnp.max(work, axis=-1, keepdims=True)
        idx = jnp.min(jnp.where(work == m, lane, n_blk), axis=-1, keepdims=True)
        picks.append(idx)
        work = jnp.where(lane == idx, -jnp.inf, work)
    return picks


def _select_blocks_t(score_t, n_pick):
    n_blk = score_t.shape[0]
    blk = lax.broadcasted_iota(jnp.int32, score_t.shape, 0)
    work = score_t
    sel = jnp.zeros(score_t.shape, F32)
    for _ in range(n_pick):
        m = jnp.max(work, axis=0, keepdims=True)
        idx = jnp.min(jnp.where(work == m, blk, n_blk), axis=0, keepdims=True)
        hit = blk == idx
        sel = jnp.where(hit, 1.0, sel)
        work = jnp.where(hit, -jnp.inf, work)
    return sel


def _cmp_attention(qg, kc, vc, pos_rows, n_sel):
    s = _dot_nt(qg, kc)
    j = lax.broadcasted_iota(jnp.int32, s.shape, 1)
    n = 2 * (j % n_sel) + j // n_sel
    mask = (n * L_CMP + (L_CMP - 1)) <= pos_rows
    p = _masked_softmax(s, mask)
    return _dot(p.astype(BF16), vc), p


def _cmp_topk_kernel(q_ref, c_ref, gates_ref, oc_ref, msel_ref, *, n_sel, n_top):
    i = pl.program_id(1)
    tq = q_ref.shape[0]
    q = q_ref[...]
    comp = c_ref[0]
    kc, vc = comp[:, :LANES], comp[:, LANES:]
    gates = gates_ref[...]
    pos = i * tq + lax.broadcasted_iota(jnp.int32, (tq, 1), 0)
    pos_rows = jnp.concatenate([pos] * GROUP, axis=0)
    blk = lax.broadcasted_iota(jnp.int32, (n_sel, tq), 0)
    cur = (i * tq + lax.broadcasted_iota(jnp.int32, (1, tq), 1)) // L_SEL
    forced = (blk == 0) | (blk == cur) | (blk == cur - 1)
    o_heads = []
    for g in range(N_KV_HEADS):
        o, p = _cmp_attention(_stack_heads(q, g), kc, vc, pos_rows, n_sel)
        imp = p[0:tq]
        for r in range(1, GROUP):
            imp = imp + p[r * tq:(r + 1) * tq]
        imp = (imp[:, :n_sel] + imp[:, n_sel:]).T
        score = jnp.where(forced, FORCED, jnp.where(blk < cur, imp, NEG))
        sel = jnp.where(score > NEG / 2, _select_blocks_t(score, n_top), 0.0)
        msel_ref[:, g * n_sel:(g + 1) * n_sel] = ((1.0 - sel) * NEG).T.astype(BF16)
        for r in range(GROUP):
            h = GROUP * g + r
            o_heads.append(o[r * tq:(r + 1) * tq] * gates[:, h:h + 1])
    oc_ref[...] = _assemble_heads(o_heads)


def _cmp_topk_prompt(q, comp, gates, batch, seq, tq):
    n_sel = seq // L_SEL
    n_c = comp.shape[1]
    tiles = seq // tq
    row = lambda b, i: (b * tiles + i, 0)
    return pl.pallas_call(
        functools.partial(_cmp_topk_kernel, n_sel=n_sel, n_top=min(TOP_N, n_sel)),
        grid=(batch, tiles),
        in_specs=[pl.BlockSpec((tq, QPAD_DIM), row),
                  pl.BlockSpec((1, n_c, KV_DIM), lambda b, i: (b, 0, 0)),
                  pl.BlockSpec((tq, LANES), row)],
        out_specs=(pl.BlockSpec((tq, NSA_DIM), row), pl.BlockSpec((tq, N_KV_HEADS * n_sel), row)),
        out_shape=(jax.ShapeDtypeStruct((batch * seq, NSA_DIM), F32),
                   jax.ShapeDtypeStruct((batch * seq, N_KV_HEADS * n_sel), BF16)),
        compiler_params=_params(2), name="cmp_topk_prompt",
    )(q, comp, gates)


def _flash_update(carry, q, k, v, bias):
    m, l, acc = carry
    s = _dot_nt(q, k)
    if bias is not None:
        s = s + bias
    m_new = jnp.maximum(m, jnp.max(s, axis=-1, keepdims=True))
    alpha = jnp.exp2(m - m_new)
    p = jnp.exp2(s - m_new)
    l = alpha * l + jnp.sum(p, axis=-1, keepdims=True)
    acc = alpha * acc + _dot(p.astype(BF16), v)
    return m_new, l, acc


def _sel_win_kernel(q_ref, ks_ref, vs_ref, kw_ref, vw_ref, msel_ref, gates_ref, oc_ref, o_ref, *, n_sel):
    i = pl.program_id(1)
    tq = q_ref.shape[0]
    hpc = HEADS_PER_CHAIN
    n_chain = N_HEADS // hpc
    rows = hpc * tq
    q = q_ref[...]
    gates = gates_ref[...]
    pos = i * tq + lax.broadcasted_iota(jnp.int32, (tq, 1), 0)
    key = lax.broadcasted_iota(jnp.int32, (1, tq), 1)
    d0 = pl.multiple_of(i * tq, tq)
    p0 = pl.multiple_of(jnp.maximum(i - 1, 0) * tq, tq)
    bias_d = jnp.concatenate([jnp.where(i * tq + key <= pos, 0.0, NEG)] * hpc, axis=0)
    in_win = (pos - ((i - 1) * tq + key) < WINDOW) & (i > 0)
    bias_p = jnp.concatenate([jnp.where(in_win, 0.0, NEG)] * hpc, axis=0)
    qcs, qas = [], []
    for c in range(n_chain):
        g = (c * hpc) // GROUP
        qc = jnp.concatenate([q[:, h * LANES:(h + 1) * LANES] for h in range(c * hpc, (c + 1) * hpc)], axis=0)
        msel = jnp.concatenate([msel_ref[:, g * n_sel:(g + 1) * n_sel]] * hpc, axis=0)
        qcs.append(qc)
        qas.append(jnp.concatenate([qc, msel], axis=1))
    init = (jnp.full((rows, 1), NEG, F32), jnp.zeros((rows, 1), F32), jnp.zeros((rows, LANES), F32))

    kd, vd = ks_ref[pl.ds(d0, tq), :], vs_ref[pl.ds(d0, tq), :]
    sel = tuple(_flash_update(init, qas[c], kd, vd, bias_d) for c in range(n_chain))

    def body(kt, carries):
        k0 = pl.multiple_of(kt * tq, tq)
        k, v = ks_ref[pl.ds(k0, tq), :], vs_ref[pl.ds(k0, tq), :]
        return tuple(_flash_update(carries[c], qas[c], k, v, None) for c in range(n_chain))

    sel = lax.fori_loop(0, i, body, sel)
    kd, vd = kw_ref[pl.ds(d0, tq), :], vw_ref[pl.ds(d0, tq), :]
    kp, vp = kw_ref[pl.ds(p0, tq), :], vw_ref[pl.ds(p0, tq), :]
    o_heads = []
    for c in range(n_chain):
        win = _flash_update(_flash_update(init, qcs[c], kd, vd, bias_d), qcs[c], kp, vp, bias_p)
        o_s = sel[c][2] / sel[c][1]
        o_w = win[2] / win[1]
        for r in range(hpc):
            h = c * hpc + r
            rs = slice(r * tq, (r + 1) * tq)
            o_heads.append(o_s[rs] * gates[:, N_HEADS + h:N_HEADS + h + 1]
                           + o_w[rs] * gates[:, 2 * N_HEADS + h:2 * N_HEADS + h + 1])
    o_ref[...] = (_assemble_heads(o_heads) + oc_ref[...]).astype(o_ref.dtype)


def _sel_win_prompt(q, ksb, vsb, kwb, vwb, msel, gates, oc, batch, seq, tq):
    assert tq == WINDOW
    n_sel = seq // L_SEL
    tiles = seq // tq
    row = lambda b, i: (b * tiles + i, 0)
    whole = lambda b, i: (b, 0)
    return pl.pallas_call(
        functools.partial(_sel_win_kernel, n_sel=n_sel),
        grid=(batch, tiles),
        in_specs=[pl.BlockSpec((tq, QPAD_DIM), row),
                  pl.BlockSpec((seq, KV_HALF + n_sel), whole),
                  pl.BlockSpec((seq, KV_HALF), whole),
                  pl.BlockSpec((seq, KV_HALF), whole),
                  pl.BlockSpec((seq, KV_HALF), whole),
                  pl.BlockSpec((tq, N_KV_HEADS * n_sel), row),
                  pl.BlockSpec((tq, LANES), row),
                  pl.BlockSpec((tq, NSA_DIM), row)],
        out_specs=pl.BlockSpec((tq, NSA_DIM), row),
        out_shape=jax.ShapeDtypeStruct((batch * seq, NSA_DIM), BF16),
        compiler_params=_params(2), name="sel_win_prompt",
    )(q, ksb, vsb, kwb, vwb, msel, gates, oc)


def _sample_cmp_kernel(q_ref, c_ref, gates_ref, oc_ref, idx_ref, *, n_blk, n_pick, past_len):
    q = q_ref[0]
    comp = c_ref[0]
    gates = gates_ref[0]
    pos_rows = jnp.full((N_HEADS, 1), past_len, jnp.int32)
    o, p = _cmp_attention(q, comp[:, :LANES], comp[:, LANES:], pos_rows, n_blk)
    oc_ref[0] = o * gates[:, 0:1]
    cur = past_len // L_SEL
    blk = lax.broadcasted_iota(jnp.int32, (N_KV_HEADS, n_blk), 1)
    imp = jnp.concatenate(
        [jnp.sum(p[GROUP * g:GROUP * (g + 1)], axis=0, keepdims=True) for g in range(N_KV_HEADS)], axis=0)
    imp = imp[:, :n_blk] + imp[:, n_blk:]
    forced = (blk == 0) | (blk == cur) | (blk == cur - 1)
    score = jnp.where(forced, FORCED, jnp.where(blk < cur, imp, NEG))
    lane = lax.broadcasted_iota(jnp.int32, (N_KV_HEADS, LANES), 1)
    out = jnp.zeros((N_KV_HEADS, LANES), jnp.int32)
    for k, idx in enumerate(_select_blocks(score, n_pick)):
        out = jnp.where(lane == k, idx, out)
    idx_ref[0] = jnp.concatenate([out, jnp.zeros((N_HEADS - N_KV_HEADS, LANES), jnp.int32)], axis=0)


def _sample_cmp(q3, comp, gates3, n_pick, past_len):
    n_seq = q3.shape[0]
    n_c = comp.shape[1]
    blk3 = lambda b: (b, 0, 0)
    return pl.pallas_call(
        functools.partial(_sample_cmp_kernel, n_blk=n_c // 2, n_pick=n_pick, past_len=past_len),
        grid=(n_seq,),
        in_specs=[pl.BlockSpec((1, N_HEADS, LANES), blk3),
                  pl.BlockSpec((1, n_c, KV_DIM), blk3),
                  pl.BlockSpec((1, N_HEADS, LANES), blk3)],
        out_specs=(pl.BlockSpec((1, N_HEADS, LANES), blk3), pl.BlockSpec((1, N_HEADS, LANES), blk3)),
        out_shape=(jax.ShapeDtypeStruct((n_seq, N_HEADS, LANES), F32),
                   jax.ShapeDtypeStruct((n_seq, N_HEADS, LANES), jnp.int32)),
        compiler_params=_params(1), name="sample_cmp",
    )(q3, comp, gates3)


def _softmax_with_new(s, valid, s_new):
    s = jnp.where(valid, s, NEG)
    m = jnp.maximum(jnp.max(s, axis=-1, keepdims=True), s_new)
    e = jnp.exp2(s - m) * valid.astype(F32)
    e_new = jnp.exp2(s_new - m)
    den = jnp.sum(e, axis=-1, keepdims=True) + e_new
    return e / den, e_new / den


def _sample_sel_win_kernel(pg_ref, hf_ref, q_ref, win_ref, kvs_ref, kvw_ref, gates_ref, oc_ref, cache_ref,
                           o_ref, buf_ref, sem_ref, *, n_pick):
    b = pl.program_id(0)
    n_b = pl.num_programs(0)
    n_dma = N_KV_HEADS * n_pick
    page = buf_ref.shape[-1]

    def page_copy(bb, slot, j):
        return pltpu.make_async_copy(cache_ref.at[pg_ref[bb * n_dma + j]], buf_ref.at[slot, j], sem_ref.at[slot])

    @pl.when(b == 0)
    def _():
        for j in range(n_dma):
            page_copy(0, 0, j).start()

    @pl.when(b + 1 < n_b)
    def _():
        for j in range(n_dma):
            page_copy(b + 1, (b + 1) % 2, j).start()

    q = q_ref[0]
    qf = q.astype(F32)
    gates = gates_ref[0]
    row = lax.broadcasted_iota(jnp.int32, (N_HEADS, LANES), 0)

    def new_key(kv_ref):
        kv = kv_ref[0].astype(BF16).astype(F32)
        return jnp.sum(qf * kv[:, :KV_HALF], axis=-1, keepdims=True), kv[:, KV_HALF:]

    win = win_ref[0]
    kt, vt = win[0].astype(BF16), win[1].astype(BF16)
    s_new, v_new = new_key(kvw_ref)
    lane_w = lax.broadcasted_iota(jnp.int32, (N_HEADS, kt.shape[1]), 1)
    p, p_new = _softmax_with_new(_dot(q, kt), lane_w >= 1, s_new)
    o_w = _dot_nt(p.astype(BF16), vt) + p_new.astype(BF16).astype(F32) * v_new

    slot = b % 2
    for j in range(n_dma):
        page_copy(b, slot, j).wait()
    s_new, v_new = new_key(kvs_ref)
    lane_p = lax.broadcasted_iota(jnp.int32, (N_HEADS, page), 1)
    o_s = jnp.zeros((N_HEADS, LANES), F32)
    for g in range(N_KV_HEADS):
        kts, vts, valids = [], [], []
        for k in range(n_pick):
            j = g * n_pick + k
            pg = buf_ref[slot, j]
            kts.append(pg[0].astype(BF16))
            vts.append(pg[1].astype(BF16))
            half = hf_ref[b * n_dma + j]
            valids.append((lane_p // L_SEL) == half)
        kt_all = jnp.concatenate(kts, axis=1)
        vt_all = jnp.concatenate(vts, axis=1)
        p, p_new = _softmax_with_new(_dot(q, kt_all), jnp.concatenate(valids, axis=1), s_new)
        o_g = _dot_nt(p.astype(BF16), vt_all) + p_new.astype(BF16).astype(F32) * v_new
        o_s = jnp.where(row // GROUP == g, o_g, o_s)
    o_ref[0] = oc_ref[0] + o_s * gates[:, 1:2] + o_w * gates[:, 2:3]


def _sample_sel_win(pages, halves, q3, win_t, kvs, kvw, gates3, oc, cache_t, n_pick):
    n_seq = q3.shape[0]
    n_dma = N_KV_HEADS * n_pick
    blk3 = lambda b, pg, hf: (b, 0, 0)
    grid_spec = pltpu.PrefetchScalarGridSpec(
        num_scalar_prefetch=2, grid=(n_seq,),
        in_specs=[pl.BlockSpec((1, N_HEADS, LANES), blk3),
                  pl.BlockSpec((1,) + win_t.shape[1:], lambda b, pg, hf: (b, 0, 0, 0)),
                  pl.BlockSpec((1, 1, KV_DIM), blk3),
                  pl.BlockSpec((1, 1, KV_DIM), blk3),
                  pl.BlockSpec((1, N_HEADS, LANES), blk3),
                  pl.BlockSpec((1, N_HEADS, LANES), blk3),
                  pl.BlockSpec(memory_space=pl.ANY)],
        out_specs=pl.BlockSpec((1, N_HEADS, LANES), blk3),
        scratch_shapes=[pltpu.VMEM((2, n_dma) + cache_t.shape[1:], F32), pltpu.SemaphoreType.DMA((2,))],
    )
    return pl.pallas_call(
        functools.partial(_sample_sel_win_kernel, n_pick=n_pick),
        grid_spec=grid_spec,
        out_shape=jax.ShapeDtypeStruct((n_seq, N_HEADS, LANES), F32),
        compiler_params=_params(1), name="sample_sel_win",
    )(pages, halves, q3, win_t, kvs, kvw, gates3, oc, cache_t)


def _tail_kernel(x_ref, attn_ref, d_ref, xs_ref, attns_ref, ds_ref, gmix_ref, wgm_ref, wpool_ref, pscale_ref,
                 wbn_ref, wbp_ref, wout_ref, gffn_ref, wr_ref, br_ref, h_ref, xn2_ref, route_ref, cout_ref, run_ref,
                 *, n_experts):
    i = pl.program_id(0)
    is_sample = i == pl.num_programs(0) - 1
    x = jnp.where(is_sample, xs_ref[...], x_ref[...])
    attn = jnp.where(is_sample, attns_ref[...], attn_ref[...])
    d = jnp.where(is_sample, ds_ref[...], d_ref[...])
    tm = x.shape[0]
    xn = _rms(x, gmix_ref[...]).astype(BF16)
    gm = jax.nn.sigmoid(_dot(xn, wgm_ref[...]))
    pool = jnp.concatenate(
        [_dot(d[:, gi * POOL_GROUP_DIM:(gi + 1) * POOL_GROUP_DIM], wpool_ref[gi]) for gi in range(len(POOL_WINDOWS))],
        axis=1) * pscale_ref[...]
    merged = (gm[:, :D_MODEL] * _dot(attn, wbn_ref[...])
              + gm[:, D_MODEL:] * _dot(pool.astype(BF16), wbp_ref[...]))
    h = x + _dot(merged.astype(BF16), wout_ref[...])
    h_ref[...] = h
    xn2 = _rms(h, gffn_ref[...]).astype(BF16)
    xn2_ref[...] = xn2
    logits = _dot(xn2, wr_ref[...]) + br_ref[...]
    lane = lax.broadcasted_iota(jnp.int32, (tm, LANES), 1)
    work = jnp.where(lane < n_experts, logits, -jnp.inf)
    vals, onehots = [], []
    for _ in range(TOP_K):
        m = jnp.max(work, axis=-1, keepdims=True)
        idx = jnp.min(jnp.where(work == m, lane, LANES), axis=-1, keepdims=True)
        vals.append(m)
        onehots.append(lane == idx)
        work = jnp.where(lane == idx, -jnp.inf, work)
    es = [jnp.exp(v - vals[0]) for v in vals]
    tot = es[0] + es[1] + es[2] + es[3]

    @pl.when(i == 0)
    def _():
        run_ref[...] = jnp.zeros(run_ref.shape, F32)

    chosen = jnp.zeros((tm, LANES), F32)
    for k in range(TOP_K):
        chosen = jnp.where(onehots[k], 1.0, chosen)
    earlier = lax.broadcasted_iota(jnp.int32, (tm, tm), 0) > lax.broadcasted_iota(jnp.int32, (tm, tm), 1)
    before = run_ref[...] + _dot(jnp.where(earlier, 1.0, 0.0).astype(BF16), chosen.astype(BF16))
    route = jnp.zeros((tm, LANES), F32)
    lane_f = lane.astype(F32)
    for k in range(TOP_K):
        e_k = jnp.sum(jnp.where(onehots[k], lane_f, 0.0), axis=-1, keepdims=True)
        r_k = jnp.sum(jnp.where(onehots[k], before, 0.0), axis=-1, keepdims=True)
        route = jnp.where(lane == k, e_k, route)
        route = jnp.where(lane == TOP_K + k, es[k] / tot, route)
        route = jnp.where(lane == 2 * TOP_K + k, r_k, route)
    route_ref[...] = route
    run_ref[...] = run_ref[...] + jnp.sum(chosen, axis=0, keepdims=True)
    cout_ref[...] = run_ref[...]


def _tail(x, attn, d, xs, attn_s, d_s, weights, n_experts, tm):
    n_p = x.shape[0] // tm
    t = (n_p + 1) * tm
    row = lambda i: (i, 0)
    prompt = lambda i: (jnp.minimum(i, n_p - 1), 0)
    fixed = lambda i: (0, 0)
    return pl.pallas_call(
        functools.partial(_tail_kernel, n_experts=n_experts),
        grid=(n_p + 1,),
        in_specs=[pl.BlockSpec((tm, D_MODEL), prompt), pl.BlockSpec((tm, NSA_DIM), prompt),
                  pl.BlockSpec((tm, POOL_DIM), prompt), pl.BlockSpec((tm, D_MODEL), fixed),
                  pl.BlockSpec((tm, NSA_DIM), fixed), pl.BlockSpec((tm, POOL_DIM), fixed)]
        + _weight_specs([w.shape for w in weights]),
        out_specs=(pl.BlockSpec((tm, D_MODEL), row), pl.BlockSpec((tm, D_MODEL), row),
                   pl.BlockSpec((tm, LANES), row), pl.BlockSpec((1, LANES), fixed)),
        out_shape=(jax.ShapeDtypeStruct((t, D_MODEL), F32), jax.ShapeDtypeStruct((t, D_MODEL), BF16),
                   jax.ShapeDtypeStruct((t, LANES), F32), jax.ShapeDtypeStruct((1, LANES), F32)),
        scratch_shapes=[pltpu.VMEM((1, LANES), F32)],
        compiler_params=_params(1), name="tail",
    )(x, attn, d, xs, attn_s, d_s, *weights)


def _moe_kernel(te_ref, tv_ref, tf_ref, x_ref, wg_ref, bg_ref, wu_ref, bu_ref, wd_ref, bd_ref, y_ref,
                wgb_ref, wub_ref, wdb_ref):
    t = pl.program_id(0)

    @pl.when(tf_ref[t] > 0)
    def _():
        wgb_ref[...] = wg_ref[0].astype(BF16)
        wub_ref[...] = wu_ref[0].astype(BF16)
        wdb_ref[...] = wd_ref[0].astype(BF16)

    @pl.when(tv_ref[t] > 0)
    def _():
        x = x_ref[...]
        gate = jnp.minimum(_dot(x, wgb_ref[...]) + bg_ref[0], SWIGLU_LIMIT)
        up = jnp.clip(_dot(x, wub_ref[...]) + bu_ref[0], -SWIGLU_LIMIT, SWIGLU_LIMIT)
        hid = (up + 1.0) * gate * jax.nn.sigmoid(SWIGLU_ALPHA * gate)
        y_ref[...] = (_dot(hid.astype(BF16), wdb_ref[...]) + bd_ref[0]).astype(y_ref.dtype)

    @pl.when(tv_ref[t] == 0)
    def _():
        y_ref[...] = jnp.zeros(y_ref.shape, y_ref.dtype)


def _moe(tile_e, tile_v, tile_f, xs, wg, bg, wu, bu, wd, bd, tme):
    p_pad = xs.shape[0]
    d_e = wg.shape[2]
    wspec = lambda shape: pl.BlockSpec((1,) + shape, lambda t, te, tv, tf: (te[t], 0, 0))
    grid_spec = pltpu.PrefetchScalarGridSpec(
        num_scalar_prefetch=3, grid=(p_pad // tme,),
        in_specs=[pl.BlockSpec((tme, D_MODEL), lambda t, te, tv, tf: (t, 0)),
                  wspec((D_MODEL, d_e)), wspec((1, d_e)), wspec((D_MODEL, d_e)), wspec((1, d_e)),
                  wspec((d_e, D_MODEL)), wspec((1, D_MODEL))],
        out_specs=pl.BlockSpec((tme, D_MODEL), lambda t, te, tv, tf: (t, 0)),
        scratch_shapes=[pltpu.VMEM((D_MODEL, d_e), BF16), pltpu.VMEM((D_MODEL, d_e), BF16),
                        pltpu.VMEM((d_e, D_MODEL), BF16)],
    )
    return pl.pallas_call(
        _moe_kernel, grid_spec=grid_spec,
        out_shape=jax.ShapeDtypeStruct((p_pad, D_MODEL), BF16),
        compiler_params=_params(1), name="moe_experts",
    )(tile_e, tile_v, tile_f, xs, wg, bg, wu, bu, wd, bd)


def _final_kernel(h_ref, y_ref, route_ref, g_ref, o_ref):
    route = route_ref[...]
    acc = h_ref[...]
    for k in range(TOP_K):
        acc = acc + y_ref[k].astype(F32) * route[:, TOP_K + k:TOP_K + k + 1]
    o_ref[...] = _rms(acc, g_ref[...])


def _final(h, yg, route, g_final, tm, first, n_tiles):
    src = lambda i: (i + first, 0)
    return pl.pallas_call(
        _final_kernel, grid=(n_tiles,),
        in_specs=[pl.BlockSpec((tm, D_MODEL), src), pl.BlockSpec((TOP_K, tm, D_MODEL), lambda i: (0, i + first, 0)),
                  pl.BlockSpec((tm, LANES), src), pl.BlockSpec((1, D_MODEL), lambda i: (0, 0))],
        out_specs=pl.BlockSpec((tm, D_MODEL), lambda i: (i, 0)),
        out_shape=jax.ShapeDtypeStruct((n_tiles * tm, D_MODEL), F32),
        compiler_params=_params(1), name="final_norm",
    )(h, yg, route, g_final)


def _rope_tables(pos):
    half = HEAD_DIM // 2
    inv = ROPE_THETA ** (-jnp.arange(half, dtype=F32) / half)
    ang = pos.astype(F32)[:, None] * inv[None, :]
    cos, sin = jnp.cos(ang), jnp.sin(ang)
    cos = jnp.concatenate([cos, cos] * (LANES // HEAD_DIM), axis=1)
    sin = jnp.concatenate([-sin, sin] * (LANES // HEAD_DIM), axis=1)
    return cos, sin


def _split_w_in(w_in):
    o_q = NSA_DIM
    o_g = o_q + 3 * KV_DIM
    o_u = o_g + 3 * N_HEADS
    o_m = o_u + POOL_DIM
    wq = w_in[:, :o_q].reshape(D_MODEL, N_HEADS, HEAD_DIM)
    zeros = jnp.zeros_like(wq)
    lo = jnp.concatenate([wq, zeros], axis=2)
    hi = jnp.concatenate([zeros, wq], axis=2)
    in_hi = (jnp.arange(N_HEADS) // GROUP == 1)[None, :, None]
    wq_pad = jnp.where(in_hi, hi, lo).reshape(D_MODEL, QPAD_DIM)
    wkv = w_in[:, o_q:o_g]
    wg = w_in[:, o_g:o_u].reshape(D_MODEL, N_HEADS, 3).transpose(0, 2, 1).reshape(D_MODEL, 3 * N_HEADS)
    wg = jnp.pad(wg, ((0, 0), (0, LANES - 3 * N_HEADS)))
    wu = w_in[:, o_u:o_m]
    wgm = w_in[:, o_m:]
    return [w.astype(BF16) for w in (wq_pad, wkv, wg, wu, wgm)]


def _compress_weights(cmp_pe, cmp_w1, cmp_w2):
    eye = jnp.eye(N_KV_HEADS, dtype=F32)
    w1 = jnp.einsum('cldh,gf->clgdfh', cmp_w1, eye).reshape(2, L_CMP, KV_HALF, N_KV_HEADS * CMP_HIDDEN)
    w2 = jnp.einsum('chd,gf->cghfd', cmp_w2, eye).reshape(2, N_KV_HEADS * CMP_HIDDEN, KV_HALF)
    pe = jnp.broadcast_to(cmp_pe.transpose(1, 0, 2)[:, :, None, :], (2, L_CMP, N_KV_HEADS, HEAD_DIM))
    return pe.reshape(2, L_CMP, 1, KV_HALF), w1.astype(BF16), w2.astype(BF16)


def _even_odd(comp, n_seq):
    n_c = comp.shape[0] // n_seq
    return comp.reshape(n_seq, n_c // 2, 2, KV_DIM).transpose(0, 2, 1, 3).reshape(n_seq, n_c, KV_DIM)


def _fix_sample_heads(o3):
    n_seq = o3.shape[0]
    o4 = o3.reshape(n_seq, N_KV_HEADS, GROUP, N_KV_HEADS, HEAD_DIM)
    return jnp.concatenate([o4[:, g, :, g, :] for g in range(N_KV_HEADS)], axis=1).reshape(n_seq, NSA_DIM)


def _gates_by_head(gates):
    g = gates[:, :3 * N_HEADS].reshape(-1, 3, N_HEADS).transpose(0, 2, 1)
    return jnp.pad(g, ((0, 0), (0, 0), (0, LANES - 3)))


def _key_minor(x):
    n, p = x.shape[:2]
    return x.transpose(0, 2, 3, 4, 1).reshape(n, 2, KV_HALF, p)


def _from_key_minor(xt):
    n, _, _, p = xt.shape
    return xt.reshape(n, 2, N_KV_HEADS, HEAD_DIM, p).transpose(0, 4, 1, 2, 3)


def _expert_order(route, counts, n_experts, tme):
    t = route.shape[0]
    n_pairs = t * TOP_K
    e = route[:, :TOP_K].astype(jnp.int32)
    rank = route[:, 2 * TOP_K:3 * TOP_K].astype(jnp.int32)
    counts = counts[0, :n_experts].astype(jnp.int32)
    padded = ((counts + tme - 1) // tme) * tme
    ends = jnp.cumsum(padded)
    off = ends - padded
    pad_before = off - (jnp.cumsum(counts) - counts)
    experts = jnp.arange(n_experts, dtype=jnp.int32)
    pair_slot = rank + jnp.sum(jnp.where(e[:, :, None] == experts, off, 0), axis=-1)
    n_tiles = -(-n_pairs // tme) + n_experts
    p_pad = n_tiles * tme
    tok = jnp.broadcast_to(jnp.arange(t, dtype=jnp.int32)[:, None], (t, TOP_K))
    _, tok_sorted = lax.sort((pair_slot.reshape(n_pairs), tok.reshape(n_pairs)), num_keys=1)
    slot = jnp.arange(p_pad, dtype=jnp.int32)
    slot_e = jnp.sum((slot[:, None] >= ends[None, :]).astype(jnp.int32), axis=-1)
    slot_e = jnp.minimum(slot_e, n_experts - 1)
    slot_pad = jnp.sum(jnp.where(slot_e[:, None] == experts, pad_before, 0), axis=-1)
    src_tok = tok_sorted[jnp.clip(slot - slot_pad, 0, n_pairs - 1)]
    tile_start = jnp.arange(n_tiles, dtype=jnp.int32) * tme
    tile_e = jnp.minimum(jnp.sum((tile_start[:, None] >= ends[None, :]).astype(jnp.int32), axis=-1), n_experts - 1)
    tile_v = (tile_start < ends[-1]).astype(jnp.int32)
    tile_f = jnp.concatenate([jnp.ones((1,), jnp.int32), (tile_e[1:] != tile_e[:-1]).astype(jnp.int32)])
    return pair_slot, src_tok, tile_e, tile_v, tile_f


def kernel(x_prompt, x_sample, cache_kv_cmp, cache_kv_sel, state_kv_win, state_pool, page_table, g_mix, w_in, cmp_pe, cmp_w1, cmp_w2, w_pool, pool_scale, w_br_nsa, w_br_pool, w_out, g_ffn, w_router, b_router, w_gate, b_gate, w_up, b_up, w_down, b_down, g_final):
    batch, seq, _ = x_prompt.shape
    n_seq, dec_seq, _ = x_sample.shape
    depth = g_mix.shape[0]
    assert depth == 1 and dec_seq == 1
    n_phys, page_size = cache_kv_cmp.shape[1:3]
    n_pages = page_table.shape[1]
    past_len = n_pages * page_size
    n_experts = w_router.shape[2]
    t_p = batch * seq
    tm = min(512, seq)

    wq, wkv, wg, wu, wgm = _split_w_in(w_in[0])
    gmix = g_mix[0][None, :]
    pe, w1, w2 = _compress_weights(cmp_pe[0], cmp_w1[0], cmp_w2[0])
    cos_p, sin_p = _rope_tables(jnp.arange(seq, dtype=jnp.int32))
    cos_s, sin_s = _rope_tables(jnp.full((n_seq,), past_len, jnp.int32))
    wr = jnp.pad(w_router[0], ((0, 0), (0, LANES - n_experts))).astype(BF16)
    br = jnp.pad(b_router[0], (0, LANES - n_experts))[None, :]
    tail_w = [gmix, wgm, w_pool[0].astype(BF16), pool_scale[0][None, :], w_br_nsa[0].astype(BF16),
              w_br_pool[0].astype(BF16), w_out[0].astype(BF16), g_ffn[0][None, :], wr, br]

    xp = x_prompt.reshape(t_p, D_MODEL)
    (q_p, kvc_p, kvct_p, kvst_p, kvwt_p, ksb_p, vsb_p, kwb_p, vwb_p, gates_p, u_p, d_p) = _inproj_prompt(
        xp, cos_p, sin_p, gmix, wq, wkv, wg, wu, batch, seq, tm)
    comp_p = _even_odd(_compress(kvc_p, pe, w1, w2, 256), batch)
    tq = min(256, seq)
    oc_p, msel_p = _cmp_topk_prompt(q_p, comp_p, gates_p, batch, seq, tq)
    attn_p = _sel_win_prompt(q_p, ksb_p, vsb_p, kwb_p, vwb_p, msel_p, gates_p, oc_p, batch, seq, WINDOW)

    xs = x_sample.reshape(n_seq, D_MODEL)
    sp_t = state_pool[0].transpose(1, 0, 2)
    (q_s, kvc_s, _, kvst_s, kvwt_s, _, _, _, _, gates_s, u_s, d_s) = _inproj_sample(
        xs, sp_t, cos_s, sin_s, gmix, wq, wkv, wg, wu, past_len)
    kvs_s = kvst_s.reshape(KV_DIM, n_seq).T
    kvw_s = kvwt_s.reshape(KV_DIM, n_seq).T
    blocks_per_page = page_size // L_CMP
    cache_cmp_t = _key_minor(cache_kv_cmp[0])
    comp_all = _compress_pages(cache_cmp_t, pe, w1, w2, 64)
    comp_s = comp_all.reshape(n_phys, blocks_per_page, KV_DIM)[page_table].reshape(n_seq * n_pages * blocks_per_page, KV_DIM)
    comp_s = _even_odd(comp_s, n_seq)
    q3 = q_s.reshape(n_seq, N_HEADS, LANES)
    gates3 = _gates_by_head(gates_s)
    n_blk = past_len // L_SEL
    n_pick = min(TOP_N, n_blk + 1) - 1
    oc_s, idx_s = _sample_cmp(q3, comp_s, gates3, n_pick, past_len)
    top_idx = idx_s[:, :N_KV_HEADS, :n_pick]
    sel_per_page = page_size // L_SEL
    pages_b = jnp.broadcast_to(page_table[:, None, :], (n_seq, N_KV_HEADS, n_pages))
    phys = jnp.take_along_axis(pages_b, top_idx // sel_per_page, axis=2).reshape(-1)
    halves = (top_idx % sel_per_page).reshape(-1)
    w_buf = state_kv_win.shape[2]
    assert w_buf == WINDOW
    win_t = _key_minor(state_kv_win[0])
    o3 = _sample_sel_win(phys, halves, q3, win_t, kvs_s[:, None, :], kvw_s[:, None, :], gates3, oc_s,
                         _key_minor(cache_kv_sel[0]), n_pick)
    attn_s = _fix_sample_heads(o3).astype(BF16)

    assert n_seq <= tm
    pad_rows = lambda a: jnp.pad(a, ((0, tm - n_seq), (0, 0)))
    h_all, xn2_all, route, counts = _tail(xp, attn_p, d_p, pad_rows(xs), pad_rows(attn_s), pad_rows(d_s),
                                          tail_w, n_experts, tm)
    tme = 512
    pair_slot, src_tok, tile_e, tile_v, tile_f = _expert_order(route, counts, n_experts, tme)
    y_sorted = _moe(tile_e, tile_v, tile_f, xn2_all[src_tok], w_gate[0], b_gate[0][:, None, :],
                    w_up[0], b_up[0][:, None, :], w_down[0], b_down[0][:, None, :], tme)
    y_pairs = y_sorted[pair_slot.T]
    gfin = g_final[None, :]
    n_p = t_p // tm
    y_p = _final(h_all, y_pairs, route, gfin, tm, 0, n_p)
    y_s = _final(h_all, y_pairs, route, gfin, tm, n_p, 1)[:n_seq]

    w_p = min(WINDOW, seq)
    kv_row = (2, N_KV_HEADS, HEAD_DIM)
    new_col = kvwt_s.reshape(2, KV_HALF, n_seq).transpose(2, 0, 1)[..., None]
    win_new_t = jnp.concatenate([win_t[..., 1:], new_col], axis=-1)
    return (
        y_p.reshape(batch, seq, D_MODEL),
        y_s.reshape(n_seq, 1, D_MODEL),
        _from_key_minor(kvct_p)[None],
        kvc_s.reshape((1, n_seq, 1) + kv_row),
        _from_key_minor(kvst_p)[None],
        kvs_s.reshape((1, n_seq, 1) + kv_row),
        _from_key_minor(kvwt_p[..., seq - w_p:])[None],
        _from_key_minor(win_new_t)[None],
        u_p.reshape(batch, seq, POOL_DIM)[None, :, seq - POOL_BUF:],
        jnp.concatenate([state_pool[0], u_s[:, None, :]], axis=1)[None, :, 1:],
    )
```

```python
import functools
import math

import jax
import jax.numpy as jnp
import numpy as np
from jax import lax
from jax.experimental import pallas as pl
from jax.experimental.pallas import tpu as pltpu

D_MODEL = 1024
N_HEADS = 8
N_KV_HEADS = 2
HEAD_DIM = 64
GROUP = N_HEADS // N_KV_HEADS
NSA_DIM = N_HEADS * HEAD_DIM
KV_DIM = 2 * N_KV_HEADS * HEAD_DIM
KV_HALF = KV_DIM // 2
L_CMP = 32
L_SEL = 64
CMP_HIDDEN = 2 * HEAD_DIM
TOP_N = 16
WINDOW = 512
ROPE_THETA = 10000.0
POOL_WINDOWS = (2, 4, 8, 16)
POOL_DIM = D_MODEL // 2
POOL_GROUP_DIM = POOL_DIM // len(POOL_WINDOWS)
POOL_BUF = max(POOL_WINDOWS) - 1
POOL_HALO = POOL_BUF + 1
TOP_K = 4
SWIGLU_LIMIT = 7.0
SWIGLU_ALPHA = 1.702
RMS_EPS = 1e-6
NEG = -1e30
FORCED = 1e9

LANES = 128
QPAD_DIM = N_HEADS * LANES
VMEM_LIMIT = 56 * 1024 * 1024
Q_SCALE = HEAD_DIM ** -0.5 * math.log2(math.e)
PAIR_UNROLL = 8
HEADS_PER_CHAIN = GROUP

BF16 = jnp.bfloat16
F32 = jnp.float32


def _params(n_grid, vmem=VMEM_LIMIT):
    return pltpu.CompilerParams(dimension_semantics=("arbitrary",) * n_grid, vmem_limit_bytes=vmem)


def _rms(x, g):
    r = lax.rsqrt(jnp.mean(x * x, axis=-1, keepdims=True) + RMS_EPS)
    return x * r * g


def _dot(a, b):
    return jnp.dot(a, b, preferred_element_type=F32)


def _dot_nt(a, b):
    return lax.dot_general(a, b, (((1,), (1,)), ((), ())), preferred_element_type=F32)


def _rope_chunk(x, cos, sin_signed):
    lane = lax.broadcasted_iota(jnp.int32, x.shape, 1)
    first = (lane % HEAD_DIM) < (HEAD_DIM // 2)
    swapped = jnp.where(first, pltpu.roll(x, LANES - HEAD_DIM // 2, 1), pltpu.roll(x, HEAD_DIM // 2, 1))
    return x * cos + swapped * sin_signed


def _project(x, g, cos, sin, pos, wq_ref, wkv_ref, wg_ref, wu_ref, outs):
    (q_ref, kvc_ref, kvct_ref, kvst_ref, kvwt_ref, ksb_ref, vs0_ref, vs1_ref, kwb_ref, vw0_ref, vw1_ref,
     gates_ref, u_ref) = outs
    xn = _rms(x, g).astype(BF16)
    q = _dot(xn, wq_ref[...])
    for h in range(N_HEADS):
        sl = slice(h * LANES, (h + 1) * LANES)
        q_ref[:, sl] = (_rope_chunk(q[:, sl], cos, sin) * Q_SCALE).astype(BF16)
    kv = _dot(xn, wkv_ref[...])
    n_sel = ksb_ref.shape[1] - KV_HALF
    blk = lax.broadcasted_iota(jnp.int32, (x.shape[0], n_sel), 1)
    for j, t_ref in enumerate((kvct_ref, kvst_ref, kvwt_ref)):
        k = _rope_chunk(kv[:, j * KV_DIM:j * KV_DIM + KV_HALF], cos, sin)
        v = kv[:, j * KV_DIM + KV_HALF:(j + 1) * KV_DIM]
        t_ref[0, 0] = k.T
        t_ref[0, 1] = v.T
        low = lax.broadcasted_iota(jnp.int32, v.shape, 1) < HEAD_DIM
        v0 = jnp.where(low, v, 1.0).astype(BF16)
        v1 = jnp.where(low, 1.0, v).astype(BF16)
        if j == 0:
            kvc_ref[:, :KV_HALF] = k
            kvc_ref[:, KV_HALF:] = v
        elif j == 1:
            ksb_ref[:, :KV_HALF] = k.astype(BF16)
            ksb_ref[:, KV_HALF:] = jnp.where(blk == pos // L_SEL, 1.0, 0.0).astype(BF16)
            vs0_ref[...] = v0
            vs1_ref[...] = v1
        else:
            kwb_ref[...] = k.astype(BF16)
            vw0_ref[...] = v0
            vw1_ref[...] = v1
    gates_ref[...] = jax.nn.sigmoid(_dot(xn, wg_ref[...]))
    u = _dot(xn, wu_ref[...])
    u_ref[...] = u
    return u


def _inproj_prompt_kernel(x_ref, xh_ref, cos_ref, sin_ref, g_ref, wq_ref, wkv_ref, wg_ref, wu_ref, *outs,
                          tiles_per_seq):
    d_ref = outs[-1]
    i = pl.program_id(0)
    tm = x_ref.shape[0]
    g = g_ref[...]
    seq_tile = i % tiles_per_seq
    pos = seq_tile * tm + lax.broadcasted_iota(jnp.int32, (tm, 1), 0)
    u = _project(x_ref[...], g, cos_ref[...], sin_ref[...], pos, wq_ref, wkv_ref, wg_ref, wu_ref, outs[:-1])
    uh = _dot(_rms(xh_ref[...], g).astype(BF16), wu_ref[...])
    uh = jnp.where(seq_tile == 0, 0.0, uh)
    ext = jnp.concatenate([uh, u], axis=0)
    for gi, w in enumerate(POOL_WINDOWS):
        sl = slice(gi * POOL_GROUP_DIM, (gi + 1) * POOL_GROUP_DIM)
        s = ext[:, sl]
        k = 1
        while k < w:
            s = s + pltpu.roll(s, k, 0)
            k *= 2
        cnt = jnp.minimum(w, pos + 1).astype(F32)
        d_ref[:, sl] = (s[POOL_HALO:, :] / cnt - u[:, sl]).astype(BF16)


def _inproj_sample_kernel(x_ref, sp_ref, cos_ref, sin_ref, g_ref, wq_ref, wkv_ref, wg_ref, wu_ref, *outs,
                          past_len):
    d_ref = outs[-1]
    pos = jnp.full((x_ref.shape[0], 1), past_len, jnp.int32)
    u = _project(x_ref[...], g_ref[...], cos_ref[...], sin_ref[...], pos, wq_ref, wkv_ref, wg_ref, wu_ref,
                 outs[:-1])
    for gi, w in enumerate(POOL_WINDOWS):
        sl = slice(gi * POOL_GROUP_DIM, (gi + 1) * POOL_GROUP_DIM)
        s = u[:, sl]
        for k in range(1, w):
            s = s + sp_ref[POOL_BUF - k][:, sl]
        cnt = float(min(w, past_len + 1))
        d_ref[:, sl] = (s / cnt - u[:, sl]).astype(BF16)


def _inproj_outs(n_seq, rows, tm, n_sel, row_map, seq_tile):
    t = n_seq * rows
    shapes, specs = [], []

    def add(shape, dtype, block, index_map):
        shapes.append(jax.ShapeDtypeStruct(shape, dtype))
        specs.append(pl.BlockSpec(block, index_map))

    t_map4 = lambda i: (seq_tile(i)[0], 0, 0, seq_tile(i)[1])
    add((t, QPAD_DIM), BF16, (tm, QPAD_DIM), row_map)
    add((t, KV_DIM), F32, (tm, KV_DIM), row_map)
    for _ in range(3):
        add((n_seq, 2, KV_HALF, rows), F32, (1, 2, KV_HALF, tm), t_map4)
    add((t, KV_HALF + n_sel), BF16, (tm, KV_HALF + n_sel), row_map)
    for _ in range(5):
        add((t, KV_HALF), BF16, (tm, KV_HALF), row_map)
    add((t, LANES), F32, (tm, LANES), row_map)
    add((t, POOL_DIM), F32, (tm, POOL_DIM), row_map)
    add((t, POOL_DIM), BF16, (tm, POOL_DIM), row_map)
    return tuple(shapes), tuple(specs)


def _weight_specs(shapes):
    return [pl.BlockSpec(s, lambda i, _n=len(s): (0,) * _n) for s in shapes]


def _inproj_prompt(x, cos, sin, g_mix, wq, wkv, wg, wu, batch, seq, tm):
    tiles_per_seq = seq // tm
    halo_blocks = tm // POOL_HALO
    in_specs = [
        pl.BlockSpec((tm, D_MODEL), lambda i: (i, 0)),
        pl.BlockSpec((POOL_HALO, D_MODEL), lambda i: (jnp.maximum(i * halo_blocks - 1, 0), 0)),
        pl.BlockSpec((tm, LANES), lambda i: (i % tiles_per_seq, 0)),
        pl.BlockSpec((tm, LANES), lambda i: (i % tiles_per_seq, 0)),
    ] + _weight_specs([g_mix.shape, wq.shape, wkv.shape, wg.shape, wu.shape])
    out_shape, out_specs = _inproj_outs(batch, seq, tm, seq // L_SEL, lambda i: (i, 0),
                                        lambda i: (i // tiles_per_seq, i % tiles_per_seq))
    return pl.pallas_call(
        functools.partial(_inproj_prompt_kernel, tiles_per_seq=tiles_per_seq),
        grid=(batch * tiles_per_seq,), in_specs=in_specs, out_specs=out_specs,
        out_shape=out_shape, compiler_params=_params(1), name="inproj_prompt",
    )(x, x, cos, sin, g_mix, wq, wkv, wg, wu)


def _inproj_sample(x, sp_t, cos, sin, g_mix, wq, wkv, wg, wu, past_len):
    t = x.shape[0]
    in_specs = [
        pl.BlockSpec((t, D_MODEL), lambda i: (0, 0)),
        pl.BlockSpec(sp_t.shape, lambda i: (0, 0, 0)),
        pl.BlockSpec((t, LANES), lambda i: (0, 0)),
        pl.BlockSpec((t, LANES), lambda i: (0, 0)),
    ] + _weight_specs([g_mix.shape, wq.shape, wkv.shape, wg.shape, wu.shape])
    out_shape, out_specs = _inproj_outs(1, t, t, LANES, lambda i: (0, 0), lambda i: (0, 0))
    return pl.pallas_call(
        functools.partial(_inproj_sample_kernel, past_len=past_len),
        grid=(1,), in_specs=in_specs, out_specs=out_specs,
        out_shape=out_shape, compiler_params=_params(1), name="inproj_sample",
    )(x, sp_t, cos, sin, g_mix, wq, wkv, wg, wu)


def _compress_rows(x_refs, pe_ref, w1_ref, w2_ref, o_ref):
    nb = o_ref.shape[0]
    for c, x_ref in enumerate(x_refs):
        acc = jnp.zeros((nb, w1_ref.shape[3]), F32)
        for l in range(L_CMP):
            xl = x_ref[pl.ds(l, nb, stride=L_CMP), :] + pe_ref[c, l]
            acc = acc + _dot(xl.astype(BF16), w1_ref[c, l])
        hid = acc * jax.nn.sigmoid(acc)
        o_ref[:, c * KV_HALF:(c + 1) * KV_HALF] = _dot(hid.astype(BF16), w2_ref[c]).astype(o_ref.dtype)


def _compress_kernel(xk_ref, xv_ref, pe_ref, w1_ref, w2_ref, o_ref):
    _compress_rows((xk_ref, xv_ref), pe_ref, w1_ref, w2_ref, o_ref)


def _compress(rows, pe, w1, w2, nb_tile):
    n_blocks = rows.shape[0] // L_CMP
    nb_tile = int(np.gcd(nb_tile, n_blocks))
    return pl.pallas_call(
        _compress_kernel,
        grid=(n_blocks // nb_tile,),
        in_specs=[pl.BlockSpec((nb_tile * L_CMP, KV_HALF), lambda i: (i, 0)),
                  pl.BlockSpec((nb_tile * L_CMP, KV_HALF), lambda i: (i, 1))]
        + _weight_specs([pe.shape, w1.shape, w2.shape]),
        out_specs=pl.BlockSpec((nb_tile, KV_DIM), lambda i: (i, 0)),
        out_shape=jax.ShapeDtypeStruct((n_blocks, KV_DIM), BF16),
        compiler_params=_params(1), name="compress",
    )(rows, rows, pe, w1, w2)


def _compress_pages_kernel(x_ref, pet_ref, perm_ref, w1_ref, w2_ref, o_ref, xs_ref):
    n_pairs = x_ref.shape[0] // 2
    nb = o_ref.shape[0]
    perm = perm_ref[...]

    unroll = int(np.gcd(n_pairs, PAIR_UNROLL))

    def body(it, carry):
        for u in range(unroll):
            pp = it * unroll + u
            for c in range(2):
                xt = jnp.concatenate([x_ref[2 * pp, c], x_ref[2 * pp + 1, c]], axis=1) + pet_ref[c]
                xs_ref[c, pp] = _dot_nt(perm, xt.astype(BF16))
        return carry

    lax.fori_loop(0, n_pairs // unroll, body, 0)
    rows_per_l = 2 * LANES // L_CMP
    for c in range(2):
        acc = jnp.zeros((nb, w1_ref.shape[3]), F32)
        for l in range(L_CMP):
            xl = xs_ref[c, :, l * rows_per_l:(l + 1) * rows_per_l, :].reshape(nb, KV_HALF)
            acc = acc + _dot(xl.astype(BF16), w1_ref[c, l])
        hid = acc * jax.nn.sigmoid(acc)
        o_ref[:, c * KV_HALF:(c + 1) * KV_HALF] = _dot(hid.astype(BF16), w2_ref[c]).astype(o_ref.dtype)


def _compress_pages(cache_t, pe, w1, w2, pages_tile):
    n_phys, _, _, page = cache_t.shape
    assert page == LANES and n_phys % 2 == 0
    pages_tile = int(np.gcd(pages_tile, n_phys))
    per_page = page // L_CMP
    pet = jnp.tile(pe[:, :, 0, :].transpose(0, 2, 1), (1, 1, 2 * per_page))
    src = np.arange(2 * page)
    dst = (src % L_CMP) * (2 * per_page) + src // L_CMP
    perm = np.zeros((2 * page, 2 * page), np.float32)
    perm[dst, src] = 1.0
    perm = jnp.asarray(perm, BF16)
    return pl.pallas_call(
        _compress_pages_kernel,
        grid=(n_phys // pages_tile,),
        in_specs=[pl.BlockSpec((pages_tile, 2, KV_HALF, page), lambda i: (i, 0, 0, 0))]
        + _weight_specs([pet.shape, perm.shape, w1.shape, w2.shape]),
        out_specs=pl.BlockSpec((pages_tile * per_page, KV_DIM), lambda i: (i, 0)),
        out_shape=jax.ShapeDtypeStruct((n_phys * per_page, KV_DIM), BF16),
        scratch_shapes=[pltpu.VMEM((2, pages_tile // 2, 2 * page, KV_HALF), F32)],
        compiler_params=_params(1), name="compress_pages",
    )(cache_t, pet, perm, w1, w2)


def _stack_heads(q, g):
    return jnp.concatenate([q[:, (GROUP * g + r) * LANES:(GROUP * g + r + 1) * LANES] for r in range(GROUP)], axis=0)


def _masked_softmax(s, mask):
    s = jnp.where(mask, s, NEG)
    e = jnp.exp2(s - jnp.max(s, axis=-1, keepdims=True))
    return e / jnp.sum(e, axis=-1, keepdims=True) * mask.astype(F32)


def _assemble_heads(o_heads):
    chunks = []
    for c in range(N_HEADS // 2):
        g = (2 * c) // GROUP
        a, b = o_heads[2 * c], o_heads[2 * c + 1]
        if g == 1:
            a = pltpu.roll(a, HEAD_DIM, 1)
        else:
            b = pltpu.roll(b, HEAD_DIM, 1)
        lane = lax.broadcasted_iota(jnp.int32, a.shape, 1)
        chunks.append(jnp.where(lane < HEAD_DIM, a, b))
    return jnp.concatenate(chunks, axis=1)


def _select_blocks(score, n_pick):
    n_blk = score.shape[1]
    lane = lax.broadcasted_iota(jnp.int32, score.shape, 1)
    work = score
    picks = []
    for _ in range(n_pick):
        m = jnp.max(work, axis=-1, keepdims=True)
        idx = jnp.min(jnp.where(work == m, lane, n_blk), axis=-1, keepdims=True)
        picks.append(idx)
        work = jnp.where(lane == idx, -jnp.inf, work)
    return picks


def _select_blocks_t(score_t, n_pick):
    n_blk = score_t.shape[0]
    blk = lax.broadcasted_iota(jnp.int32, score_t.shape, 0)
    work = score_t
    sel = jnp.zeros(score_t.shape, F32)
    for _ in range(n_pick):
        m = jnp.max(work, axis=0, keepdims=True)
        idx = jnp.min(jnp.where(work == m, blk, n_blk), axis=0, keepdims=True)
        hit = blk == idx
        sel = jnp.where(hit, 1.0, sel)
        work = jnp.where(hit, -jnp.inf, work)
    return sel


def _cmp_attention(qg, kc, vc, pos_rows, n_sel):
    s = _dot_nt(qg, kc)
    j = lax.broadcasted_iota(jnp.int32, s.shape, 1)
    n = 2 * (j % n_sel) + j // n_sel
    mask = (n * L_CMP + (L_CMP - 1)) <= pos_rows
    p = _masked_softmax(s, mask)
    return _dot(p.astype(BF16), vc), p


def _cmp_topk_kernel(q_ref, c_ref, gates_ref, oc_ref, msel_ref, *, n_sel, n_top):
    i = pl.program_id(1)
    tq = q_ref.shape[0]
    q = q_ref[...]
    comp = c_ref[0]
    kc, vc = comp[:, :LANES], comp[:, LANES:]
    gates = gates_ref[...]
    pos = i * tq + lax.broadcasted_iota(jnp.int32, (tq, 1), 0)
    pos_rows = jnp.concatenate([pos] * GROUP, axis=0)
    blk = lax.broadcasted_iota(jnp.int32, (n_sel, tq), 0)
    cur = (i * tq + lax.broadcasted_iota(jnp.int32, (1, tq), 1)) // L_SEL
    forced = (blk == 0) | (blk == cur) | (blk == cur - 1)
    o_heads = []
    for g in range(N_KV_HEADS):
        o, p = _cmp_attention(_stack_heads(q, g), kc, vc, pos_rows, n_sel)
        imp = p[0:tq]
        for r in range(1, GROUP):
            imp = imp + p[r * tq:(r + 1) * tq]
        imp = (imp[:, :n_sel] + imp[:, n_sel:]).T
        score = jnp.where(forced, FORCED, jnp.where(blk < cur, imp, NEG))
        sel = jnp.where(score > NEG / 2, _select_blocks_t(score, n_top), 0.0)
        msel_ref[:, g * n_sel:(g + 1) * n_sel] = ((1.0 - sel) * NEG).T.astype(BF16)
        for r in range(GROUP):
            h = GROUP * g + r
            o_heads.append(o[r * tq:(r + 1) * tq] * gates[:, h:h + 1])
    oc_ref[...] = _assemble_heads(o_heads)


def _cmp_topk_prompt(q, comp, gates, batch, seq, tq):
    n_sel = seq // L_SEL
    n_c = comp.shape[1]
    tiles = seq // tq
    row = lambda b, i: (b * tiles + i, 0)
    return pl.pallas_call(
        functools.partial(_cmp_topk_kernel, n_sel=n_sel, n_top=min(TOP_N, n_sel)),
        grid=(batch, tiles),
        in_specs=[pl.BlockSpec((tq, QPAD_DIM), row),
                  pl.BlockSpec((1, n_c, KV_DIM), lambda b, i: (b, 0, 0)),
                  pl.BlockSpec((tq, LANES), row)],
        out_specs=(pl.BlockSpec((tq, NSA_DIM), row), pl.BlockSpec((tq, N_KV_HEADS * n_sel), row)),
        out_shape=(jax.ShapeDtypeStruct((batch * seq, NSA_DIM), F32),
                   jax.ShapeDtypeStruct((batch * seq, N_KV_HEADS * n_sel), BF16)),
        compiler_params=_params(2), name="cmp_topk_prompt",
    )(q, comp, gates)


def _flash_update(carry, q, k, v, bias):
    m, acc = carry
    s = _dot_nt(q, k)
    if bias is not None:
        s = s + bias
    m_new = jnp.maximum(m, jnp.max(s, axis=-1, keepdims=True))
    p = jnp.exp2(s - m_new).astype(BF16)
    acc = jnp.exp2(m - m_new) * acc + _dot(p, v)
    return m_new, acc


def _sel_win_kernel(q_ref, ks_ref, vs0_ref, vs1_ref, kw_ref, vw0_ref, vw1_ref, msel_ref, gates_ref, oc_ref, o_ref,
                    *, n_sel):
    i = pl.program_id(1)
    tq = q_ref.shape[0]
    hpc = HEADS_PER_CHAIN
    n_chain = N_HEADS // hpc
    rows = hpc * tq
    q = q_ref[...]
    gates = gates_ref[...]
    vs_refs, vw_refs = (vs0_ref, vs1_ref), (vw0_ref, vw1_ref)
    pos = i * tq + lax.broadcasted_iota(jnp.int32, (tq, 1), 0)
    key = lax.broadcasted_iota(jnp.int32, (1, tq), 1)
    d0 = pl.multiple_of(i * tq, tq)
    p0 = pl.multiple_of(jnp.maximum(i - 1, 0) * tq, tq)
    bias_d = jnp.concatenate([jnp.where(i * tq + key <= pos, 0.0, NEG)] * hpc, axis=0)
    in_win = (pos - ((i - 1) * tq + key) < WINDOW) & (i > 0)
    bias_p = jnp.concatenate([jnp.where(in_win, 0.0, NEG)] * hpc, axis=0)
    groups = [(c * hpc) // GROUP for c in range(n_chain)]
    qcs, qas = [], []
    for c in range(n_chain):
        g = groups[c]
        qc = jnp.concatenate([q[:, h * LANES:(h + 1) * LANES] for h in range(c * hpc, (c + 1) * hpc)], axis=0)
        msel = jnp.concatenate([msel_ref[:, g * n_sel:(g + 1) * n_sel]] * hpc, axis=0)
        qcs.append(qc)
        qas.append(jnp.concatenate([qc, msel], axis=1))
    init = (jnp.full((rows, 1), NEG, F32), jnp.zeros((rows, LANES), F32))

    def tiles(k_ref, v_refs, start):
        return k_ref[pl.ds(start, tq), :], [v_ref[pl.ds(start, tq), :] for v_ref in v_refs]

    kd, vd = tiles(ks_ref, vs_refs, d0)
    sel = tuple(_flash_update(init, qas[c], kd, vd[groups[c]], bias_d) for c in range(n_chain))

    def body(kt, carries):
        k, v = tiles(ks_ref, vs_refs, pl.multiple_of(kt * tq, tq))
        return tuple(_flash_update(carries[c], qas[c], k, v[groups[c]], None) for c in range(n_chain))

    sel = lax.fori_loop(0, i, body, sel)
    kd, vd = tiles(kw_ref, vw_refs, d0)
    kp, vp = tiles(kw_ref, vw_refs, p0)
    o_heads = []
    for c in range(n_chain):
        g = groups[c]
        win = _flash_update(_flash_update(init, qcs[c], kd, vd[g], bias_d), qcs[c], kp, vp[g], bias_p)
        den = (1 - g) * HEAD_DIM
        o_s = sel[c][1] / sel[c][1][:, den:den + 1]
        o_w = win[1] / win[1][:, den:den + 1]
        for r in range(hpc):
            h = c * hpc + r
            rs = slice(r * tq, (r + 1) * tq)
            o_heads.append(o_s[rs] * gates[:, N_HEADS + h:N_HEADS + h + 1]
                           + o_w[rs] * gates[:, 2 * N_HEADS + h:2 * N_HEADS + h + 1])
    o_ref[...] = (_assemble_heads(o_heads) + oc_ref[...]).astype(o_ref.dtype)


def _sel_win_prompt(q, ksb, vs0, vs1, kwb, vw0, vw1, msel, gates, oc, batch, seq, tq):
    assert tq == WINDOW
    n_sel = seq // L_SEL
    tiles = seq // tq
    row = lambda b, i: (b * tiles + i, 0)
    whole = lambda b, i: (b, 0)
    return pl.pallas_call(
        functools.partial(_sel_win_kernel, n_sel=n_sel),
        grid=(batch, tiles),
        in_specs=[pl.BlockSpec((tq, QPAD_DIM), row),
                  pl.BlockSpec((seq, KV_HALF + n_sel), whole, pipeline_mode=pl.Buffered(1))]
        + [pl.BlockSpec((seq, KV_HALF), whole, pipeline_mode=pl.Buffered(1))] * 5
        + [pl.BlockSpec((tq, N_KV_HEADS * n_sel), row),
           pl.BlockSpec((tq, LANES), row),
           pl.BlockSpec((tq, NSA_DIM), row)],
        out_specs=pl.BlockSpec((tq, NSA_DIM), row),
        out_shape=jax.ShapeDtypeStruct((batch * seq, NSA_DIM), BF16),
        compiler_params=_params(2), name="sel_win_prompt",
    )(q, ksb, vs0, vs1, kwb, vw0, vw1, msel, gates, oc)


def _sample_cmp_kernel(q_ref, c_ref, gates_ref, oc_ref, idx_ref, *, n_blk, n_pick, past_len):
    n_sb = q_ref.shape[0]
    pos_rows = jnp.full((N_HEADS, 1), past_len, jnp.int32)
    imps = []
    for s in range(n_sb):
        comp = c_ref[s]
        o, p = _cmp_attention(q_ref[s], comp[:, :LANES], comp[:, LANES:], pos_rows, n_blk)
        oc_ref[s] = o * gates_ref[s][:, 0:1]
        imps += [jnp.sum(p[GROUP * g:GROUP * (g + 1)], axis=0, keepdims=True) for g in range(N_KV_HEADS)]
    imp = jnp.concatenate(imps, axis=0)
    imp = imp[:, :n_blk] + imp[:, n_blk:]
    cur = past_len // L_SEL
    blk = lax.broadcasted_iota(jnp.int32, imp.shape, 1)
    forced = (blk == 0) | (blk == cur) | (blk == cur - 1)
    score = jnp.where(forced, FORCED, jnp.where(blk < cur, imp, NEG))
    lane = lax.broadcasted_iota(jnp.int32, idx_ref.shape, 1)
    out = jnp.zeros(idx_ref.shape, jnp.int32)
    for k, idx in enumerate(_select_blocks(score, n_pick)):
        out = jnp.where(lane == k, idx, out)
    idx_ref[...] = out


def _sample_cmp(q3, comp, gates3, n_pick, past_len, seqs_per_step):
    n_seq = q3.shape[0]
    n_c = comp.shape[1]
    n_sb = int(np.gcd(seqs_per_step, n_seq))
    blk3 = lambda b: (b, 0, 0)
    return pl.pallas_call(
        functools.partial(_sample_cmp_kernel, n_blk=n_c // 2, n_pick=n_pick, past_len=past_len),
        grid=(n_seq // n_sb,),
        in_specs=[pl.BlockSpec((n_sb, N_HEADS, LANES), blk3),
                  pl.BlockSpec((n_sb, n_c, KV_DIM), blk3),
                  pl.BlockSpec((n_sb, N_HEADS, LANES), blk3)],
        out_specs=(pl.BlockSpec((n_sb, N_HEADS, LANES), blk3),
                   pl.BlockSpec((n_sb * N_KV_HEADS, LANES), lambda b: (b, 0))),
        out_shape=(jax.ShapeDtypeStruct((n_seq, N_HEADS, LANES), F32),
                   jax.ShapeDtypeStruct((n_seq * N_KV_HEADS, LANES), jnp.int32)),
        compiler_params=_params(1), name="sample_cmp",
    )(q3, comp, gates3)


def _softmax_with_new(s, valid, s_new):
    s = jnp.where(valid, s, NEG)
    m = jnp.maximum(jnp.max(s, axis=-1, keepdims=True), s_new)
    e = jnp.exp2(s - m) * valid.astype(F32)
    e_new = jnp.exp2(s_new - m)
    den = jnp.sum(e, axis=-1, keepdims=True) + e_new
    return e / den, e_new / den


def _sample_sel_win_kernel(pg_ref, hf_ref, q_ref, win_ref, kvs_ref, kvw_ref, gates_ref, oc_ref, cache_ref,
                           o_ref, buf_ref, sem_ref, *, n_pick):
    b = pl.program_id(0)
    n_b = pl.num_programs(0)
    n_dma = N_KV_HEADS * n_pick
    page = buf_ref.shape[-1]

    def page_copy(bb, slot, j):
        return pltpu.make_async_copy(cache_ref.at[pg_ref[bb * n_dma + j]], buf_ref.at[slot, j], sem_ref.at[slot])

    @pl.when(b == 0)
    def _():
        for j in range(n_dma):
            page_copy(0, 0, j).start()

    @pl.when(b + 1 < n_b)
    def _():
        for j in range(n_dma):
            page_copy(b + 1, (b + 1) % 2, j).start()

    q = q_ref[0]
    qf = q.astype(F32)
    gates = gates_ref[0]
    row = lax.broadcasted_iota(jnp.int32, (N_HEADS, LANES), 0)

    def new_key(kv_ref):
        kv = kv_ref[0].astype(BF16).astype(F32)
        return jnp.sum(qf * kv[:, :KV_HALF], axis=-1, keepdims=True), kv[:, KV_HALF:]

    win = win_ref[0]
    kt, vt = win[0].astype(BF16), win[1].astype(BF16)
    s_new, v_new = new_key(kvw_ref)
    lane_w = lax.broadcasted_iota(jnp.int32, (N_HEADS, kt.shape[1]), 1)
    p, p_new = _softmax_with_new(_dot(q, kt), lane_w >= 1, s_new)
    o_w = _dot_nt(p.astype(BF16), vt) + p_new.astype(BF16).astype(F32) * v_new

    slot = b % 2
    for j in range(n_dma):
        page_copy(b, slot, j).wait()
    s_new, v_new = new_key(kvs_ref)
    lane_p = lax.broadcasted_iota(jnp.int32, (N_HEADS, page), 1)
    o_s = jnp.zeros((N_HEADS, LANES), F32)
    for g in range(N_KV_HEADS):
        kts, vts, valids = [], [], []
        for k in range(n_pick):
            j = g * n_pick + k
            pg = buf_ref[slot, j]
            kts.append(pg[0].astype(BF16))
            vts.append(pg[1].astype(BF16))
            half = hf_ref[b * n_dma + j]
            valids.append((lane_p // L_SEL) == half)
        kt_all = jnp.concatenate(kts, axis=1)
        vt_all = jnp.concatenate(vts, axis=1)
        p, p_new = _softmax_with_new(_dot(q, kt_all), jnp.concatenate(valids, axis=1), s_new)
        o_g = _dot_nt(p.astype(BF16), vt_all) + p_new.astype(BF16).astype(F32) * v_new
        o_s = jnp.where(row // GROUP == g, o_g, o_s)
    o_ref[0] = oc_ref[0] + o_s * gates[:, 1:2] + o_w * gates[:, 2:3]


def _sample_sel_win(pages, halves, q3, win_t, kvs, kvw, gates3, oc, cache_t, n_pick):
    n_seq = q3.shape[0]
    n_dma = N_KV_HEADS * n_pick
    blk3 = lambda b, pg, hf: (b, 0, 0)
    grid_spec = pltpu.PrefetchScalarGridSpec(
        num_scalar_prefetch=2, grid=(n_seq,),
        in_specs=[pl.BlockSpec((1, N_HEADS, LANES), blk3),
                  pl.BlockSpec((1,) + win_t.shape[1:], lambda b, pg, hf: (b, 0, 0, 0)),
                  pl.BlockSpec((1, 1, KV_DIM), blk3),
                  pl.BlockSpec((1, 1, KV_DIM), blk3),
                  pl.BlockSpec((1, N_HEADS, LANES), blk3),
                  pl.BlockSpec((1, N_HEADS, LANES), blk3),
                  pl.BlockSpec(memory_space=pl.ANY)],
        out_specs=pl.BlockSpec((1, N_HEADS, LANES), blk3),
        scratch_shapes=[pltpu.VMEM((2, n_dma) + cache_t.shape[1:], F32), pltpu.SemaphoreType.DMA((2,))],
    )
    return pl.pallas_call(
        functools.partial(_sample_sel_win_kernel, n_pick=n_pick),
        grid_spec=grid_spec,
        out_shape=jax.ShapeDtypeStruct((n_seq, N_HEADS, LANES), F32),
        compiler_params=_params(1), name="sample_sel_win",
    )(pages, halves, q3, win_t, kvs, kvw, gates3, oc, cache_t)


def _tail_kernel(x_ref, attn_ref, d_ref, xs_ref, attns_ref, ds_ref, gmix_ref, wgm_ref, wpool_ref, pscale_ref,
                 wbn_ref, wbp_ref, wout_ref, gffn_ref, wr_ref, br_ref, h_ref, xn2_ref, route_ref, cout_ref, run_ref,
                 *, n_experts):
    i = pl.program_id(0)
    is_sample = i == pl.num_programs(0) - 1
    x = jnp.where(is_sample, xs_ref[...], x_ref[...])
    attn = jnp.where(is_sample, attns_ref[...], attn_ref[...])
    d = jnp.where(is_sample, ds_ref[...], d_ref[...])
    tm = x.shape[0]
    xn = _rms(x, gmix_ref[...]).astype(BF16)
    gm = jax.nn.sigmoid(_dot(xn, wgm_ref[...]))
    pool = jnp.concatenate(
        [_dot(d[:, gi * POOL_GROUP_DIM:(gi + 1) * POOL_GROUP_DIM], wpool_ref[gi]) for gi in range(len(POOL_WINDOWS))],
        axis=1) * pscale_ref[...]
    merged = (gm[:, :D_MODEL] * _dot(attn, wbn_ref[...])
              + gm[:, D_MODEL:] * _dot(pool.astype(BF16), wbp_ref[...]))
    h = x + _dot(merged.astype(BF16), wout_ref[...])
    h_ref[...] = h
    xn2 = _rms(h, gffn_ref[...]).astype(BF16)
    xn2_ref[...] = xn2
    logits = _dot(xn2, wr_ref[...]) + br_ref[...]
    lane = lax.broadcasted_iota(jnp.int32, (tm, LANES), 1)
    work = jnp.where(lane < n_experts, logits, -jnp.inf)
    vals, onehots = [], []
    for _ in range(TOP_K):
        m = jnp.max(work, axis=-1, keepdims=True)
        idx = jnp.min(jnp.where(work == m, lane, LANES), axis=-1, keepdims=True)
        vals.append(m)
        onehots.append(lane == idx)
        work = jnp.where(lane == idx, -jnp.inf, work)
    es = [jnp.exp(v - vals[0]) for v in vals]
    tot = es[0] + es[1] + es[2] + es[3]

    @pl.when(i == 0)
    def _():
        run_ref[...] = jnp.zeros(run_ref.shape, F32)

    chosen = jnp.zeros((tm, LANES), F32)
    for k in range(TOP_K):
        chosen = jnp.where(onehots[k], 1.0, chosen)
    earlier = lax.broadcasted_iota(jnp.int32, (tm, tm), 0) > lax.broadcasted_iota(jnp.int32, (tm, tm), 1)
    before = run_ref[...] + _dot(jnp.where(earlier, 1.0, 0.0).astype(BF16), chosen.astype(BF16))
    route = jnp.zeros((tm, LANES), F32)
    lane_f = lane.astype(F32)
    for k in range(TOP_K):
        e_k = jnp.sum(jnp.where(onehots[k], lane_f, 0.0), axis=-1, keepdims=True)
        r_k = jnp.sum(jnp.where(onehots[k], before, 0.0), axis=-1, keepdims=True)
        route = jnp.where(lane == k, e_k, route)
        route = jnp.where(lane == TOP_K + k, es[k] / tot, route)
        route = jnp.where(lane == 2 * TOP_K + k, r_k, route)
    route_ref[...] = route
    run_ref[...] = run_ref[...] + jnp.sum(chosen, axis=0, keepdims=True)
    cout_ref[...] = run_ref[...]


def _tail(x, attn, d, xs, attn_s, d_s, weights, n_experts, tm):
    n_p = x.shape[0] // tm
    t = (n_p + 1) * tm
    row = lambda i: (i, 0)
    prompt = lambda i: (jnp.minimum(i, n_p - 1), 0)
    fixed = lambda i: (0, 0)
    return pl.pallas_call(
        functools.partial(_tail_kernel, n_experts=n_experts),
        grid=(n_p + 1,),
        in_specs=[pl.BlockSpec((tm, D_MODEL), prompt), pl.BlockSpec((tm, NSA_DIM), prompt),
                  pl.BlockSpec((tm, POOL_DIM), prompt), pl.BlockSpec((tm, D_MODEL), fixed),
                  pl.BlockSpec((tm, NSA_DIM), fixed), pl.BlockSpec((tm, POOL_DIM), fixed)]
        + _weight_specs([w.shape for w in weights]),
        out_specs=(pl.BlockSpec((tm, D_MODEL), row), pl.BlockSpec((tm, D_MODEL), row),
                   pl.BlockSpec((tm, LANES), row), pl.BlockSpec((1, LANES), fixed)),
        out_shape=(jax.ShapeDtypeStruct((t, D_MODEL), F32), jax.ShapeDtypeStruct((t, D_MODEL), BF16),
                   jax.ShapeDtypeStruct((t, LANES), F32), jax.ShapeDtypeStruct((1, LANES), F32)),
        scratch_shapes=[pltpu.VMEM((1, LANES), F32)],
        compiler_params=_params(1), name="tail",
    )(x, attn, d, xs, attn_s, d_s, *weights)


def _moe_kernel(te_ref, tv_ref, tf_ref, x_ref, wg_ref, bg_ref, wu_ref, bu_ref, wd_ref, bd_ref, y_ref,
                wgb_ref, wub_ref, wdb_ref):
    t = pl.program_id(0)

    @pl.when(tf_ref[t] > 0)
    def _():
        wgb_ref[...] = wg_ref[0].astype(BF16)
        wub_ref[...] = wu_ref[0].astype(BF16)
        wdb_ref[...] = wd_ref[0].astype(BF16)

    @pl.when(tv_ref[t] > 0)
    def _():
        x = x_ref[...]
        gate = jnp.minimum(_dot(x, wgb_ref[...]) + bg_ref[0], SWIGLU_LIMIT)
        up = jnp.clip(_dot(x, wub_ref[...]) + bu_ref[0], -SWIGLU_LIMIT, SWIGLU_LIMIT)
        hid = (up + 1.0) * gate * jax.nn.sigmoid(SWIGLU_ALPHA * gate)
        y_ref[...] = (_dot(hid.astype(BF16), wdb_ref[...]) + bd_ref[0]).astype(y_ref.dtype)

    @pl.when(tv_ref[t] == 0)
    def _():
        y_ref[...] = jnp.zeros(y_ref.shape, y_ref.dtype)


def _moe(tile_e, tile_v, tile_f, xs, wg, bg, wu, bu, wd, bd, tme):
    p_pad = xs.shape[0]
    d_e = wg.shape[2]
    wspec = lambda shape: pl.BlockSpec((1,) + shape, lambda t, te, tv, tf: (te[t], 0, 0))
    grid_spec = pltpu.PrefetchScalarGridSpec(
        num_scalar_prefetch=3, grid=(p_pad // tme,),
        in_specs=[pl.BlockSpec((tme, D_MODEL), lambda t, te, tv, tf: (t, 0)),
                  wspec((D_MODEL, d_e)), wspec((1, d_e)), wspec((D_MODEL, d_e)), wspec((1, d_e)),
                  wspec((d_e, D_MODEL)), wspec((1, D_MODEL))],
        out_specs=pl.BlockSpec((tme, D_MODEL), lambda t, te, tv, tf: (t, 0)),
        scratch_shapes=[pltpu.VMEM((D_MODEL, d_e), BF16), pltpu.VMEM((D_MODEL, d_e), BF16),
                        pltpu.VMEM((d_e, D_MODEL), BF16)],
    )
    return pl.pallas_call(
        _moe_kernel, grid_spec=grid_spec,
        out_shape=jax.ShapeDtypeStruct((p_pad, D_MODEL), BF16),
        compiler_params=_params(1), name="moe_experts",
    )(tile_e, tile_v, tile_f, xs, wg, bg, wu, bu, wd, bd)


def _final_kernel(h_ref, y_ref, route_ref, g_ref, o_ref):
    route = route_ref[...]
    acc = h_ref[...]
    for k in range(TOP_K):
        acc = acc + y_ref[k].astype(F32) * route[:, TOP_K + k:TOP_K + k + 1]
    o_ref[...] = _rms(acc, g_ref[...])


def _final(h, yg, route, g_final, tm, first, n_tiles):
    src = lambda i: (i + first, 0)
    return pl.pallas_call(
        _final_kernel, grid=(n_tiles,),
        in_specs=[pl.BlockSpec((tm, D_MODEL), src), pl.BlockSpec((TOP_K, tm, D_MODEL), lambda i: (0, i + first, 0)),
                  pl.BlockSpec((tm, LANES), src), pl.BlockSpec((1, D_MODEL), lambda i: (0, 0))],
        out_specs=pl.BlockSpec((tm, D_MODEL), lambda i: (i, 0)),
        out_shape=jax.ShapeDtypeStruct((n_tiles * tm, D_MODEL), F32),
        compiler_params=_params(1), name="final_norm",
    )(h, yg, route, g_final)


def _rope_tables(pos):
    half = HEAD_DIM // 2
    inv = ROPE_THETA ** (-jnp.arange(half, dtype=F32) / half)
    ang = pos.astype(F32)[:, None] * inv[None, :]
    cos, sin = jnp.cos(ang), jnp.sin(ang)
    cos = jnp.concatenate([cos, cos] * (LANES // HEAD_DIM), axis=1)
    sin = jnp.concatenate([-sin, sin] * (LANES // HEAD_DIM), axis=1)
    return cos, sin


def _split_w_in(w_in):
    o_q = NSA_DIM
    o_g = o_q + 3 * KV_DIM
    o_u = o_g + 3 * N_HEADS
    o_m = o_u + POOL_DIM
    wq = w_in[:, :o_q].reshape(D_MODEL, N_HEADS, HEAD_DIM)
    zeros = jnp.zeros_like(wq)
    lo = jnp.concatenate([wq, zeros], axis=2)
    hi = jnp.concatenate([zeros, wq], axis=2)
    in_hi = (jnp.arange(N_HEADS) // GROUP == 1)[None, :, None]
    wq_pad = jnp.where(in_hi, hi, lo).reshape(D_MODEL, QPAD_DIM)
    wkv = w_in[:, o_q:o_g]
    wg = w_in[:, o_g:o_u].reshape(D_MODEL, N_HEADS, 3).transpose(0, 2, 1).reshape(D_MODEL, 3 * N_HEADS)
    wg = jnp.pad(wg, ((0, 0), (0, LANES - 3 * N_HEADS)))
    wu = w_in[:, o_u:o_m]
    wgm = w_in[:, o_m:]
    return [w.astype(BF16) for w in (wq_pad, wkv, wg, wu, wgm)]


def _compress_weights(cmp_pe, cmp_w1, cmp_w2):
    eye = jnp.eye(N_KV_HEADS, dtype=F32)
    w1 = jnp.einsum('cldh,gf->clgdfh', cmp_w1, eye).reshape(2, L_CMP, KV_HALF, N_KV_HEADS * CMP_HIDDEN)
    w2 = jnp.einsum('chd,gf->cghfd', cmp_w2, eye).reshape(2, N_KV_HEADS * CMP_HIDDEN, KV_HALF)
    pe = jnp.broadcast_to(cmp_pe.transpose(1, 0, 2)[:, :, None, :], (2, L_CMP, N_KV_HEADS, HEAD_DIM))
    return pe.reshape(2, L_CMP, 1, KV_HALF), w1.astype(BF16), w2.astype(BF16)


def _even_odd(comp, n_seq):
    n_c = comp.shape[0] // n_seq
    return comp.reshape(n_seq, n_c // 2, 2, KV_DIM).transpose(0, 2, 1, 3).reshape(n_seq, n_c, KV_DIM)


def _fix_sample_heads(o3):
    n_seq = o3.shape[0]
    o4 = o3.reshape(n_seq, N_KV_HEADS, GROUP, N_KV_HEADS, HEAD_DIM)
    return jnp.concatenate([o4[:, g, :, g, :] for g in range(N_KV_HEADS)], axis=1).reshape(n_seq, NSA_DIM)


def _gates_by_head(gates):
    g = gates[:, :3 * N_HEADS].reshape(-1, 3, N_HEADS).transpose(0, 2, 1)
    return jnp.pad(g, ((0, 0), (0, 0), (0, LANES - 3)))


def _key_minor(x):
    n, p = x.shape[:2]
    return x.transpose(0, 2, 3, 4, 1).reshape(n, 2, KV_HALF, p)


def _from_key_minor(xt):
    n, _, _, p = xt.shape
    return xt.reshape(n, 2, N_KV_HEADS, HEAD_DIM, p).transpose(0, 4, 1, 2, 3)


def _expert_order(route, counts, n_experts, tme):
    t = route.shape[0]
    n_pairs = t * TOP_K
    e = route[:, :TOP_K].astype(jnp.int32)
    rank = route[:, 2 * TOP_K:3 * TOP_K].astype(jnp.int32)
    counts = counts[0, :n_experts].astype(jnp.int32)
    padded = ((counts + tme - 1) // tme) * tme
    ends = jnp.cumsum(padded)
    off = ends - padded
    pad_before = off - (jnp.cumsum(counts) - counts)
    experts = jnp.arange(n_experts, dtype=jnp.int32)
    pair_slot = rank + jnp.sum(jnp.where(e[:, :, None] == experts, off, 0), axis=-1)
    n_tiles = -(-n_pairs // tme) + n_experts
    p_pad = n_tiles * tme
    tok = jnp.broadcast_to(jnp.arange(t, dtype=jnp.int32)[:, None], (t, TOP_K))
    _, tok_sorted = lax.sort((pair_slot.reshape(n_pairs), tok.reshape(n_pairs)), num_keys=1)
    slot = jnp.arange(p_pad, dtype=jnp.int32)
    slot_e = jnp.sum((slot[:, None] >= ends[None, :]).astype(jnp.int32), axis=-1)
    slot_e = jnp.minimum(slot_e, n_experts - 1)
    slot_pad = jnp.sum(jnp.where(slot_e[:, None] == experts, pad_before, 0), axis=-1)
    src_tok = tok_sorted[jnp.clip(slot - slot_pad, 0, n_pairs - 1)]
    tile_start = jnp.arange(n_tiles, dtype=jnp.int32) * tme
    tile_e = jnp.minimum(jnp.sum((tile_start[:, None] >= ends[None, :]).astype(jnp.int32), axis=-1), n_experts - 1)
    tile_v = (tile_start < ends[-1]).astype(jnp.int32)
    tile_f = jnp.concatenate([jnp.ones((1,), jnp.int32), (tile_e[1:] != tile_e[:-1]).astype(jnp.int32)])
    return pair_slot, src_tok, tile_e, tile_v, tile_f


def kernel(x_prompt, x_sample, cache_kv_cmp, cache_kv_sel, state_kv_win, state_pool, page_table, g_mix, w_in, cmp_pe, cmp_w1, cmp_w2, w_pool, pool_scale, w_br_nsa, w_br_pool, w_out, g_ffn, w_router, b_router, w_gate, b_gate, w_up, b_up, w_down, b_down, g_final):
    batch, seq, _ = x_prompt.shape
    n_seq, dec_seq, _ = x_sample.shape
    depth = g_mix.shape[0]
    assert depth == 1 and dec_seq == 1
    n_phys, page_size = cache_kv_cmp.shape[1:3]
    n_pages = page_table.shape[1]
    past_len = n_pages * page_size
    n_experts = w_router.shape[2]
    t_p = batch * seq
    tm = min(512, seq)

    wq, wkv, wg, wu, wgm = _split_w_in(w_in[0])
    gmix = g_mix[0][None, :]
    pe, w1, w2 = _compress_weights(cmp_pe[0], cmp_w1[0], cmp_w2[0])
    cos_p, sin_p = _rope_tables(jnp.arange(seq, dtype=jnp.int32))
    cos_s, sin_s = _rope_tables(jnp.full((n_seq,), past_len, jnp.int32))
    wr = jnp.pad(w_router[0], ((0, 0), (0, LANES - n_experts))).astype(BF16)
    br = jnp.pad(b_router[0], (0, LANES - n_experts))[None, :]
    tail_w = [gmix, wgm, w_pool[0].astype(BF16), pool_scale[0][None, :], w_br_nsa[0].astype(BF16),
              w_br_pool[0].astype(BF16), w_out[0].astype(BF16), g_ffn[0][None, :], wr, br]

    xp = x_prompt.reshape(t_p, D_MODEL)
    (q_p, kvc_p, kvct_p, kvst_p, kvwt_p, ksb_p, vs0_p, vs1_p, kwb_p, vw0_p, vw1_p, gates_p, u_p, d_p) = _inproj_prompt(
        xp, cos_p, sin_p, gmix, wq, wkv, wg, wu, batch, seq, tm)
    comp_p = _even_odd(_compress(kvc_p, pe, w1, w2, 256), batch)
    tq = min(256, seq)
    oc_p, msel_p = _cmp_topk_prompt(q_p, comp_p, gates_p, batch, seq, tq)
    attn_p = _sel_win_prompt(q_p, ksb_p, vs0_p, vs1_p, kwb_p, vw0_p, vw1_p, msel_p, gates_p, oc_p, batch, seq, WINDOW)

    xs = x_sample.reshape(n_seq, D_MODEL)
    sp_t = state_pool[0].transpose(1, 0, 2)
    (q_s, kvc_s, _, kvst_s, kvwt_s, _, _, _, _, _, _, gates_s, u_s, d_s) = _inproj_sample(
        xs, sp_t, cos_s, sin_s, gmix, wq, wkv, wg, wu, past_len)
    kvs_s = kvst_s.reshape(KV_DIM, n_seq).T
    kvw_s = kvwt_s.reshape(KV_DIM, n_seq).T
    blocks_per_page = page_size // L_CMP
    cache_cmp_t = _key_minor(cache_kv_cmp[0])
    comp_all = _compress_pages(cache_cmp_t, pe, w1, w2, 64)
    comp_s = comp_all.reshape(n_phys, blocks_per_page, KV_DIM)[page_table].reshape(n_seq * n_pages * blocks_per_page, KV_DIM)
    comp_s = _even_odd(comp_s, n_seq)
    q3 = q_s.reshape(n_seq, N_HEADS, LANES)
    gates3 = _gates_by_head(gates_s)
    n_blk = past_len // L_SEL
    n_pick = min(TOP_N, n_blk + 1) - 1
    oc_s, idx_s = _sample_cmp(q3, comp_s, gates3, n_pick, past_len, 16)
    top_idx = idx_s.reshape(n_seq, N_KV_HEADS, LANES)[:, :, :n_pick]
    sel_per_page = page_size // L_SEL
    pages_b = jnp.broadcast_to(page_table[:, None, :], (n_seq, N_KV_HEADS, n_pages))
    phys = jnp.take_along_axis(pages_b, top_idx // sel_per_page, axis=2).reshape(-1)
    halves = (top_idx % sel_per_page).reshape(-1)
    w_buf = state_kv_win.shape[2]
    assert w_buf == WINDOW
    win_t = _key_minor(state_kv_win[0])
    o3 = _sample_sel_win(phys, halves, q3, win_t, kvs_s[:, None, :], kvw_s[:, None, :], gates3, oc_s,
                         _key_minor(cache_kv_sel[0]), n_pick)
    attn_s = _fix_sample_heads(o3).astype(BF16)

    assert n_seq <= tm
    pad_rows = lambda a: jnp.pad(a, ((0, tm - n_seq), (0, 0)))
    h_all, xn2_all, route, counts = _tail(xp, attn_p, d_p, pad_rows(xs), pad_rows(attn_s), pad_rows(d_s),
                                          tail_w, n_experts, tm)
    tme = 512
    pair_slot, src_tok, tile_e, tile_v, tile_f = _expert_order(route, counts, n_experts, tme)
    y_sorted = _moe(tile_e, tile_v, tile_f, xn2_all[src_tok], w_gate[0], b_gate[0][:, None, :],
                    w_up[0], b_up[0][:, None, :], w_down[0], b_down[0][:, None, :], tme)
    y_pairs = y_sorted[pair_slot.T]
    gfin = g_final[None, :]
    n_p = t_p // tm
    y_p = _final(h_all, y_pairs, route, gfin, tm, 0, n_p)
    y_s = _final(h_all, y_pairs, route, gfin, tm, n_p, 1)[:n_seq]

    w_p = min(WINDOW, seq)
    kv_row = (2, N_KV_HEADS, HEAD_DIM)
    new_col = kvwt_s.reshape(2, KV_HALF, n_seq).transpose(2, 0, 1)[..., None]
    win_new_t = jnp.concatenate([win_t[..., 1:], new_col], axis=-1)
    return (
        y_p.reshape(batch, seq, D_MODEL),
        y_s.reshape(n_seq, 1, D_MODEL),
        _from_key_minor(kvct_p)[None],
        kvc_s.reshape((1, n_seq, 1) + kv_row),
        _from_key_minor(kvst_p)[None],
        kvs_s.reshape((1, n_seq, 1) + kv_row),
        _from_key_minor(kvwt_p[..., seq - w_p:])[None],
        _from_key_minor(win_new_t)[None],
        u_p.reshape(batch, seq, POOL_DIM)[None, :, seq - POOL_BUF:],
        jnp.concatenate([state_pool[0], u_s[:, None, :]], axis=1)[None, :, 1:],
    )
```

```python
import functools
import math

import jax
import jax.numpy as jnp
import numpy as np
from jax import lax
from jax.experimental import pallas as pl
from jax.experimental.pallas import tpu as pltpu

D_MODEL = 1024
N_HEADS = 8
N_KV_HEADS = 2
HEAD_DIM = 64
GROUP = N_HEADS // N_KV_HEADS
NSA_DIM = N_HEADS * HEAD_DIM
KV_DIM = 2 * N_KV_HEADS * HEAD_DIM
KV_HALF = KV_DIM // 2
L_CMP = 32
L_SEL = 64
CMP_HIDDEN = 2 * HEAD_DIM
TOP_N = 16
WINDOW = 512
ROPE_THETA = 10000.0
POOL_WINDOWS = (2, 4, 8, 16)
POOL_DIM = D_MODEL // 2
POOL_GROUP_DIM = POOL_DIM // len(POOL_WINDOWS)
POOL_BUF = max(POOL_WINDOWS) - 1
POOL_HALO = POOL_BUF + 1
TOP_K = 4
SWIGLU_LIMIT = 7.0
SWIGLU_ALPHA = 1.702
RMS_EPS = 1e-6
NEG = -1e30
FORCED = 1e9

LANES = 128
QPAD_DIM = N_HEADS * LANES
VMEM_LIMIT = 56 * 1024 * 1024
Q_SCALE = HEAD_DIM ** -0.5 * math.log2(math.e)
ROW_TILE = 512
CMP_Q_TILE = 256
CMP_BLOCK_TILE = 256
MOE_TILE = 512
SAMPLE_SEQS_PER_STEP = 16
PAIR_UNROLL = 8
HEADS_PER_CHAIN = GROUP

BF16 = jnp.bfloat16
F32 = jnp.float32


def _params(n_grid, vmem=VMEM_LIMIT):
    return pltpu.CompilerParams(dimension_semantics=("arbitrary",) * n_grid, vmem_limit_bytes=vmem)


def _rms(x, g):
    r = lax.rsqrt(jnp.mean(x * x, axis=-1, keepdims=True) + RMS_EPS)
    return x * r * g


def _dot(a, b):
    return jnp.dot(a, b, preferred_element_type=F32)


def _dot_nt(a, b):
    return lax.dot_general(a, b, (((1,), (1,)), ((), ())), preferred_element_type=F32)


def _rope_chunk(x, cos, sin_signed):
    lane = lax.broadcasted_iota(jnp.int32, x.shape, 1)
    first = (lane % HEAD_DIM) < (HEAD_DIM // 2)
    swapped = jnp.where(first, pltpu.roll(x, LANES - HEAD_DIM // 2, 1), pltpu.roll(x, HEAD_DIM // 2, 1))
    return x * cos + swapped * sin_signed


def _project(x, g, cos, sin, pos, wq_ref, wkv_ref, wg_ref, wu_ref, outs):
    (q_ref, kvc_ref, kvct_ref, kvst_ref, kvwt_ref, ksb_ref, vs0_ref, vs1_ref, kwb_ref, vw0_ref, vw1_ref,
     gates_ref, u_ref) = outs
    xn = _rms(x, g).astype(BF16)
    q = _dot(xn, wq_ref[...])
    for h in range(N_HEADS):
        sl = slice(h * LANES, (h + 1) * LANES)
        q_ref[:, sl] = (_rope_chunk(q[:, sl], cos, sin) * Q_SCALE).astype(BF16)
    kv = _dot(xn, wkv_ref[...])
    n_sel = ksb_ref.shape[1] - KV_HALF
    blk = lax.broadcasted_iota(jnp.int32, (x.shape[0], n_sel), 1)
    for j, t_ref in enumerate((kvct_ref, kvst_ref, kvwt_ref)):
        k = _rope_chunk(kv[:, j * KV_DIM:j * KV_DIM + KV_HALF], cos, sin)
        v = kv[:, j * KV_DIM + KV_HALF:(j + 1) * KV_DIM]
        t_ref[0, 0] = k.T
        t_ref[0, 1] = v.T
        low = lax.broadcasted_iota(jnp.int32, v.shape, 1) < HEAD_DIM
        v0 = jnp.where(low, v, 1.0).astype(BF16)
        v1 = jnp.where(low, 1.0, v).astype(BF16)
        if j == 0:
            kvc_ref[:, :KV_HALF] = k
            kvc_ref[:, KV_HALF:] = v
        elif j == 1:
            ksb_ref[:, :KV_HALF] = k.astype(BF16)
            ksb_ref[:, KV_HALF:] = jnp.where(blk == pos // L_SEL, 1.0, 0.0).astype(BF16)
            vs0_ref[...] = v0
            vs1_ref[...] = v1
        else:
            kwb_ref[...] = k.astype(BF16)
            vw0_ref[...] = v0
            vw1_ref[...] = v1
    gates_ref[...] = jax.nn.sigmoid(_dot(xn, wg_ref[...]))
    u = _dot(xn, wu_ref[...])
    u_ref[...] = u
    return u


def _inproj_prompt_kernel(x_ref, xh_ref, cos_ref, sin_ref, g_ref, wq_ref, wkv_ref, wg_ref, wu_ref, *outs,
                          tiles_per_seq):
    d_ref = outs[-1]
    i = pl.program_id(0)
    tm = x_ref.shape[0]
    g = g_ref[...]
    seq_tile = i % tiles_per_seq
    pos = seq_tile * tm + lax.broadcasted_iota(jnp.int32, (tm, 1), 0)
    u = _project(x_ref[...], g, cos_ref[...], sin_ref[...], pos, wq_ref, wkv_ref, wg_ref, wu_ref, outs[:-1])
    uh = _dot(_rms(xh_ref[...], g).astype(BF16), wu_ref[...])
    uh = jnp.where(seq_tile == 0, 0.0, uh)
    ext = jnp.concatenate([uh, u], axis=0)
    for gi, w in enumerate(POOL_WINDOWS):
        sl = slice(gi * POOL_GROUP_DIM, (gi + 1) * POOL_GROUP_DIM)
        s = ext[:, sl]
        k = 1
        while k < w:
            s = s + pltpu.roll(s, k, 0)
            k *= 2
        cnt = jnp.minimum(w, pos + 1).astype(F32)
        d_ref[:, sl] = (s[POOL_HALO:, :] / cnt - u[:, sl]).astype(BF16)


def _inproj_sample_kernel(x_ref, sp_ref, cos_ref, sin_ref, g_ref, wq_ref, wkv_ref, wg_ref, wu_ref, *outs,
                          past_len):
    d_ref = outs[-1]
    pos = jnp.full((x_ref.shape[0], 1), past_len, jnp.int32)
    u = _project(x_ref[...], g_ref[...], cos_ref[...], sin_ref[...], pos, wq_ref, wkv_ref, wg_ref, wu_ref,
                 outs[:-1])
    for gi, w in enumerate(POOL_WINDOWS):
        sl = slice(gi * POOL_GROUP_DIM, (gi + 1) * POOL_GROUP_DIM)
        s = u[:, sl]
        for k in range(1, w):
            s = s + sp_ref[POOL_BUF - k][:, sl]
        cnt = float(min(w, past_len + 1))
        d_ref[:, sl] = (s / cnt - u[:, sl]).astype(BF16)


def _inproj_outs(n_seq, rows, tm, n_sel, row_map, seq_tile):
    t = n_seq * rows
    shapes, specs = [], []

    def add(shape, dtype, block, index_map):
        shapes.append(jax.ShapeDtypeStruct(shape, dtype))
        specs.append(pl.BlockSpec(block, index_map))

    t_map4 = lambda i: (seq_tile(i)[0], 0, 0, seq_tile(i)[1])
    add((t, QPAD_DIM), BF16, (tm, QPAD_DIM), row_map)
    add((t, KV_DIM), F32, (tm, KV_DIM), row_map)
    for _ in range(3):
        add((n_seq, 2, KV_HALF, rows), F32, (1, 2, KV_HALF, tm), t_map4)
    add((t, KV_HALF + n_sel), BF16, (tm, KV_HALF + n_sel), row_map)
    for _ in range(5):
        add((t, KV_HALF), BF16, (tm, KV_HALF), row_map)
    add((t, LANES), F32, (tm, LANES), row_map)
    add((t, POOL_DIM), F32, (tm, POOL_DIM), row_map)
    add((t, POOL_DIM), BF16, (tm, POOL_DIM), row_map)
    return tuple(shapes), tuple(specs)


def _weight_specs(shapes):
    return [pl.BlockSpec(s, lambda i, _n=len(s): (0,) * _n) for s in shapes]


def _inproj_prompt(x, cos, sin, g_mix, wq, wkv, wg, wu, batch, seq, tm):
    tiles_per_seq = seq // tm
    halo_blocks = tm // POOL_HALO
    in_specs = [
        pl.BlockSpec((tm, D_MODEL), lambda i: (i, 0)),
        pl.BlockSpec((POOL_HALO, D_MODEL), lambda i: (jnp.maximum(i * halo_blocks - 1, 0), 0)),
        pl.BlockSpec((tm, LANES), lambda i: (i % tiles_per_seq, 0)),
        pl.BlockSpec((tm, LANES), lambda i: (i % tiles_per_seq, 0)),
    ] + _weight_specs([g_mix.shape, wq.shape, wkv.shape, wg.shape, wu.shape])
    out_shape, out_specs = _inproj_outs(batch, seq, tm, seq // L_SEL, lambda i: (i, 0),
                                        lambda i: (i // tiles_per_seq, i % tiles_per_seq))
    return pl.pallas_call(
        functools.partial(_inproj_prompt_kernel, tiles_per_seq=tiles_per_seq),
        grid=(batch * tiles_per_seq,), in_specs=in_specs, out_specs=out_specs,
        out_shape=out_shape, compiler_params=_params(1), name="inproj_prompt",
    )(x, x, cos, sin, g_mix, wq, wkv, wg, wu)


def _inproj_sample(x, sp_t, cos, sin, g_mix, wq, wkv, wg, wu, past_len):
    t = x.shape[0]
    in_specs = [
        pl.BlockSpec((t, D_MODEL), lambda i: (0, 0)),
        pl.BlockSpec(sp_t.shape, lambda i: (0, 0, 0)),
        pl.BlockSpec((t, LANES), lambda i: (0, 0)),
        pl.BlockSpec((t, LANES), lambda i: (0, 0)),
    ] + _weight_specs([g_mix.shape, wq.shape, wkv.shape, wg.shape, wu.shape])
    out_shape, out_specs = _inproj_outs(1, t, t, LANES, lambda i: (0, 0), lambda i: (0, 0))
    return pl.pallas_call(
        functools.partial(_inproj_sample_kernel, past_len=past_len),
        grid=(1,), in_specs=in_specs, out_specs=out_specs,
        out_shape=out_shape, compiler_params=_params(1), name="inproj_sample",
    )(x, sp_t, cos, sin, g_mix, wq, wkv, wg, wu)


def _compress_rows(x_refs, pe_ref, w1_ref, w2_ref, o_ref):
    nb = o_ref.shape[0]
    for c, x_ref in enumerate(x_refs):
        acc = jnp.zeros((nb, w1_ref.shape[3]), F32)
        for l in range(L_CMP):
            xl = x_ref[pl.ds(l, nb, stride=L_CMP), :] + pe_ref[c, l]
            acc = acc + _dot(xl.astype(BF16), w1_ref[c, l])
        hid = acc * jax.nn.sigmoid(acc)
        o_ref[:, c * KV_HALF:(c + 1) * KV_HALF] = _dot(hid.astype(BF16), w2_ref[c]).astype(o_ref.dtype)


def _compress_kernel(xk_ref, xv_ref, pe_ref, w1_ref, w2_ref, o_ref):
    _compress_rows((xk_ref, xv_ref), pe_ref, w1_ref, w2_ref, o_ref)


def _compress(rows, pe, w1, w2, nb_tile):
    n_blocks = rows.shape[0] // L_CMP
    nb_tile = int(np.gcd(nb_tile, n_blocks))
    return pl.pallas_call(
        _compress_kernel,
        grid=(n_blocks // nb_tile,),
        in_specs=[pl.BlockSpec((nb_tile * L_CMP, KV_HALF), lambda i: (i, 0)),
                  pl.BlockSpec((nb_tile * L_CMP, KV_HALF), lambda i: (i, 1))]
        + _weight_specs([pe.shape, w1.shape, w2.shape]),
        out_specs=pl.BlockSpec((nb_tile, KV_DIM), lambda i: (i, 0)),
        out_shape=jax.ShapeDtypeStruct((n_blocks, KV_DIM), BF16),
        compiler_params=_params(1), name="compress",
    )(rows, rows, pe, w1, w2)


def _compress_pages_kernel(pt_ref, cache_ref, pet_ref, perm_ref, w1_ref, w2_ref, o_ref, buf_ref, sem_ref, xs_ref):
    b = pl.program_id(0)
    n_b = pl.num_programs(0)
    n_pages = buf_ref.shape[1]
    n_pairs = n_pages // 2
    nb = o_ref.shape[0]
    perm = perm_ref[...]

    def page_copy(bb, slot, j):
        return pltpu.make_async_copy(cache_ref.at[pt_ref[bb * n_pages + j]], buf_ref.at[slot, j], sem_ref.at[slot])

    @pl.when(b == 0)
    def _():
        for j in range(n_pages):
            page_copy(0, 0, j).start()

    @pl.when(b + 1 < n_b)
    def _():
        for j in range(n_pages):
            page_copy(b + 1, (b + 1) % 2, j).start()

    slot = b % 2
    for j in range(n_pages):
        page_copy(b, slot, j).wait()

    unroll = int(np.gcd(n_pairs, PAIR_UNROLL))

    def body(it, carry):
        for u in range(unroll):
            pp = it * unroll + u
            for c in range(2):
                xt = jnp.concatenate([buf_ref[slot, 2 * pp, c], buf_ref[slot, 2 * pp + 1, c]], axis=1) + pet_ref[c]
                xs_ref[c, pp] = _dot_nt(perm, xt.astype(BF16))
        return carry

    lax.fori_loop(0, n_pairs // unroll, body, 0)
    rows_per_l = 2 * LANES // L_CMP
    for c in range(2):
        acc = jnp.zeros((nb, w1_ref.shape[3]), F32)
        for l in range(L_CMP):
            xl = xs_ref[c, :, l * rows_per_l:(l + 1) * rows_per_l, :].reshape(nb, KV_HALF)
            acc = acc + _dot(xl.astype(BF16), w1_ref[c, l])
        hid = acc * jax.nn.sigmoid(acc)
        o_ref[:, c * KV_HALF:(c + 1) * KV_HALF] = _dot(hid.astype(BF16), w2_ref[c]).astype(o_ref.dtype)


def _compress_pages(cache_t, page_table, pe, w1, w2):
    _, _, _, page = cache_t.shape
    n_seq, n_pages = page_table.shape
    assert page == LANES and n_pages % 2 == 0
    per_page = page // L_CMP
    pet = jnp.tile(pe[:, :, 0, :].transpose(0, 2, 1), (1, 1, 2 * per_page))
    src = np.arange(2 * page)
    dst = (src % L_CMP) * (2 * per_page) + src // L_CMP
    perm = np.zeros((2 * page, 2 * page), np.float32)
    perm[dst, src] = 1.0
    perm = jnp.asarray(perm, BF16)
    fixed = lambda shape: pl.BlockSpec(shape, lambda b, pt, _n=len(shape): (0,) * _n)
    grid_spec = pltpu.PrefetchScalarGridSpec(
        num_scalar_prefetch=1, grid=(n_seq,),
        in_specs=[pl.BlockSpec(memory_space=pl.ANY)] + [fixed(a.shape) for a in (pet, perm, w1, w2)],
        out_specs=pl.BlockSpec((n_pages * per_page, KV_DIM), lambda b, pt: (b, 0)),
        scratch_shapes=[pltpu.VMEM((2, n_pages) + cache_t.shape[1:], F32), pltpu.SemaphoreType.DMA((2,)),
                        pltpu.VMEM((2, n_pages // 2, 2 * page, KV_HALF), F32)],
    )
    return pl.pallas_call(
        _compress_pages_kernel, grid_spec=grid_spec,
        out_shape=jax.ShapeDtypeStruct((n_seq * n_pages * per_page, KV_DIM), BF16),
        compiler_params=_params(1), name="compress_pages",
    )(page_table.reshape(-1), cache_t, pet, perm, w1, w2)


def _stack_heads(q, g):
    return jnp.concatenate([q[:, (GROUP * g + r) * LANES:(GROUP * g + r + 1) * LANES] for r in range(GROUP)], axis=0)


def _masked_softmax(s, mask):
    s = jnp.where(mask, s, NEG)
    e = jnp.exp2(s - jnp.max(s, axis=-1, keepdims=True))
    return e / jnp.sum(e, axis=-1, keepdims=True) * mask.astype(F32)


def _assemble_heads(o_heads):
    chunks = []
    for c in range(N_HEADS // 2):
        g = (2 * c) // GROUP
        a, b = o_heads[2 * c], o_heads[2 * c + 1]
        if g == 1:
            a = pltpu.roll(a, HEAD_DIM, 1)
        else:
            b = pltpu.roll(b, HEAD_DIM, 1)
        lane = lax.broadcasted_iota(jnp.int32, a.shape, 1)
        chunks.append(jnp.where(lane < HEAD_DIM, a, b))
    return jnp.concatenate(chunks, axis=1)


def _select_blocks(score, n_pick):
    n_blk = score.shape[1]
    lane = lax.broadcasted_iota(jnp.int32, score.shape, 1)
    work = score
    picks = []
    for _ in range(n_pick):
        m = jnp.max(work, axis=-1, keepdims=True)
        idx = jnp.min(jnp.where(work == m, lane, n_blk), axis=-1, keepdims=True)
        picks.append(idx)
        work = jnp.where(lane == idx, -jnp.inf, work)
    return picks


def _select_blocks_t(score_t, n_pick):
    n_blk = score_t.shape[0]
    blk = lax.broadcasted_iota(jnp.int32, score_t.shape, 0)
    work = score_t
    sel = jnp.zeros(score_t.shape, F32)
    for _ in range(n_pick):
        m = jnp.max(work, axis=0, keepdims=True)
        idx = jnp.min(jnp.where(work == m, blk, n_blk), axis=0, keepdims=True)
        hit = blk == idx
        sel = jnp.where(hit, 1.0, sel)
        work = jnp.where(hit, -jnp.inf, work)
    return sel


def _cmp_attention(qg, kc, vc, pos_rows, n_sel):
    s = _dot_nt(qg, kc)
    j = lax.broadcasted_iota(jnp.int32, s.shape, 1)
    n = 2 * (j % n_sel) + j // n_sel
    mask = (n * L_CMP + (L_CMP - 1)) <= pos_rows
    p = _masked_softmax(s, mask)
    return _dot(p.astype(BF16), vc), p


def _cmp_topk_kernel(q_ref, c_ref, gates_ref, oc_ref, msel_ref, *, n_sel, n_top):
    i = pl.program_id(1)
    tq = q_ref.shape[0]
    q = q_ref[...]
    comp = c_ref[0]
    kc, vc = comp[:, :LANES], comp[:, LANES:]
    gates = gates_ref[...]
    pos = i * tq + lax.broadcasted_iota(jnp.int32, (tq, 1), 0)
    pos_rows = jnp.concatenate([pos] * GROUP, axis=0)
    blk = lax.broadcasted_iota(jnp.int32, (n_sel, tq), 0)
    cur = (i * tq + lax.broadcasted_iota(jnp.int32, (1, tq), 1)) // L_SEL
    forced = (blk == 0) | (blk == cur) | (blk == cur - 1)
    o_heads = []
    for g in range(N_KV_HEADS):
        o, p = _cmp_attention(_stack_heads(q, g), kc, vc, pos_rows, n_sel)
        imp = p[0:tq]
        for r in range(1, GROUP):
            imp = imp + p[r * tq:(r + 1) * tq]
        imp = (imp[:, :n_sel] + imp[:, n_sel:]).T
        score = jnp.where(forced, FORCED, jnp.where(blk < cur, imp, NEG))
        sel = jnp.where(score > NEG / 2, _select_blocks_t(score, n_top), 0.0)
        msel_ref[:, g * n_sel:(g + 1) * n_sel] = ((1.0 - sel) * NEG).T.astype(BF16)
        for r in range(GROUP):
            h = GROUP * g + r
            o_heads.append(o[r * tq:(r + 1) * tq] * gates[:, h:h + 1])
    oc_ref[...] = _assemble_heads(o_heads)


def _cmp_topk_prompt(q, comp, gates, batch, seq, tq):
    n_sel = seq // L_SEL
    n_c = comp.shape[1]
    tiles = seq // tq
    row = lambda b, i: (b * tiles + i, 0)
    return pl.pallas_call(
        functools.partial(_cmp_topk_kernel, n_sel=n_sel, n_top=min(TOP_N, n_sel)),
        grid=(batch, tiles),
        in_specs=[pl.BlockSpec((tq, QPAD_DIM), row),
                  pl.BlockSpec((1, n_c, KV_DIM), lambda b, i: (b, 0, 0)),
                  pl.BlockSpec((tq, LANES), row)],
        out_specs=(pl.BlockSpec((tq, NSA_DIM), row), pl.BlockSpec((tq, N_KV_HEADS * n_sel), row)),
        out_shape=(jax.ShapeDtypeStruct((batch * seq, NSA_DIM), F32),
                   jax.ShapeDtypeStruct((batch * seq, N_KV_HEADS * n_sel), BF16)),
        compiler_params=_params(2), name="cmp_topk_prompt",
    )(q, comp, gates)


def _flash_update(carry, q, k, v, bias):
    m, acc = carry
    s = _dot_nt(q, k)
    if bias is not None:
        s = s + bias
    m_new = jnp.maximum(m, jnp.max(s, axis=-1, keepdims=True))
    p = jnp.exp2(s - m_new).astype(BF16)
    acc = jnp.exp2(m - m_new) * acc + _dot(p, v)
    return m_new, acc


def _sel_win_kernel(q_ref, ks_ref, vs0_ref, vs1_ref, kw_ref, vw0_ref, vw1_ref, msel_ref, gates_ref, oc_ref, o_ref,
                    *, n_sel):
    i = pl.program_id(1)
    tq = q_ref.shape[0]
    hpc = HEADS_PER_CHAIN
    n_chain = N_HEADS // hpc
    rows = hpc * tq
    q = q_ref[...]
    gates = gates_ref[...]
    vs_refs, vw_refs = (vs0_ref, vs1_ref), (vw0_ref, vw1_ref)
    pos = i * tq + lax.broadcasted_iota(jnp.int32, (tq, 1), 0)
    key = lax.broadcasted_iota(jnp.int32, (1, tq), 1)
    d0 = pl.multiple_of(i * tq, tq)
    p0 = pl.multiple_of(jnp.maximum(i - 1, 0) * tq, tq)
    bias_d = jnp.concatenate([jnp.where(i * tq + key <= pos, 0.0, NEG)] * hpc, axis=0)
    in_win = (pos - ((i - 1) * tq + key) < WINDOW) & (i > 0)
    bias_p = jnp.concatenate([jnp.where(in_win, 0.0, NEG)] * hpc, axis=0)
    groups = [(c * hpc) // GROUP for c in range(n_chain)]
    qcs, qas = [], []
    for c in range(n_chain):
        g = groups[c]
        qc = jnp.concatenate([q[:, h * LANES:(h + 1) * LANES] for h in range(c * hpc, (c + 1) * hpc)], axis=0)
        msel = jnp.concatenate([msel_ref[:, g * n_sel:(g + 1) * n_sel]] * hpc, axis=0)
        qcs.append(qc)
        qas.append(jnp.concatenate([qc, msel], axis=1))
    init = (jnp.full((rows, 1), NEG, F32), jnp.zeros((rows, LANES), F32))

    def tiles(k_ref, v_refs, start):
        return k_ref[pl.ds(start, tq), :], [v_ref[pl.ds(start, tq), :] for v_ref in v_refs]

    kd, vd = tiles(ks_ref, vs_refs, d0)
    sel = tuple(_flash_update(init, qas[c], kd, vd[groups[c]], bias_d) for c in range(n_chain))

    def body(kt, carries):
        k, v = tiles(ks_ref, vs_refs, pl.multiple_of(kt * tq, tq))
        return tuple(_flash_update(carries[c], qas[c], k, v[groups[c]], None) for c in range(n_chain))

    sel = lax.fori_loop(0, i, body, sel)
    kd, vd = tiles(kw_ref, vw_refs, d0)
    kp, vp = tiles(kw_ref, vw_refs, p0)
    o_heads = []
    for c in range(n_chain):
        g = groups[c]
        win = _flash_update(_flash_update(init, qcs[c], kd, vd[g], bias_d), qcs[c], kp, vp[g], bias_p)
        den = (1 - g) * HEAD_DIM
        o_s = sel[c][1] / sel[c][1][:, den:den + 1]
        o_w = win[1] / win[1][:, den:den + 1]
        for r in range(hpc):
            h = c * hpc + r
            rs = slice(r * tq, (r + 1) * tq)
            o_heads.append(o_s[rs] * gates[:, N_HEADS + h:N_HEADS + h + 1]
                           + o_w[rs] * gates[:, 2 * N_HEADS + h:2 * N_HEADS + h + 1])
    o_ref[...] = (_assemble_heads(o_heads) + oc_ref[...]).astype(o_ref.dtype)


def _sel_win_prompt(q, ksb, vs0, vs1, kwb, vw0, vw1, msel, gates, oc, batch, seq, tq):
    assert tq == WINDOW
    n_sel = seq // L_SEL
    tiles = seq // tq
    row = lambda b, i: (b * tiles + i, 0)
    whole = lambda b, i: (b, 0)
    return pl.pallas_call(
        functools.partial(_sel_win_kernel, n_sel=n_sel),
        grid=(batch, tiles),
        in_specs=[pl.BlockSpec((tq, QPAD_DIM), row),
                  pl.BlockSpec((seq, KV_HALF + n_sel), whole, pipeline_mode=pl.Buffered(1))]
        + [pl.BlockSpec((seq, KV_HALF), whole, pipeline_mode=pl.Buffered(1))] * 5
        + [pl.BlockSpec((tq, N_KV_HEADS * n_sel), row),
           pl.BlockSpec((tq, LANES), row),
           pl.BlockSpec((tq, NSA_DIM), row)],
        out_specs=pl.BlockSpec((tq, NSA_DIM), row),
        out_shape=jax.ShapeDtypeStruct((batch * seq, NSA_DIM), BF16),
        compiler_params=_params(2), name="sel_win_prompt",
    )(q, ksb, vs0, vs1, kwb, vw0, vw1, msel, gates, oc)


def _sample_cmp_kernel(q_ref, c_ref, gates_ref, oc_ref, idx_ref, *, n_blk, n_pick, past_len):
    n_sb = q_ref.shape[0]
    pos_rows = jnp.full((N_HEADS, 1), past_len, jnp.int32)
    imps = []
    for s in range(n_sb):
        comp = c_ref[s]
        o, p = _cmp_attention(q_ref[s], comp[:, :LANES], comp[:, LANES:], pos_rows, n_blk)
        oc_ref[s] = o * gates_ref[s][:, 0:1]
        imps += [jnp.sum(p[GROUP * g:GROUP * (g + 1)], axis=0, keepdims=True) for g in range(N_KV_HEADS)]
    imp = jnp.concatenate(imps, axis=0)
    imp = imp[:, :n_blk] + imp[:, n_blk:]
    cur = past_len // L_SEL
    blk = lax.broadcasted_iota(jnp.int32, imp.shape, 1)
    forced = (blk == 0) | (blk == cur) | (blk == cur - 1)
    score = jnp.where(forced, FORCED, jnp.where(blk < cur, imp, NEG))
    lane = lax.broadcasted_iota(jnp.int32, idx_ref.shape, 1)
    out = jnp.zeros(idx_ref.shape, jnp.int32)
    for k, idx in enumerate(_select_blocks(score, n_pick)):
        out = jnp.where(lane == k, idx, out)
    idx_ref[...] = out


def _sample_cmp(q3, comp, gates3, n_pick, past_len, seqs_per_step):
    n_seq = q3.shape[0]
    n_c = comp.shape[1]
    n_sb = int(np.gcd(seqs_per_step, n_seq))
    blk3 = lambda b: (b, 0, 0)
    return pl.pallas_call(
        functools.partial(_sample_cmp_kernel, n_blk=n_c // 2, n_pick=n_pick, past_len=past_len),
        grid=(n_seq // n_sb,),
        in_specs=[pl.BlockSpec((n_sb, N_HEADS, LANES), blk3),
                  pl.BlockSpec((n_sb, n_c, KV_DIM), blk3),
                  pl.BlockSpec((n_sb, N_HEADS, LANES), blk3)],
        out_specs=(pl.BlockSpec((n_sb, N_HEADS, LANES), blk3),
                   pl.BlockSpec((n_sb * N_KV_HEADS, LANES), lambda b: (b, 0))),
        out_shape=(jax.ShapeDtypeStruct((n_seq, N_HEADS, LANES), F32),
                   jax.ShapeDtypeStruct((n_seq * N_KV_HEADS, LANES), jnp.int32)),
        compiler_params=_params(1), name="sample_cmp",
    )(q3, comp, gates3)


def _softmax_with_new(s, valid, s_new):
    s = jnp.where(valid, s, NEG)
    m = jnp.maximum(jnp.max(s, axis=-1, keepdims=True), s_new)
    e = jnp.exp2(s - m) * valid.astype(F32)
    e_new = jnp.exp2(s_new - m)
    den = jnp.sum(e, axis=-1, keepdims=True) + e_new
    return e / den, e_new / den


def _sample_sel_win_kernel(pg_ref, hf_ref, q_ref, win_ref, kvs_ref, kvw_ref, gates_ref, oc_ref, cache_ref,
                           o_ref, buf_ref, sem_ref, *, n_pick):
    b = pl.program_id(0)
    n_b = pl.num_programs(0)
    n_dma = N_KV_HEADS * n_pick
    page = buf_ref.shape[-1]

    def page_copy(bb, slot, j):
        return pltpu.make_async_copy(cache_ref.at[pg_ref[bb * n_dma + j]], buf_ref.at[slot, j], sem_ref.at[slot])

    @pl.when(b == 0)
    def _():
        for j in range(n_dma):
            page_copy(0, 0, j).start()

    @pl.when(b + 1 < n_b)
    def _():
        for j in range(n_dma):
            page_copy(b + 1, (b + 1) % 2, j).start()

    q = q_ref[0]
    qf = q.astype(F32)
    gates = gates_ref[0]
    row = lax.broadcasted_iota(jnp.int32, (N_HEADS, LANES), 0)

    def new_key(kv_ref):
        kv = kv_ref[0].astype(BF16).astype(F32)
        return jnp.sum(qf * kv[:, :KV_HALF], axis=-1, keepdims=True), kv[:, KV_HALF:]

    win = win_ref[0]
    kt, vt = win[0].astype(BF16), win[1].astype(BF16)
    s_new, v_new = new_key(kvw_ref)
    lane_w = lax.broadcasted_iota(jnp.int32, (N_HEADS, kt.shape[1]), 1)
    p, p_new = _softmax_with_new(_dot(q, kt), lane_w >= 1, s_new)
    o_w = _dot_nt(p.astype(BF16), vt) + p_new.astype(BF16).astype(F32) * v_new

    slot = b % 2
    for j in range(n_dma):
        page_copy(b, slot, j).wait()
    s_new, v_new = new_key(kvs_ref)
    lane_p = lax.broadcasted_iota(jnp.int32, (N_HEADS, page), 1)
    o_s = jnp.zeros((N_HEADS, LANES), F32)
    for g in range(N_KV_HEADS):
        kts, vts, valids = [], [], []
        for k in range(n_pick):
            j = g * n_pick + k
            pg = buf_ref[slot, j]
            kts.append(pg[0].astype(BF16))
            vts.append(pg[1].astype(BF16))
            half = hf_ref[b * n_dma + j]
            valids.append((lane_p // L_SEL) == half)
        kt_all = jnp.concatenate(kts, axis=1)
        vt_all = jnp.concatenate(vts, axis=1)
        p, p_new = _softmax_with_new(_dot(q, kt_all), jnp.concatenate(valids, axis=1), s_new)
        o_g = _dot_nt(p.astype(BF16), vt_all) + p_new.astype(BF16).astype(F32) * v_new
        o_s = jnp.where(row // GROUP == g, o_g, o_s)
    o_ref[0] = oc_ref[0] + o_s * gates[:, 1:2] + o_w * gates[:, 2:3]


def _sample_sel_win(pages, halves, q3, win_t, kvs, kvw, gates3, oc, cache_t, n_pick):
    n_seq = q3.shape[0]
    n_dma = N_KV_HEADS * n_pick
    blk3 = lambda b, pg, hf: (b, 0, 0)
    grid_spec = pltpu.PrefetchScalarGridSpec(
        num_scalar_prefetch=2, grid=(n_seq,),
        in_specs=[pl.BlockSpec((1, N_HEADS, LANES), blk3),
                  pl.BlockSpec((1,) + win_t.shape[1:], lambda b, pg, hf: (b, 0, 0, 0)),
                  pl.BlockSpec((1, 1, KV_DIM), blk3),
                  pl.BlockSpec((1, 1, KV_DIM), blk3),
                  pl.BlockSpec((1, N_HEADS, LANES), blk3),
                  pl.BlockSpec((1, N_HEADS, LANES), blk3),
                  pl.BlockSpec(memory_space=pl.ANY)],
        out_specs=pl.BlockSpec((1, N_HEADS, LANES), blk3),
        scratch_shapes=[pltpu.VMEM((2, n_dma) + cache_t.shape[1:], F32), pltpu.SemaphoreType.DMA((2,))],
    )
    return pl.pallas_call(
        functools.partial(_sample_sel_win_kernel, n_pick=n_pick),
        grid_spec=grid_spec,
        out_shape=jax.ShapeDtypeStruct((n_seq, N_HEADS, LANES), F32),
        compiler_params=_params(1), name="sample_sel_win",
    )(pages, halves, q3, win_t, kvs, kvw, gates3, oc, cache_t)


def _tail_kernel(x_ref, attn_ref, d_ref, xs_ref, attns_ref, ds_ref, gmix_ref, wgm_ref, wpool_ref, pscale_ref,
                 wbn_ref, wbp_ref, wout_ref, gffn_ref, wr_ref, br_ref, h_ref, xn2_ref, route_ref, cout_ref, run_ref,
                 *, n_experts):
    i = pl.program_id(0)
    is_sample = i == pl.num_programs(0) - 1
    x = jnp.where(is_sample, xs_ref[...], x_ref[...])
    attn = jnp.where(is_sample, attns_ref[...], attn_ref[...])
    d = jnp.where(is_sample, ds_ref[...], d_ref[...])
    tm = x.shape[0]
    xn = _rms(x, gmix_ref[...]).astype(BF16)
    gm = jax.nn.sigmoid(_dot(xn, wgm_ref[...]))
    pool = jnp.concatenate(
        [_dot(d[:, gi * POOL_GROUP_DIM:(gi + 1) * POOL_GROUP_DIM], wpool_ref[gi]) for gi in range(len(POOL_WINDOWS))],
        axis=1) * pscale_ref[...]
    merged = (gm[:, :D_MODEL] * _dot(attn, wbn_ref[...])
              + gm[:, D_MODEL:] * _dot(pool.astype(BF16), wbp_ref[...]))
    h = x + _dot(merged.astype(BF16), wout_ref[...])
    h_ref[...] = h
    xn2 = _rms(h, gffn_ref[...]).astype(BF16)
    xn2_ref[...] = xn2
    logits = _dot(xn2, wr_ref[...]) + br_ref[...]
    lane = lax.broadcasted_iota(jnp.int32, (tm, LANES), 1)
    work = jnp.where(lane < n_experts, logits, -jnp.inf)
    vals, onehots = [], []
    for _ in range(TOP_K):
        m = jnp.max(work, axis=-1, keepdims=True)
        idx = jnp.min(jnp.where(work == m, lane, LANES), axis=-1, keepdims=True)
        vals.append(m)
        onehots.append(lane == idx)
        work = jnp.where(lane == idx, -jnp.inf, work)
    es = [jnp.exp(v - vals[0]) for v in vals]
    tot = es[0] + es[1] + es[2] + es[3]

    @pl.when(i == 0)
    def _():
        run_ref[...] = jnp.zeros(run_ref.shape, F32)

    chosen = jnp.zeros((tm, LANES), F32)
    for k in range(TOP_K):
        chosen = jnp.where(onehots[k], 1.0, chosen)
    earlier = lax.broadcasted_iota(jnp.int32, (tm, tm), 0) > lax.broadcasted_iota(jnp.int32, (tm, tm), 1)
    before = run_ref[...] + _dot(jnp.where(earlier, 1.0, 0.0).astype(BF16), chosen.astype(BF16))
    route = jnp.zeros((tm, LANES), F32)
    lane_f = lane.astype(F32)
    for k in range(TOP_K):
        e_k = jnp.sum(jnp.where(onehots[k], lane_f, 0.0), axis=-1, keepdims=True)
        r_k = jnp.sum(jnp.where(onehots[k], before, 0.0), axis=-1, keepdims=True)
        route = jnp.where(lane == k, e_k, route)
        route = jnp.where(lane == TOP_K + k, es[k] / tot, route)
        route = jnp.where(lane == 2 * TOP_K + k, r_k, route)
    route_ref[...] = route
    run_ref[...] = run_ref[...] + jnp.sum(chosen, axis=0, keepdims=True)
    cout_ref[...] = run_ref[...]


def _tail(x, attn, d, xs, attn_s, d_s, weights, n_experts, tm):
    n_p = x.shape[0] // tm
    t = (n_p + 1) * tm
    row = lambda i: (i, 0)
    prompt = lambda i: (jnp.minimum(i, n_p - 1), 0)
    fixed = lambda i: (0, 0)
    return pl.pallas_call(
        functools.partial(_tail_kernel, n_experts=n_experts),
        grid=(n_p + 1,),
        in_specs=[pl.BlockSpec((tm, D_MODEL), prompt), pl.BlockSpec((tm, NSA_DIM), prompt),
                  pl.BlockSpec((tm, POOL_DIM), prompt), pl.BlockSpec((tm, D_MODEL), fixed),
                  pl.BlockSpec((tm, NSA_DIM), fixed), pl.BlockSpec((tm, POOL_DIM), fixed)]
        + _weight_specs([w.shape for w in weights]),
        out_specs=(pl.BlockSpec((tm, D_MODEL), row), pl.BlockSpec((tm, D_MODEL), row),
                   pl.BlockSpec((tm, LANES), row), pl.BlockSpec((1, LANES), fixed)),
        out_shape=(jax.ShapeDtypeStruct((t, D_MODEL), F32), jax.ShapeDtypeStruct((t, D_MODEL), BF16),
                   jax.ShapeDtypeStruct((t, LANES), F32), jax.ShapeDtypeStruct((1, LANES), F32)),
        scratch_shapes=[pltpu.VMEM((1, LANES), F32)],
        compiler_params=_params(1), name="tail",
    )(x, attn, d, xs, attn_s, d_s, *weights)


def _moe_kernel(te_ref, tv_ref, tf_ref, x_ref, wg_ref, bg_ref, wu_ref, bu_ref, wd_ref, bd_ref, y_ref,
                wgb_ref, wub_ref, wdb_ref):
    t = pl.program_id(0)

    @pl.when(tf_ref[t] > 0)
    def _():
        wgb_ref[...] = wg_ref[0].astype(BF16)
        wub_ref[...] = wu_ref[0].astype(BF16)
        wdb_ref[...] = wd_ref[0].astype(BF16)

    @pl.when(tv_ref[t] > 0)
    def _():
        x = x_ref[...]
        gate = jnp.minimum(_dot(x, wgb_ref[...]) + bg_ref[0], SWIGLU_LIMIT)
        up = jnp.clip(_dot(x, wub_ref[...]) + bu_ref[0], -SWIGLU_LIMIT, SWIGLU_LIMIT)
        hid = (up + 1.0) * gate * jax.nn.sigmoid(SWIGLU_ALPHA * gate)
        y_ref[...] = (_dot(hid.astype(BF16), wdb_ref[...]) + bd_ref[0]).astype(y_ref.dtype)

    @pl.when(tv_ref[t] == 0)
    def _():
        y_ref[...] = jnp.zeros(y_ref.shape, y_ref.dtype)


def _moe(tile_e, tile_v, tile_f, xs, wg, bg, wu, bu, wd, bd, tme):
    p_pad = xs.shape[0]
    d_e = wg.shape[2]
    wspec = lambda shape: pl.BlockSpec((1,) + shape, lambda t, te, tv, tf: (te[t], 0, 0))
    grid_spec = pltpu.PrefetchScalarGridSpec(
        num_scalar_prefetch=3, grid=(p_pad // tme,),
        in_specs=[pl.BlockSpec((tme, D_MODEL), lambda t, te, tv, tf: (t, 0)),
                  wspec((D_MODEL, d_e)), wspec((1, d_e)), wspec((D_MODEL, d_e)), wspec((1, d_e)),
                  wspec((d_e, D_MODEL)), wspec((1, D_MODEL))],
        out_specs=pl.BlockSpec((tme, D_MODEL), lambda t, te, tv, tf: (t, 0)),
        scratch_shapes=[pltpu.VMEM((D_MODEL, d_e), BF16), pltpu.VMEM((D_MODEL, d_e), BF16),
                        pltpu.VMEM((d_e, D_MODEL), BF16)],
    )
    return pl.pallas_call(
        _moe_kernel, grid_spec=grid_spec,
        out_shape=jax.ShapeDtypeStruct((p_pad, D_MODEL), BF16),
        compiler_params=_params(1), name="moe_experts",
    )(tile_e, tile_v, tile_f, xs, wg, bg, wu, bu, wd, bd)


def _final_kernel(h_ref, y_ref, route_ref, g_ref, o_ref):
    route = route_ref[...]
    acc = h_ref[...]
    for k in range(TOP_K):
        acc = acc + y_ref[k].astype(F32) * route[:, TOP_K + k:TOP_K + k + 1]
    o_ref[...] = _rms(acc, g_ref[...])


def _final(h, yg, route, g_final, tm, first, n_tiles):
    src = lambda i: (i + first, 0)
    return pl.pallas_call(
        _final_kernel, grid=(n_tiles,),
        in_specs=[pl.BlockSpec((tm, D_MODEL), src), pl.BlockSpec((TOP_K, tm, D_MODEL), lambda i: (0, i + first, 0)),
                  pl.BlockSpec((tm, LANES), src), pl.BlockSpec((1, D_MODEL), lambda i: (0, 0))],
        out_specs=pl.BlockSpec((tm, D_MODEL), lambda i: (i, 0)),
        out_shape=jax.ShapeDtypeStruct((n_tiles * tm, D_MODEL), F32),
        compiler_params=_params(1), name="final_norm",
    )(h, yg, route, g_final)


def _rope_tables(pos):
    half = HEAD_DIM // 2
    inv = ROPE_THETA ** (-jnp.arange(half, dtype=F32) / half)
    ang = pos.astype(F32)[:, None] * inv[None, :]
    cos, sin = jnp.cos(ang), jnp.sin(ang)
    cos = jnp.concatenate([cos, cos] * (LANES // HEAD_DIM), axis=1)
    sin = jnp.concatenate([-sin, sin] * (LANES // HEAD_DIM), axis=1)
    return cos, sin


def _split_w_in(w_in):
    o_q = NSA_DIM
    o_g = o_q + 3 * KV_DIM
    o_u = o_g + 3 * N_HEADS
    o_m = o_u + POOL_DIM
    wq = w_in[:, :o_q].reshape(D_MODEL, N_HEADS, HEAD_DIM)
    zeros = jnp.zeros_like(wq)
    lo = jnp.concatenate([wq, zeros], axis=2)
    hi = jnp.concatenate([zeros, wq], axis=2)
    in_hi = (jnp.arange(N_HEADS) // GROUP == 1)[None, :, None]
    wq_pad = jnp.where(in_hi, hi, lo).reshape(D_MODEL, QPAD_DIM)
    wkv = w_in[:, o_q:o_g]
    wg = w_in[:, o_g:o_u].reshape(D_MODEL, N_HEADS, 3).transpose(0, 2, 1).reshape(D_MODEL, 3 * N_HEADS)
    wg = jnp.pad(wg, ((0, 0), (0, LANES - 3 * N_HEADS)))
    wu = w_in[:, o_u:o_m]
    wgm = w_in[:, o_m:]
    return [w.astype(BF16) for w in (wq_pad, wkv, wg, wu, wgm)]


def _compress_weights(cmp_pe, cmp_w1, cmp_w2):
    eye = jnp.eye(N_KV_HEADS, dtype=F32)
    w1 = jnp.einsum('cldh,gf->clgdfh', cmp_w1, eye).reshape(2, L_CMP, KV_HALF, N_KV_HEADS * CMP_HIDDEN)
    w2 = jnp.einsum('chd,gf->cghfd', cmp_w2, eye).reshape(2, N_KV_HEADS * CMP_HIDDEN, KV_HALF)
    pe = jnp.broadcast_to(cmp_pe.transpose(1, 0, 2)[:, :, None, :], (2, L_CMP, N_KV_HEADS, HEAD_DIM))
    return pe.reshape(2, L_CMP, 1, KV_HALF), w1.astype(BF16), w2.astype(BF16)


def _even_odd(comp, n_seq):
    n_c = comp.shape[0] // n_seq
    return comp.reshape(n_seq, n_c // 2, 2, KV_DIM).transpose(0, 2, 1, 3).reshape(n_seq, n_c, KV_DIM)


def _fix_sample_heads(o3):
    n_seq = o3.shape[0]
    o4 = o3.reshape(n_seq, N_KV_HEADS, GROUP, N_KV_HEADS, HEAD_DIM)
    return jnp.concatenate([o4[:, g, :, g, :] for g in range(N_KV_HEADS)], axis=1).reshape(n_seq, NSA_DIM)


def _gates_by_head(gates):
    g = gates[:, :3 * N_HEADS].reshape(-1, 3, N_HEADS).transpose(0, 2, 1)
    return jnp.pad(g, ((0, 0), (0, 0), (0, LANES - 3)))


def _key_minor(x):
    n, p = x.shape[:2]
    return x.transpose(0, 2, 3, 4, 1).reshape(n, 2, KV_HALF, p)


def _from_key_minor(xt):
    n, _, _, p = xt.shape
    return xt.reshape(n, 2, N_KV_HEADS, HEAD_DIM, p).transpose(0, 4, 1, 2, 3)


def _expert_order(route, counts, n_experts, tme):
    t = route.shape[0]
    n_pairs = t * TOP_K
    e = route[:, :TOP_K].astype(jnp.int32)
    rank = route[:, 2 * TOP_K:3 * TOP_K].astype(jnp.int32)
    counts = counts[0, :n_experts].astype(jnp.int32)
    padded = ((counts + tme - 1) // tme) * tme
    ends = jnp.cumsum(padded)
    off = ends - padded
    pad_before = off - (jnp.cumsum(counts) - counts)
    experts = jnp.arange(n_experts, dtype=jnp.int32)
    pair_slot = rank + jnp.sum(jnp.where(e[:, :, None] == experts, off, 0), axis=-1)
    n_tiles = -(-n_pairs // tme) + n_experts
    p_pad = n_tiles * tme
    tok = jnp.broadcast_to(jnp.arange(t, dtype=jnp.int32)[:, None], (t, TOP_K))
    _, tok_sorted = lax.sort((pair_slot.reshape(n_pairs), tok.reshape(n_pairs)), num_keys=1)
    slot = jnp.arange(p_pad, dtype=jnp.int32)
    slot_e = jnp.sum((slot[:, None] >= ends[None, :]).astype(jnp.int32), axis=-1)
    slot_e = jnp.minimum(slot_e, n_experts - 1)
    slot_pad = jnp.sum(jnp.where(slot_e[:, None] == experts, pad_before, 0), axis=-1)
    src_tok = tok_sorted[jnp.clip(slot - slot_pad, 0, n_pairs - 1)]
    tile_start = jnp.arange(n_tiles, dtype=jnp.int32) * tme
    tile_e = jnp.minimum(jnp.sum((tile_start[:, None] >= ends[None, :]).astype(jnp.int32), axis=-1), n_experts - 1)
    tile_v = (tile_start < ends[-1]).astype(jnp.int32)
    tile_f = jnp.concatenate([jnp.ones((1,), jnp.int32), (tile_e[1:] != tile_e[:-1]).astype(jnp.int32)])
    return pair_slot, src_tok, tile_e, tile_v, tile_f


def kernel(x_prompt, x_sample, cache_kv_cmp, cache_kv_sel, state_kv_win, state_pool, page_table, g_mix, w_in, cmp_pe, cmp_w1, cmp_w2, w_pool, pool_scale, w_br_nsa, w_br_pool, w_out, g_ffn, w_router, b_router, w_gate, b_gate, w_up, b_up, w_down, b_down, g_final):
    batch, seq, _ = x_prompt.shape
    n_seq, dec_seq, _ = x_sample.shape
    depth = g_mix.shape[0]
    assert depth == 1 and dec_seq == 1
    page_size = cache_kv_cmp.shape[2]
    n_pages = page_table.shape[1]
    past_len = n_pages * page_size
    n_experts = w_router.shape[2]
    t_p = batch * seq
    tm = min(ROW_TILE, seq)

    wq, wkv, wg, wu, wgm = _split_w_in(w_in[0])
    gmix = g_mix[0][None, :]
    pe, w1, w2 = _compress_weights(cmp_pe[0], cmp_w1[0], cmp_w2[0])
    cos_p, sin_p = _rope_tables(jnp.arange(seq, dtype=jnp.int32))
    cos_s, sin_s = _rope_tables(jnp.full((n_seq,), past_len, jnp.int32))
    wr = jnp.pad(w_router[0], ((0, 0), (0, LANES - n_experts))).astype(BF16)
    br = jnp.pad(b_router[0], (0, LANES - n_experts))[None, :]
    tail_w = [gmix, wgm, w_pool[0].astype(BF16), pool_scale[0][None, :], w_br_nsa[0].astype(BF16),
              w_br_pool[0].astype(BF16), w_out[0].astype(BF16), g_ffn[0][None, :], wr, br]

    xp = x_prompt.reshape(t_p, D_MODEL)
    (q_p, kvc_p, kvct_p, kvst_p, kvwt_p, ksb_p, vs0_p, vs1_p, kwb_p, vw0_p, vw1_p, gates_p, u_p, d_p) = _inproj_prompt(
        xp, cos_p, sin_p, gmix, wq, wkv, wg, wu, batch, seq, tm)
    comp_p = _even_odd(_compress(kvc_p, pe, w1, w2, CMP_BLOCK_TILE), batch)
    tq = min(CMP_Q_TILE, seq)
    oc_p, msel_p = _cmp_topk_prompt(q_p, comp_p, gates_p, batch, seq, tq)
    attn_p = _sel_win_prompt(q_p, ksb_p, vs0_p, vs1_p, kwb_p, vw0_p, vw1_p, msel_p, gates_p, oc_p, batch, seq, WINDOW)

    xs = x_sample.reshape(n_seq, D_MODEL)
    sp_t = state_pool[0].transpose(1, 0, 2)
    (q_s, kvc_s, _, kvst_s, kvwt_s, _, _, _, _, _, _, gates_s, u_s, d_s) = _inproj_sample(
        xs, sp_t, cos_s, sin_s, gmix, wq, wkv, wg, wu, past_len)
    kvs_s = kvst_s.reshape(KV_DIM, n_seq).T
    kvw_s = kvwt_s.reshape(KV_DIM, n_seq).T
    comp_s = _even_odd(_compress_pages(_key_minor(cache_kv_cmp[0]), page_table, pe, w1, w2), n_seq)
    q3 = q_s.reshape(n_seq, N_HEADS, LANES)
    gates3 = _gates_by_head(gates_s)
    n_blk = past_len // L_SEL
    n_pick = min(TOP_N, n_blk + 1) - 1
    oc_s, idx_s = _sample_cmp(q3, comp_s, gates3, n_pick, past_len, SAMPLE_SEQS_PER_STEP)
    top_idx = idx_s.reshape(n_seq, N_KV_HEADS, LANES)[:, :, :n_pick]
    sel_per_page = page_size // L_SEL
    pages_b = jnp.broadcast_to(page_table[:, None, :], (n_seq, N_KV_HEADS, n_pages))
    phys = jnp.take_along_axis(pages_b, top_idx // sel_per_page, axis=2).reshape(-1)
    halves = (top_idx % sel_per_page).reshape(-1)
    w_buf = state_kv_win.shape[2]
    assert w_buf == WINDOW
    win_t = _key_minor(state_kv_win[0])
    o3 = _sample_sel_win(phys, halves, q3, win_t, kvs_s[:, None, :], kvw_s[:, None, :], gates3, oc_s,
                         _key_minor(cache_kv_sel[0]), n_pick)
    attn_s = _fix_sample_heads(o3).astype(BF16)

    assert n_seq <= tm
    pad_rows = lambda a: jnp.pad(a, ((0, tm - n_seq), (0, 0)))
    h_all, xn2_all, route, counts = _tail(xp, attn_p, d_p, pad_rows(xs), pad_rows(attn_s), pad_rows(d_s),
                                          tail_w, n_experts, tm)
    tme = MOE_TILE
    pair_slot, src_tok, tile_e, tile_v, tile_f = _expert_order(route, counts, n_experts, tme)
    y_sorted = _moe(tile_e, tile_v, tile_f, xn2_all[src_tok], w_gate[0], b_gate[0][:, None, :],
                    w_up[0], b_up[0][:, None, :], w_down[0], b_down[0][:, None, :], tme)
    y_pairs = y_sorted[pair_slot.T]
    gfin = g_final[None, :]
    n_p = t_p // tm
    y_p = _final(h_all, y_pairs, route, gfin, tm, 0, n_p)
    y_s = _final(h_all, y_pairs, route, gfin, tm, n_p, 1)[:n_seq]

    w_p = min(WINDOW, seq)
    kv_row = (2, N_KV_HEADS, HEAD_DIM)
    new_col = kvwt_s.reshape(2, KV_HALF, n_seq).transpose(2, 0, 1)[..., None]
    win_new_t = jnp.concatenate([win_t[..., 1:], new_col], axis=-1)
    return (
        y_p.reshape(batch, seq, D_MODEL),
        y_s.reshape(n_seq, 1, D_MODEL),
        _from_key_minor(kvct_p)[None],
        kvc_s.reshape((1, n_seq, 1) + kv_row),
        _from_key_minor(kvst_p)[None],
        kvs_s.reshape((1, n_seq, 1) + kv_row),
        _from_key_minor(kvwt_p[..., seq - w_p:])[None],
        _from_key_minor(win_new_t)[None],
        u_p.reshape(batch, seq, POOL_DIM)[None, :, seq - POOL_BUF:],
        jnp.concatenate([state_pool[0], u_s[:, None, :]], axis=1)[None, :, 1:],
    )
```

```python
import functools
import math

import jax
import jax.numpy as jnp
import numpy as np
from jax import lax
from jax.experimental import pallas as pl
from jax.experimental.pallas import tpu as pltpu

D_MODEL = 1024
N_HEADS = 8
N_KV_HEADS = 2
HEAD_DIM = 64
GROUP = N_HEADS // N_KV_HEADS
NSA_DIM = N_HEADS * HEAD_DIM
KV_DIM = 2 * N_KV_HEADS * HEAD_DIM
KV_HALF = KV_DIM // 2
L_CMP = 32
L_SEL = 64
CMP_HIDDEN = 2 * HEAD_DIM
TOP_N = 16
WINDOW = 512
ROPE_THETA = 10000.0
POOL_WINDOWS = (2, 4, 8, 16)
POOL_DIM = D_MODEL // 2
POOL_GROUP_DIM = POOL_DIM // len(POOL_WINDOWS)
POOL_BUF = max(POOL_WINDOWS) - 1
POOL_HALO = POOL_BUF + 1
TOP_K = 4
SWIGLU_LIMIT = 7.0
SWIGLU_ALPHA = 1.702
RMS_EPS = 1e-6
NEG = -1e30
FORCED = 1e9

LANES = 128
QPAD_DIM = N_HEADS * LANES
VMEM_LIMIT = 56 * 1024 * 1024
Q_SCALE = HEAD_DIM ** -0.5 * math.log2(math.e)
ROW_TILE = 512
CMP_Q_TILE = 256
CMP_BLOCK_TILE = 256
MOE_TILE = 512
SAMPLE_MOE_TILE = 128
SAMPLE_SEQS_PER_STEP = 16
PAIR_UNROLL = 8
HEADS_PER_CHAIN = GROUP

BF16 = jnp.bfloat16
F32 = jnp.float32


def _params(n_grid, vmem=VMEM_LIMIT):
    return pltpu.CompilerParams(dimension_semantics=("arbitrary",) * n_grid, vmem_limit_bytes=vmem)


def _rms(x, g):
    r = lax.rsqrt(jnp.mean(x * x, axis=-1, keepdims=True) + RMS_EPS)
    return x * r * g


def _dot(a, b):
    return jnp.dot(a, b, preferred_element_type=F32)


def _dot_nt(a, b):
    return lax.dot_general(a, b, (((1,), (1,)), ((), ())), preferred_element_type=F32)


def _rope_chunk(x, cos, sin_signed):
    lane = lax.broadcasted_iota(jnp.int32, x.shape, 1)
    first = (lane % HEAD_DIM) < (HEAD_DIM // 2)
    swapped = jnp.where(first, pltpu.roll(x, LANES - HEAD_DIM // 2, 1), pltpu.roll(x, HEAD_DIM // 2, 1))
    return x * cos + swapped * sin_signed


def _project(x, g, cos, sin, pos, wq_ref, wkv_ref, wg_ref, wu_ref, outs):
    (q_ref, kvc_ref, kvct_ref, kvst_ref, kvwt_ref, ksb_ref, vs0_ref, vs1_ref, kwb_ref, vw0_ref, vw1_ref,
     gates_ref, u_ref) = outs
    xn = _rms(x, g).astype(BF16)
    q = _dot(xn, wq_ref[...])
    for h in range(N_HEADS):
        sl = slice(h * LANES, (h + 1) * LANES)
        q_ref[:, sl] = (_rope_chunk(q[:, sl], cos, sin) * Q_SCALE).astype(BF16)
    kv = _dot(xn, wkv_ref[...])
    n_sel = ksb_ref.shape[1] - KV_HALF
    blk = lax.broadcasted_iota(jnp.int32, (x.shape[0], n_sel), 1)
    for j, t_ref in enumerate((kvct_ref, kvst_ref, kvwt_ref)):
        k = _rope_chunk(kv[:, j * KV_DIM:j * KV_DIM + KV_HALF], cos, sin)
        v = kv[:, j * KV_DIM + KV_HALF:(j + 1) * KV_DIM]
        t_ref[0, 0] = k.T
        t_ref[0, 1] = v.T
        low = lax.broadcasted_iota(jnp.int32, v.shape, 1) < HEAD_DIM
        v0 = jnp.where(low, v, 1.0).astype(BF16)
        v1 = jnp.where(low, 1.0, v).astype(BF16)
        if j == 0:
            kvc_ref[:, :KV_HALF] = k
            kvc_ref[:, KV_HALF:] = v
        elif j == 1:
            ksb_ref[:, :KV_HALF] = k.astype(BF16)
            ksb_ref[:, KV_HALF:] = jnp.where(blk == pos // L_SEL, 1.0, 0.0).astype(BF16)
            vs0_ref[...] = v0
            vs1_ref[...] = v1
        else:
            kwb_ref[...] = k.astype(BF16)
            vw0_ref[...] = v0
            vw1_ref[...] = v1
    gates_ref[...] = jax.nn.sigmoid(_dot(xn, wg_ref[...]))
    u = _dot(xn, wu_ref[...])
    u_ref[...] = u
    return u


def _inproj_prompt_kernel(x_ref, xh_ref, cos_ref, sin_ref, g_ref, wq_ref, wkv_ref, wg_ref, wu_ref, *outs,
                          tiles_per_seq):
    d_ref = outs[-1]
    i = pl.program_id(0)
    tm = x_ref.shape[0]
    g = g_ref[...]
    seq_tile = i % tiles_per_seq
    pos = seq_tile * tm + lax.broadcasted_iota(jnp.int32, (tm, 1), 0)
    u = _project(x_ref[...], g, cos_ref[...], sin_ref[...], pos, wq_ref, wkv_ref, wg_ref, wu_ref, outs[:-1])
    uh = _dot(_rms(xh_ref[...], g).astype(BF16), wu_ref[...])
    uh = jnp.where(seq_tile == 0, 0.0, uh)
    ext = jnp.concatenate([uh, u], axis=0)
    for gi, w in enumerate(POOL_WINDOWS):
        sl = slice(gi * POOL_GROUP_DIM, (gi + 1) * POOL_GROUP_DIM)
        s = ext[:, sl]
        k = 1
        while k < w:
            s = s + pltpu.roll(s, k, 0)
            k *= 2
        cnt = jnp.minimum(w, pos + 1).astype(F32)
        d_ref[:, sl] = (s[POOL_HALO:, :] / cnt - u[:, sl]).astype(BF16)


def _inproj_sample_kernel(x_ref, sp_ref, cos_ref, sin_ref, g_ref, wq_ref, wkv_ref, wg_ref, wu_ref, *outs,
                          past_len):
    d_ref = outs[-1]
    pos = jnp.full((x_ref.shape[0], 1), past_len, jnp.int32)
    u = _project(x_ref[...], g_ref[...], cos_ref[...], sin_ref[...], pos, wq_ref, wkv_ref, wg_ref, wu_ref,
                 outs[:-1])
    for gi, w in enumerate(POOL_WINDOWS):
        sl = slice(gi * POOL_GROUP_DIM, (gi + 1) * POOL_GROUP_DIM)
        s = u[:, sl]
        for k in range(1, w):
            s = s + sp_ref[POOL_BUF - k][:, sl]
        cnt = float(min(w, past_len + 1))
        d_ref[:, sl] = (s / cnt - u[:, sl]).astype(BF16)


def _inproj_outs(n_seq, rows, tm, n_sel, row_map, seq_tile):
    t = n_seq * rows
    shapes, specs = [], []

    def add(shape, dtype, block, index_map):
        shapes.append(jax.ShapeDtypeStruct(shape, dtype))
        specs.append(pl.BlockSpec(block, index_map))

    t_map4 = lambda i: (seq_tile(i)[0], 0, 0, seq_tile(i)[1])
    add((t, QPAD_DIM), BF16, (tm, QPAD_DIM), row_map)
    add((t, KV_DIM), F32, (tm, KV_DIM), row_map)
    for _ in range(3):
        add((n_seq, 2, KV_HALF, rows), F32, (1, 2, KV_HALF, tm), t_map4)
    add((t, KV_HALF + n_sel), BF16, (tm, KV_HALF + n_sel), row_map)
    for _ in range(5):
        add((t, KV_HALF), BF16, (tm, KV_HALF), row_map)
    add((t, LANES), F32, (tm, LANES), row_map)
    add((t, POOL_DIM), F32, (tm, POOL_DIM), row_map)
    add((t, POOL_DIM), BF16, (tm, POOL_DIM), row_map)
    return tuple(shapes), tuple(specs)


def _weight_specs(shapes):
    return [pl.BlockSpec(s, lambda i, _n=len(s): (0,) * _n) for s in shapes]


def _inproj_prompt(x, cos, sin, g_mix, wq, wkv, wg, wu, batch, seq, tm):
    tiles_per_seq = seq // tm
    halo_blocks = tm // POOL_HALO
    in_specs = [
        pl.BlockSpec((tm, D_MODEL), lambda i: (i, 0)),
        pl.BlockSpec((POOL_HALO, D_MODEL), lambda i: (jnp.maximum(i * halo_blocks - 1, 0), 0)),
        pl.BlockSpec((tm, LANES), lambda i: (i % tiles_per_seq, 0)),
        pl.BlockSpec((tm, LANES), lambda i: (i % tiles_per_seq, 0)),
    ] + _weight_specs([g_mix.shape, wq.shape, wkv.shape, wg.shape, wu.shape])
    out_shape, out_specs = _inproj_outs(batch, seq, tm, seq // L_SEL, lambda i: (i, 0),
                                        lambda i: (i // tiles_per_seq, i % tiles_per_seq))
    return pl.pallas_call(
        functools.partial(_inproj_prompt_kernel, tiles_per_seq=tiles_per_seq),
        grid=(batch * tiles_per_seq,), in_specs=in_specs, out_specs=out_specs,
        out_shape=out_shape, compiler_params=_params(1), name="inproj_prompt",
    )(x, x, cos, sin, g_mix, wq, wkv, wg, wu)


def _inproj_sample(x, sp_t, cos, sin, g_mix, wq, wkv, wg, wu, past_len):
    t = x.shape[0]
    in_specs = [
        pl.BlockSpec((t, D_MODEL), lambda i: (0, 0)),
        pl.BlockSpec(sp_t.shape, lambda i: (0, 0, 0)),
        pl.BlockSpec((t, LANES), lambda i: (0, 0)),
        pl.BlockSpec((t, LANES), lambda i: (0, 0)),
    ] + _weight_specs([g_mix.shape, wq.shape, wkv.shape, wg.shape, wu.shape])
    out_shape, out_specs = _inproj_outs(1, t, t, LANES, lambda i: (0, 0), lambda i: (0, 0))
    return pl.pallas_call(
        functools.partial(_inproj_sample_kernel, past_len=past_len),
        grid=(1,), in_specs=in_specs, out_specs=out_specs,
        out_shape=out_shape, compiler_params=_params(1), name="inproj_sample",
    )(x, sp_t, cos, sin, g_mix, wq, wkv, wg, wu)


def _compress_rows(x_refs, pe_ref, w1_ref, w2_ref, o_ref):
    nb = o_ref.shape[0]
    for c, x_ref in enumerate(x_refs):
        acc = jnp.zeros((nb, w1_ref.shape[3]), F32)
        for l in range(L_CMP):
            xl = x_ref[pl.ds(l, nb, stride=L_CMP), :] + pe_ref[c, l]
            acc = acc + _dot(xl.astype(BF16), w1_ref[c, l])
        hid = acc * jax.nn.sigmoid(acc)
        o_ref[:, c * KV_HALF:(c + 1) * KV_HALF] = _dot(hid.astype(BF16), w2_ref[c]).astype(o_ref.dtype)


def _compress_kernel(xk_ref, xv_ref, pe_ref, w1_ref, w2_ref, o_ref):
    _compress_rows((xk_ref, xv_ref), pe_ref, w1_ref, w2_ref, o_ref)


def _compress(rows, pe, w1, w2, nb_tile):
    n_blocks = rows.shape[0] // L_CMP
    nb_tile = int(np.gcd(nb_tile, n_blocks))
    return pl.pallas_call(
        _compress_kernel,
        grid=(n_blocks // nb_tile,),
        in_specs=[pl.BlockSpec((nb_tile * L_CMP, KV_HALF), lambda i: (i, 0)),
                  pl.BlockSpec((nb_tile * L_CMP, KV_HALF), lambda i: (i, 1))]
        + _weight_specs([pe.shape, w1.shape, w2.shape]),
        out_specs=pl.BlockSpec((nb_tile, KV_DIM), lambda i: (i, 0)),
        out_shape=jax.ShapeDtypeStruct((n_blocks, KV_DIM), BF16),
        compiler_params=_params(1), name="compress",
    )(rows, rows, pe, w1, w2)


def _compress_pages_kernel(pt_ref, cache_ref, pet_ref, perm_ref, w1_ref, w2_ref, o_ref, buf_ref, sem_ref, xs_ref):
    b = pl.program_id(0)
    n_b = pl.num_programs(0)
    n_pages = buf_ref.shape[1]
    n_pairs = n_pages // 2
    nb = o_ref.shape[0]
    perm = perm_ref[...]

    def page_copy(bb, slot, j):
        return pltpu.make_async_copy(cache_ref.at[pt_ref[bb * n_pages + j]], buf_ref.at[slot, j], sem_ref.at[slot])

    @pl.when(b == 0)
    def _():
        for j in range(n_pages):
            page_copy(0, 0, j).start()

    @pl.when(b + 1 < n_b)
    def _():
        for j in range(n_pages):
            page_copy(b + 1, (b + 1) % 2, j).start()

    slot = b % 2
    for j in range(n_pages):
        page_copy(b, slot, j).wait()

    unroll = int(np.gcd(n_pairs, PAIR_UNROLL))

    def body(it, carry):
        for u in range(unroll):
            pp = it * unroll + u
            for c in range(2):
                xt = jnp.concatenate([buf_ref[slot, 2 * pp, c], buf_ref[slot, 2 * pp + 1, c]], axis=1) + pet_ref[c]
                xs_ref[c, pp] = _dot_nt(perm, xt.astype(BF16))
        return carry

    lax.fori_loop(0, n_pairs // unroll, body, 0)
    rows_per_l = 2 * LANES // L_CMP
    for c in range(2):
        acc = jnp.zeros((nb, w1_ref.shape[3]), F32)
        for l in range(L_CMP):
            xl = xs_ref[c, :, l * rows_per_l:(l + 1) * rows_per_l, :].reshape(nb, KV_HALF)
            acc = acc + _dot(xl.astype(BF16), w1_ref[c, l])
        hid = acc * jax.nn.sigmoid(acc)
        o_ref[:, c * KV_HALF:(c + 1) * KV_HALF] = _dot(hid.astype(BF16), w2_ref[c]).astype(o_ref.dtype)


def _compress_pages(cache_t, page_table, pe, w1, w2):
    _, _, _, page = cache_t.shape
    n_seq, n_pages = page_table.shape
    assert page == LANES and n_pages % 2 == 0
    per_page = page // L_CMP
    pet = jnp.tile(pe[:, :, 0, :].transpose(0, 2, 1), (1, 1, 2 * per_page))
    src = np.arange(2 * page)
    dst = (src % L_CMP) * (2 * per_page) + src // L_CMP
    perm = np.zeros((2 * page, 2 * page), np.float32)
    perm[dst, src] = 1.0
    perm = jnp.asarray(perm, BF16)
    fixed = lambda shape: pl.BlockSpec(shape, lambda b, pt, _n=len(shape): (0,) * _n)
    grid_spec = pltpu.PrefetchScalarGridSpec(
        num_scalar_prefetch=1, grid=(n_seq,),
        in_specs=[pl.BlockSpec(memory_space=pl.ANY)] + [fixed(a.shape) for a in (pet, perm, w1, w2)],
        out_specs=pl.BlockSpec((n_pages * per_page, KV_DIM), lambda b, pt: (b, 0)),
        scratch_shapes=[pltpu.VMEM((2, n_pages) + cache_t.shape[1:], F32), pltpu.SemaphoreType.DMA((2,)),
                        pltpu.VMEM((2, n_pages // 2, 2 * page, KV_HALF), F32)],
    )
    return pl.pallas_call(
        _compress_pages_kernel, grid_spec=grid_spec,
        out_shape=jax.ShapeDtypeStruct((n_seq * n_pages * per_page, KV_DIM), BF16),
        compiler_params=_params(1), name="compress_pages",
    )(page_table.reshape(-1), cache_t, pet, perm, w1, w2)


def _stack_heads(q, g):
    return jnp.concatenate([q[:, (GROUP * g + r) * LANES:(GROUP * g + r + 1) * LANES] for r in range(GROUP)], axis=0)


def _masked_softmax(s, mask):
    s = jnp.where(mask, s, NEG)
    e = jnp.exp2(s - jnp.max(s, axis=-1, keepdims=True))
    return e / jnp.sum(e, axis=-1, keepdims=True) * mask.astype(F32)


def _assemble_heads(o_heads):
    chunks = []
    for c in range(N_HEADS // 2):
        g = (2 * c) // GROUP
        a, b = o_heads[2 * c], o_heads[2 * c + 1]
        if g == 1:
            a = pltpu.roll(a, HEAD_DIM, 1)
        else:
            b = pltpu.roll(b, HEAD_DIM, 1)
        lane = lax.broadcasted_iota(jnp.int32, a.shape, 1)
        chunks.append(jnp.where(lane < HEAD_DIM, a, b))
    return jnp.concatenate(chunks, axis=1)


def _select_blocks(score, n_pick):
    n_blk = score.shape[1]
    lane = lax.broadcasted_iota(jnp.int32, score.shape, 1)
    work = score
    picks = []
    for _ in range(n_pick):
        m = jnp.max(work, axis=-1, keepdims=True)
        idx = jnp.min(jnp.where(work == m, lane, n_blk), axis=-1, keepdims=True)
        picks.append(idx)
        work = jnp.where(lane == idx, -jnp.inf, work)
    return picks


def _select_blocks_t(score_t, n_pick):
    n_blk = score_t.shape[0]
    blk = lax.broadcasted_iota(jnp.int32, score_t.shape, 0)
    work = score_t
    sel = jnp.zeros(score_t.shape, F32)
    for _ in range(n_pick):
        m = jnp.max(work, axis=0, keepdims=True)
        idx = jnp.min(jnp.where(work == m, blk, n_blk), axis=0, keepdims=True)
        hit = blk == idx
        sel = jnp.where(hit, 1.0, sel)
        work = jnp.where(hit, -jnp.inf, work)
    return sel


def _cmp_attention(qg, kc, vc, pos_rows, n_sel):
    s = _dot_nt(qg, kc)
    j = lax.broadcasted_iota(jnp.int32, s.shape, 1)
    n = 2 * (j % n_sel) + j // n_sel
    mask = (n * L_CMP + (L_CMP - 1)) <= pos_rows
    p = _masked_softmax(s, mask)
    return _dot(p.astype(BF16), vc), p


def _cmp_topk_kernel(q_ref, c_ref, gates_ref, oc_ref, msel_ref, *, n_sel, n_top):
    i = pl.program_id(1)
    tq = q_ref.shape[0]
    q = q_ref[...]
    comp = c_ref[0]
    kc, vc = comp[:, :LANES], comp[:, LANES:]
    gates = gates_ref[...]
    pos = i * tq + lax.broadcasted_iota(jnp.int32, (tq, 1), 0)
    pos_rows = jnp.concatenate([pos] * GROUP, axis=0)
    blk = lax.broadcasted_iota(jnp.int32, (n_sel, tq), 0)
    cur = (i * tq + lax.broadcasted_iota(jnp.int32, (1, tq), 1)) // L_SEL
    forced = (blk == 0) | (blk == cur) | (blk == cur - 1)
    o_heads = []
    for g in range(N_KV_HEADS):
        o, p = _cmp_attention(_stack_heads(q, g), kc, vc, pos_rows, n_sel)
        imp = p[0:tq]
        for r in range(1, GROUP):
            imp = imp + p[r * tq:(r + 1) * tq]
        imp = (imp[:, :n_sel] + imp[:, n_sel:]).T
        score = jnp.where(forced, FORCED, jnp.where(blk < cur, imp, NEG))
        sel = jnp.where(score > NEG / 2, _select_blocks_t(score, n_top), 0.0)
        msel_ref[:, g * n_sel:(g + 1) * n_sel] = ((1.0 - sel) * NEG).T.astype(BF16)
        for r in range(GROUP):
            h = GROUP * g + r
            o_heads.append(o[r * tq:(r + 1) * tq] * gates[:, h:h + 1])
    oc_ref[...] = _assemble_heads(o_heads)


def _cmp_topk_prompt(q, comp, gates, batch, seq, tq):
    n_sel = seq // L_SEL
    n_c = comp.shape[1]
    tiles = seq // tq
    row = lambda b, i: (b * tiles + i, 0)
    return pl.pallas_call(
        functools.partial(_cmp_topk_kernel, n_sel=n_sel, n_top=min(TOP_N, n_sel)),
        grid=(batch, tiles),
        in_specs=[pl.BlockSpec((tq, QPAD_DIM), row),
                  pl.BlockSpec((1, n_c, KV_DIM), lambda b, i: (b, 0, 0)),
                  pl.BlockSpec((tq, LANES), row)],
        out_specs=(pl.BlockSpec((tq, NSA_DIM), row), pl.BlockSpec((tq, N_KV_HEADS * n_sel), row)),
        out_shape=(jax.ShapeDtypeStruct((batch * seq, NSA_DIM), F32),
                   jax.ShapeDtypeStruct((batch * seq, N_KV_HEADS * n_sel), BF16)),
        compiler_params=_params(2), name="cmp_topk_prompt",
    )(q, comp, gates)


def _flash_update(carry, q, k, v, bias):
    m, acc = carry
    s = _dot_nt(q, k)
    if bias is not None:
        s = s + bias
    m_new = jnp.maximum(m, jnp.max(s, axis=-1, keepdims=True))
    p = jnp.exp2(s - m_new).astype(BF16)
    acc = jnp.exp2(m - m_new) * acc + _dot(p, v)
    return m_new, acc


def _sel_win_kernel(q_ref, ks_ref, vs0_ref, vs1_ref, kw_ref, vw0_ref, vw1_ref, msel_ref, gates_ref, oc_ref, o_ref,
                    *, n_sel):
    i = pl.program_id(1)
    tq = q_ref.shape[0]
    hpc = HEADS_PER_CHAIN
    n_chain = N_HEADS // hpc
    rows = hpc * tq
    q = q_ref[...]
    gates = gates_ref[...]
    vs_refs, vw_refs = (vs0_ref, vs1_ref), (vw0_ref, vw1_ref)
    pos = i * tq + lax.broadcasted_iota(jnp.int32, (tq, 1), 0)
    key = lax.broadcasted_iota(jnp.int32, (1, tq), 1)
    d0 = pl.multiple_of(i * tq, tq)
    p0 = pl.multiple_of(jnp.maximum(i - 1, 0) * tq, tq)
    bias_d = jnp.concatenate([jnp.where(i * tq + key <= pos, 0.0, NEG)] * hpc, axis=0)
    in_win = (pos - ((i - 1) * tq + key) < WINDOW) & (i > 0)
    bias_p = jnp.concatenate([jnp.where(in_win, 0.0, NEG)] * hpc, axis=0)
    groups = [(c * hpc) // GROUP for c in range(n_chain)]
    qcs, qas = [], []
    for c in range(n_chain):
        g = groups[c]
        qc = jnp.concatenate([q[:, h * LANES:(h + 1) * LANES] for h in range(c * hpc, (c + 1) * hpc)], axis=0)
        msel = jnp.concatenate([msel_ref[:, g * n_sel:(g + 1) * n_sel]] * hpc, axis=0)
        qcs.append(qc)
        qas.append(jnp.concatenate([qc, msel], axis=1))
    init = (jnp.full((rows, 1), NEG, F32), jnp.zeros((rows, LANES), F32))

    def tiles(k_ref, v_refs, start):
        return k_ref[pl.ds(start, tq), :], [v_ref[pl.ds(start, tq), :] for v_ref in v_refs]

    kd, vd = tiles(ks_ref, vs_refs, d0)
    sel = tuple(_flash_update(init, qas[c], kd, vd[groups[c]], bias_d) for c in range(n_chain))

    def body(kt, carries):
        k, v = tiles(ks_ref, vs_refs, pl.multiple_of(kt * tq, tq))
        return tuple(_flash_update(carries[c], qas[c], k, v[groups[c]], None) for c in range(n_chain))

    sel = lax.fori_loop(0, i, body, sel)
    kd, vd = tiles(kw_ref, vw_refs, d0)
    kp, vp = tiles(kw_ref, vw_refs, p0)
    o_heads = []
    for c in range(n_chain):
        g = groups[c]
        win = _flash_update(_flash_update(init, qcs[c], kd, vd[g], bias_d), qcs[c], kp, vp[g], bias_p)
        den = (1 - g) * HEAD_DIM
        o_s = sel[c][1] / sel[c][1][:, den:den + 1]
        o_w = win[1] / win[1][:, den:den + 1]
        for r in range(hpc):
            h = c * hpc + r
            rs = slice(r * tq, (r + 1) * tq)
            o_heads.append(o_s[rs] * gates[:, N_HEADS + h:N_HEADS + h + 1]
                           + o_w[rs] * gates[:, 2 * N_HEADS + h:2 * N_HEADS + h + 1])
    o_ref[...] = (_assemble_heads(o_heads) + oc_ref[...]).astype(o_ref.dtype)


def _sel_win_prompt(q, ksb, vs0, vs1, kwb, vw0, vw1, msel, gates, oc, batch, seq, tq):
    assert tq == WINDOW
    n_sel = seq // L_SEL
    tiles = seq // tq
    row = lambda b, i: (b * tiles + i, 0)
    whole = lambda b, i: (b, 0)
    return pl.pallas_call(
        functools.partial(_sel_win_kernel, n_sel=n_sel),
        grid=(batch, tiles),
        in_specs=[pl.BlockSpec((tq, QPAD_DIM), row),
                  pl.BlockSpec((seq, KV_HALF + n_sel), whole, pipeline_mode=pl.Buffered(1))]
        + [pl.BlockSpec((seq, KV_HALF), whole, pipeline_mode=pl.Buffered(1))] * 5
        + [pl.BlockSpec((tq, N_KV_HEADS * n_sel), row),
           pl.BlockSpec((tq, LANES), row),
           pl.BlockSpec((tq, NSA_DIM), row)],
        out_specs=pl.BlockSpec((tq, NSA_DIM), row),
        out_shape=jax.ShapeDtypeStruct((batch * seq, NSA_DIM), BF16),
        compiler_params=_params(2), name="sel_win_prompt",
    )(q, ksb, vs0, vs1, kwb, vw0, vw1, msel, gates, oc)


def _sample_cmp_kernel(q_ref, c_ref, gates_ref, oc_ref, idx_ref, *, n_blk, n_pick, past_len):
    n_sb = q_ref.shape[0]
    pos_rows = jnp.full((N_HEADS, 1), past_len, jnp.int32)
    imps = []
    for s in range(n_sb):
        comp = c_ref[s]
        o, p = _cmp_attention(q_ref[s], comp[:, :LANES], comp[:, LANES:], pos_rows, n_blk)
        oc_ref[s] = o * gates_ref[s][:, 0:1]
        imps += [jnp.sum(p[GROUP * g:GROUP * (g + 1)], axis=0, keepdims=True) for g in range(N_KV_HEADS)]
    imp = jnp.concatenate(imps, axis=0)
    imp = imp[:, :n_blk] + imp[:, n_blk:]
    cur = past_len // L_SEL
    blk = lax.broadcasted_iota(jnp.int32, imp.shape, 1)
    forced = (blk == 0) | (blk == cur) | (blk == cur - 1)
    score = jnp.where(forced, FORCED, jnp.where(blk < cur, imp, NEG))
    lane = lax.broadcasted_iota(jnp.int32, idx_ref.shape, 1)
    out = jnp.zeros(idx_ref.shape, jnp.int32)
    for k, idx in enumerate(_select_blocks(score, n_pick)):
        out = jnp.where(lane == k, idx, out)
    idx_ref[...] = out


def _sample_cmp(q3, comp, gates3, n_pick, past_len, seqs_per_step):
    n_seq = q3.shape[0]
    n_c = comp.shape[1]
    n_sb = int(np.gcd(seqs_per_step, n_seq))
    blk3 = lambda b: (b, 0, 0)
    return pl.pallas_call(
        functools.partial(_sample_cmp_kernel, n_blk=n_c // 2, n_pick=n_pick, past_len=past_len),
        grid=(n_seq // n_sb,),
        in_specs=[pl.BlockSpec((n_sb, N_HEADS, LANES), blk3),
                  pl.BlockSpec((n_sb, n_c, KV_DIM), blk3),
                  pl.BlockSpec((n_sb, N_HEADS, LANES), blk3)],
        out_specs=(pl.BlockSpec((n_sb, N_HEADS, LANES), blk3),
                   pl.BlockSpec((n_sb * N_KV_HEADS, LANES), lambda b: (b, 0))),
        out_shape=(jax.ShapeDtypeStruct((n_seq, N_HEADS, LANES), F32),
                   jax.ShapeDtypeStruct((n_seq * N_KV_HEADS, LANES), jnp.int32)),
        compiler_params=_params(1), name="sample_cmp",
    )(q3, comp, gates3)


def _softmax_with_new(s, valid, s_new):
    s = jnp.where(valid, s, NEG)
    m = jnp.maximum(jnp.max(s, axis=-1, keepdims=True), s_new)
    e = jnp.exp2(s - m) * valid.astype(F32)
    e_new = jnp.exp2(s_new - m)
    den = jnp.sum(e, axis=-1, keepdims=True) + e_new
    return e / den, e_new / den


def _sample_sel_win_kernel(pg_ref, hf_ref, q_ref, win_ref, kvs_ref, kvw_ref, gates_ref, oc_ref, cache_ref,
                           o_ref, buf_ref, sem_ref, *, n_pick):
    b = pl.program_id(0)
    n_b = pl.num_programs(0)
    n_dma = N_KV_HEADS * n_pick
    page = buf_ref.shape[-1]

    def page_copy(bb, slot, j):
        return pltpu.make_async_copy(cache_ref.at[pg_ref[bb * n_dma + j]], buf_ref.at[slot, j], sem_ref.at[slot])

    @pl.when(b == 0)
    def _():
        for j in range(n_dma):
            page_copy(0, 0, j).start()

    @pl.when(b + 1 < n_b)
    def _():
        for j in range(n_dma):
            page_copy(b + 1, (b + 1) % 2, j).start()

    q = q_ref[0]
    qf = q.astype(F32)
    gates = gates_ref[0]
    row = lax.broadcasted_iota(jnp.int32, (N_HEADS, LANES), 0)

    def new_key(kv_ref):
        kv = kv_ref[0].astype(BF16).astype(F32)
        return jnp.sum(qf * kv[:, :KV_HALF], axis=-1, keepdims=True), kv[:, KV_HALF:]

    win = win_ref[0]
    kt, vt = win[0].astype(BF16), win[1].astype(BF16)
    s_new, v_new = new_key(kvw_ref)
    lane_w = lax.broadcasted_iota(jnp.int32, (N_HEADS, kt.shape[1]), 1)
    p, p_new = _softmax_with_new(_dot(q, kt), lane_w >= 1, s_new)
    o_w = _dot_nt(p.astype(BF16), vt) + p_new.astype(BF16).astype(F32) * v_new

    slot = b % 2
    for j in range(n_dma):
        page_copy(b, slot, j).wait()
    s_new, v_new = new_key(kvs_ref)
    lane_p = lax.broadcasted_iota(jnp.int32, (N_HEADS, page), 1)
    o_s = jnp.zeros((N_HEADS, LANES), F32)
    for g in range(N_KV_HEADS):
        kts, vts, valids = [], [], []
        for k in range(n_pick):
            j = g * n_pick + k
            pg = buf_ref[slot, j]
            kts.append(pg[0].astype(BF16))
            vts.append(pg[1].astype(BF16))
            half = hf_ref[b * n_dma + j]
            valids.append((lane_p // L_SEL) == half)
        kt_all = jnp.concatenate(kts, axis=1)
        vt_all = jnp.concatenate(vts, axis=1)
        p, p_new = _softmax_with_new(_dot(q, kt_all), jnp.concatenate(valids, axis=1), s_new)
        o_g = _dot_nt(p.astype(BF16), vt_all) + p_new.astype(BF16).astype(F32) * v_new
        o_s = jnp.where(row // GROUP == g, o_g, o_s)
    o_ref[0] = oc_ref[0] + o_s * gates[:, 1:2] + o_w * gates[:, 2:3]


def _sample_sel_win(pages, halves, q3, win_t, kvs, kvw, gates3, oc, cache_t, n_pick):
    n_seq = q3.shape[0]
    n_dma = N_KV_HEADS * n_pick
    blk3 = lambda b, pg, hf: (b, 0, 0)
    grid_spec = pltpu.PrefetchScalarGridSpec(
        num_scalar_prefetch=2, grid=(n_seq,),
        in_specs=[pl.BlockSpec((1, N_HEADS, LANES), blk3),
                  pl.BlockSpec((1,) + win_t.shape[1:], lambda b, pg, hf: (b, 0, 0, 0)),
                  pl.BlockSpec((1, 1, KV_DIM), blk3),
                  pl.BlockSpec((1, 1, KV_DIM), blk3),
                  pl.BlockSpec((1, N_HEADS, LANES), blk3),
                  pl.BlockSpec((1, N_HEADS, LANES), blk3),
                  pl.BlockSpec(memory_space=pl.ANY)],
        out_specs=pl.BlockSpec((1, N_HEADS, LANES), blk3),
        scratch_shapes=[pltpu.VMEM((2, n_dma) + cache_t.shape[1:], F32), pltpu.SemaphoreType.DMA((2,))],
    )
    return pl.pallas_call(
        functools.partial(_sample_sel_win_kernel, n_pick=n_pick),
        grid_spec=grid_spec,
        out_shape=jax.ShapeDtypeStruct((n_seq, N_HEADS, LANES), F32),
        compiler_params=_params(1), name="sample_sel_win",
    )(pages, halves, q3, win_t, kvs, kvw, gates3, oc, cache_t)


def _tail_kernel(x_ref, attn_ref, d_ref, gmix_ref, wgm_ref, wpool_ref, pscale_ref,
                 wbn_ref, wbp_ref, wout_ref, gffn_ref, wr_ref, br_ref, h_ref, xn2_ref, route_ref, cout_ref, run_ref,
                 *, n_experts):
    i = pl.program_id(0)
    x = x_ref[...]
    attn = attn_ref[...]
    d = d_ref[...]
    tm = x.shape[0]
    xn = _rms(x, gmix_ref[...]).astype(BF16)
    gm = jax.nn.sigmoid(_dot(xn, wgm_ref[...]))
    pool = jnp.concatenate(
        [_dot(d[:, gi * POOL_GROUP_DIM:(gi + 1) * POOL_GROUP_DIM], wpool_ref[gi]) for gi in range(len(POOL_WINDOWS))],
        axis=1) * pscale_ref[...]
    merged = (gm[:, :D_MODEL] * _dot(attn, wbn_ref[...])
              + gm[:, D_MODEL:] * _dot(pool.astype(BF16), wbp_ref[...]))
    h = x + _dot(merged.astype(BF16), wout_ref[...])
    h_ref[...] = h
    xn2 = _rms(h, gffn_ref[...]).astype(BF16)
    xn2_ref[...] = xn2
    logits = _dot(xn2, wr_ref[...]) + br_ref[...]
    lane = lax.broadcasted_iota(jnp.int32, (tm, LANES), 1)
    work = jnp.where(lane < n_experts, logits, -jnp.inf)
    vals, onehots = [], []
    for _ in range(TOP_K):
        m = jnp.max(work, axis=-1, keepdims=True)
        idx = jnp.min(jnp.where(work == m, lane, LANES), axis=-1, keepdims=True)
        vals.append(m)
        onehots.append(lane == idx)
        work = jnp.where(lane == idx, -jnp.inf, work)
    es = [jnp.exp(v - vals[0]) for v in vals]
    tot = es[0] + es[1] + es[2] + es[3]

    @pl.when(i == 0)
    def _():
        run_ref[...] = jnp.zeros(run_ref.shape, F32)

    chosen = jnp.zeros((tm, LANES), F32)
    for k in range(TOP_K):
        chosen = jnp.where(onehots[k], 1.0, chosen)
    earlier = lax.broadcasted_iota(jnp.int32, (tm, tm), 0) > lax.broadcasted_iota(jnp.int32, (tm, tm), 1)
    before = run_ref[...] + _dot(jnp.where(earlier, 1.0, 0.0).astype(BF16), chosen.astype(BF16))
    route = jnp.zeros((tm, LANES), F32)
    lane_f = lane.astype(F32)
    for k in range(TOP_K):
        e_k = jnp.sum(jnp.where(onehots[k], lane_f, 0.0), axis=-1, keepdims=True)
        r_k = jnp.sum(jnp.where(onehots[k], before, 0.0), axis=-1, keepdims=True)
        route = jnp.where(lane == k, e_k, route)
        route = jnp.where(lane == TOP_K + k, es[k] / tot, route)
        route = jnp.where(lane == 2 * TOP_K + k, r_k, route)
    route_ref[...] = route
    run_ref[...] = run_ref[...] + jnp.sum(chosen, axis=0, keepdims=True)
    cout_ref[...] = run_ref[...]


def _tail(x, attn, d, weights, n_experts, tm):
    t = x.shape[0]
    row = lambda i: (i, 0)
    fixed = lambda i: (0, 0)
    return pl.pallas_call(
        functools.partial(_tail_kernel, n_experts=n_experts),
        grid=(t // tm,),
        in_specs=[pl.BlockSpec((tm, D_MODEL), row), pl.BlockSpec((tm, NSA_DIM), row),
                  pl.BlockSpec((tm, POOL_DIM), row)]
        + _weight_specs([w.shape for w in weights]),
        out_specs=(pl.BlockSpec((tm, D_MODEL), row), pl.BlockSpec((tm, D_MODEL), row),
                   pl.BlockSpec((tm, LANES), row), pl.BlockSpec((1, LANES), fixed)),
        out_shape=(jax.ShapeDtypeStruct((t, D_MODEL), F32), jax.ShapeDtypeStruct((t, D_MODEL), BF16),
                   jax.ShapeDtypeStruct((t, LANES), F32), jax.ShapeDtypeStruct((1, LANES), F32)),
        scratch_shapes=[pltpu.VMEM((1, LANES), F32)],
        compiler_params=_params(1), name="tail",
    )(x, attn, d, *weights)


def _moe_kernel(te_ref, tv_ref, tf_ref, x_ref, wg_ref, bg_ref, wu_ref, bu_ref, wd_ref, bd_ref, y_ref,
                wgb_ref, wub_ref, wdb_ref):
    t = pl.program_id(0)

    @pl.when(tf_ref[t] > 0)
    def _():
        wgb_ref[...] = wg_ref[0].astype(BF16)
        wub_ref[...] = wu_ref[0].astype(BF16)
        wdb_ref[...] = wd_ref[0].astype(BF16)

    @pl.when(tv_ref[t] > 0)
    def _():
        x = x_ref[...]
        gate = jnp.minimum(_dot(x, wgb_ref[...]) + bg_ref[0], SWIGLU_LIMIT)
        up = jnp.clip(_dot(x, wub_ref[...]) + bu_ref[0], -SWIGLU_LIMIT, SWIGLU_LIMIT)
        hid = (up + 1.0) * gate * jax.nn.sigmoid(SWIGLU_ALPHA * gate)
        y_ref[...] = (_dot(hid.astype(BF16), wdb_ref[...]) + bd_ref[0]).astype(y_ref.dtype)

    @pl.when(tv_ref[t] == 0)
    def _():
        y_ref[...] = jnp.zeros(y_ref.shape, y_ref.dtype)


def _moe(tile_e, tile_v, tile_f, xs, wg, bg, wu, bu, wd, bd, tme):
    p_pad = xs.shape[0]
    d_e = wg.shape[2]
    wspec = lambda shape: pl.BlockSpec((1,) + shape, lambda t, te, tv, tf: (te[t], 0, 0))
    grid_spec = pltpu.PrefetchScalarGridSpec(
        num_scalar_prefetch=3, grid=(p_pad // tme,),
        in_specs=[pl.BlockSpec((tme, D_MODEL), lambda t, te, tv, tf: (t, 0)),
                  wspec((D_MODEL, d_e)), wspec((1, d_e)), wspec((D_MODEL, d_e)), wspec((1, d_e)),
                  wspec((d_e, D_MODEL)), wspec((1, D_MODEL))],
        out_specs=pl.BlockSpec((tme, D_MODEL), lambda t, te, tv, tf: (t, 0)),
        scratch_shapes=[pltpu.VMEM((D_MODEL, d_e), BF16), pltpu.VMEM((D_MODEL, d_e), BF16),
                        pltpu.VMEM((d_e, D_MODEL), BF16)],
    )
    return pl.pallas_call(
        _moe_kernel, grid_spec=grid_spec,
        out_shape=jax.ShapeDtypeStruct((p_pad, D_MODEL), BF16),
        compiler_params=_params(1), name="moe_experts",
    )(tile_e, tile_v, tile_f, xs, wg, bg, wu, bu, wd, bd)


def _final_kernel(h_ref, y_ref, route_ref, g_ref, o_ref):
    route = route_ref[...]
    acc = h_ref[...]
    for k in range(TOP_K):
        acc = acc + y_ref[k].astype(F32) * route[:, TOP_K + k:TOP_K + k + 1]
    o_ref[...] = _rms(acc, g_ref[...])


def _final(h, yg, route, g_final, tm, first, n_tiles):
    src = lambda i: (i + first, 0)
    return pl.pallas_call(
        _final_kernel, grid=(n_tiles,),
        in_specs=[pl.BlockSpec((tm, D_MODEL), src), pl.BlockSpec((TOP_K, tm, D_MODEL), lambda i: (0, i + first, 0)),
                  pl.BlockSpec((tm, LANES), src), pl.BlockSpec((1, D_MODEL), lambda i: (0, 0))],
        out_specs=pl.BlockSpec((tm, D_MODEL), lambda i: (i, 0)),
        out_shape=jax.ShapeDtypeStruct((n_tiles * tm, D_MODEL), F32),
        compiler_params=_params(1), name="final_norm",
    )(h, yg, route, g_final)


def _rope_tables(pos):
    half = HEAD_DIM // 2
    inv = ROPE_THETA ** (-jnp.arange(half, dtype=F32) / half)
    ang = pos.astype(F32)[:, None] * inv[None, :]
    cos, sin = jnp.cos(ang), jnp.sin(ang)
    cos = jnp.concatenate([cos, cos] * (LANES // HEAD_DIM), axis=1)
    sin = jnp.concatenate([-sin, sin] * (LANES // HEAD_DIM), axis=1)
    return cos, sin


def _split_w_in(w_in):
    o_q = NSA_DIM
    o_g = o_q + 3 * KV_DIM
    o_u = o_g + 3 * N_HEADS
    o_m = o_u + POOL_DIM
    wq = w_in[:, :o_q].reshape(D_MODEL, N_HEADS, HEAD_DIM)
    zeros = jnp.zeros_like(wq)
    lo = jnp.concatenate([wq, zeros], axis=2)
    hi = jnp.concatenate([zeros, wq], axis=2)
    in_hi = (jnp.arange(N_HEADS) // GROUP == 1)[None, :, None]
    wq_pad = jnp.where(in_hi, hi, lo).reshape(D_MODEL, QPAD_DIM)
    wkv = w_in[:, o_q:o_g]
    wg = w_in[:, o_g:o_u].reshape(D_MODEL, N_HEADS, 3).transpose(0, 2, 1).reshape(D_MODEL, 3 * N_HEADS)
    wg = jnp.pad(wg, ((0, 0), (0, LANES - 3 * N_HEADS)))
    wu = w_in[:, o_u:o_m]
    wgm = w_in[:, o_m:]
    return [w.astype(BF16) for w in (wq_pad, wkv, wg, wu, wgm)]


def _compress_weights(cmp_pe, cmp_w1, cmp_w2):
    eye = jnp.eye(N_KV_HEADS, dtype=F32)
    w1 = jnp.einsum('cldh,gf->clgdfh', cmp_w1, eye).reshape(2, L_CMP, KV_HALF, N_KV_HEADS * CMP_HIDDEN)
    w2 = jnp.einsum('chd,gf->cghfd', cmp_w2, eye).reshape(2, N_KV_HEADS * CMP_HIDDEN, KV_HALF)
    pe = jnp.broadcast_to(cmp_pe.transpose(1, 0, 2)[:, :, None, :], (2, L_CMP, N_KV_HEADS, HEAD_DIM))
    return pe.reshape(2, L_CMP, 1, KV_HALF), w1.astype(BF16), w2.astype(BF16)


def _even_odd(comp, n_seq):
    n_c = comp.shape[0] // n_seq
    return comp.reshape(n_seq, n_c // 2, 2, KV_DIM).transpose(0, 2, 1, 3).reshape(n_seq, n_c, KV_DIM)


def _fix_sample_heads(o3):
    n_seq = o3.shape[0]
    o4 = o3.reshape(n_seq, N_KV_HEADS, GROUP, N_KV_HEADS, HEAD_DIM)
    return jnp.concatenate([o4[:, g, :, g, :] for g in range(N_KV_HEADS)], axis=1).reshape(n_seq, NSA_DIM)


def _gates_by_head(gates):
    g = gates[:, :3 * N_HEADS].reshape(-1, 3, N_HEADS).transpose(0, 2, 1)
    return jnp.pad(g, ((0, 0), (0, 0), (0, LANES - 3)))


def _key_minor(x):
    n, p = x.shape[:2]
    return x.transpose(0, 2, 3, 4, 1).reshape(n, 2, KV_HALF, p)


def _from_key_minor(xt):
    n, _, _, p = xt.shape
    return xt.reshape(n, 2, N_KV_HEADS, HEAD_DIM, p).transpose(0, 4, 1, 2, 3)


def _expert_order(route, counts, n_experts, tme):
    t = route.shape[0]
    n_pairs = t * TOP_K
    e = route[:, :TOP_K].astype(jnp.int32)
    rank = route[:, 2 * TOP_K:3 * TOP_K].astype(jnp.int32)
    counts = counts[0, :n_experts].astype(jnp.int32)
    padded = ((counts + tme - 1) // tme) * tme
    ends = jnp.cumsum(padded)
    off = ends - padded
    pad_before = off - (jnp.cumsum(counts) - counts)
    experts = jnp.arange(n_experts, dtype=jnp.int32)
    pair_slot = rank + jnp.sum(jnp.where(e[:, :, None] == experts, off, 0), axis=-1)
    n_tiles = -(-n_pairs // tme) + n_experts
    p_pad = n_tiles * tme
    tok = jnp.broadcast_to(jnp.arange(t, dtype=jnp.int32)[:, None], (t, TOP_K))
    _, tok_sorted = lax.sort((pair_slot.reshape(n_pairs), tok.reshape(n_pairs)), num_keys=1)
    slot = jnp.arange(p_pad, dtype=jnp.int32)
    slot_e = jnp.sum((slot[:, None] >= ends[None, :]).astype(jnp.int32), axis=-1)
    slot_e = jnp.minimum(slot_e, n_experts - 1)
    slot_pad = jnp.sum(jnp.where(slot_e[:, None] == experts, pad_before, 0), axis=-1)
    src_tok = tok_sorted[jnp.clip(slot - slot_pad, 0, n_pairs - 1)]
    tile_start = jnp.arange(n_tiles, dtype=jnp.int32) * tme
    tile_e = jnp.minimum(jnp.sum((tile_start[:, None] >= ends[None, :]).astype(jnp.int32), axis=-1), n_experts - 1)
    tile_v = (tile_start < ends[-1]).astype(jnp.int32)
    tile_f = jnp.concatenate([jnp.ones((1,), jnp.int32), (tile_e[1:] != tile_e[:-1]).astype(jnp.int32)])
    return pair_slot, src_tok, tile_e, tile_v, tile_f


def kernel(x_prompt, x_sample, cache_kv_cmp, cache_kv_sel, state_kv_win, state_pool, page_table, g_mix, w_in, cmp_pe, cmp_w1, cmp_w2, w_pool, pool_scale, w_br_nsa, w_br_pool, w_out, g_ffn, w_router, b_router, w_gate, b_gate, w_up, b_up, w_down, b_down, g_final):
    batch, seq, _ = x_prompt.shape
    n_seq, dec_seq, _ = x_sample.shape
    depth = g_mix.shape[0]
    assert depth == 1 and dec_seq == 1
    page_size = cache_kv_cmp.shape[2]
    n_pages = page_table.shape[1]
    past_len = n_pages * page_size
    n_experts = w_router.shape[2]
    t_p = batch * seq
    tm = min(ROW_TILE, seq)

    wq, wkv, wg, wu, wgm = _split_w_in(w_in[0])
    gmix = g_mix[0][None, :]
    pe, w1, w2 = _compress_weights(cmp_pe[0], cmp_w1[0], cmp_w2[0])
    cos_p, sin_p = _rope_tables(jnp.arange(seq, dtype=jnp.int32))
    cos_s, sin_s = _rope_tables(jnp.full((n_seq,), past_len, jnp.int32))
    wr = jnp.pad(w_router[0], ((0, 0), (0, LANES - n_experts))).astype(BF16)
    br = jnp.pad(b_router[0], (0, LANES - n_experts))[None, :]
    tail_w = [gmix, wgm, w_pool[0].astype(BF16), pool_scale[0][None, :], w_br_nsa[0].astype(BF16),
              w_br_pool[0].astype(BF16), w_out[0].astype(BF16), g_ffn[0][None, :], wr, br]

    xp = x_prompt.reshape(t_p, D_MODEL)
    (q_p, kvc_p, kvct_p, kvst_p, kvwt_p, ksb_p, vs0_p, vs1_p, kwb_p, vw0_p, vw1_p, gates_p, u_p, d_p) = _inproj_prompt(
        xp, cos_p, sin_p, gmix, wq, wkv, wg, wu, batch, seq, tm)
    comp_p = _even_odd(_compress(kvc_p, pe, w1, w2, CMP_BLOCK_TILE), batch)
    tq = min(CMP_Q_TILE, seq)
    oc_p, msel_p = _cmp_topk_prompt(q_p, comp_p, gates_p, batch, seq, tq)
    attn_p = _sel_win_prompt(q_p, ksb_p, vs0_p, vs1_p, kwb_p, vw0_p, vw1_p, msel_p, gates_p, oc_p, batch, seq, WINDOW)

    def ffn(x, attn, d, tile, tme, after=None):
        h, xn2, route, counts = _tail(x, attn, d, tail_w, n_experts, tile)
        pair_slot, src_tok, tile_e, tile_v, tile_f = _expert_order(route, counts, n_experts, tme)
        tied = after(src_tok) if after is not None else None
        y_sorted = _moe(tile_e, tile_v, tile_f, xn2[src_tok], w_gate[0], b_gate[0][:, None, :],
                        w_up[0], b_up[0][:, None, :], w_down[0], b_down[0][:, None, :], tme)
        y = _final(h, y_sorted[pair_slot.T], route, g_final[None, :], tile, 0, x.shape[0] // tile)
        return y, tied

    tie = lambda order: lax.optimization_barrier((x_sample.reshape(n_seq, D_MODEL), page_table, order))[:2]
    y_p, (xs, page_table) = ffn(xp, attn_p, d_p, tm, MOE_TILE, after=tie)

    sp_t = state_pool[0].transpose(1, 0, 2)
    (q_s, kvc_s, _, kvst_s, kvwt_s, _, _, _, _, _, _, gates_s, u_s, d_s) = _inproj_sample(
        xs, sp_t, cos_s, sin_s, gmix, wq, wkv, wg, wu, past_len)
    kvs_s = kvst_s.reshape(KV_DIM, n_seq).T
    kvw_s = kvwt_s.reshape(KV_DIM, n_seq).T
    comp_s = _even_odd(_compress_pages(_key_minor(cache_kv_cmp[0]), page_table, pe, w1, w2), n_seq)
    q3 = q_s.reshape(n_seq, N_HEADS, LANES)
    gates3 = _gates_by_head(gates_s)
    n_blk = past_len // L_SEL
    n_pick = min(TOP_N, n_blk + 1) - 1
    oc_s, idx_s = _sample_cmp(q3, comp_s, gates3, n_pick, past_len, SAMPLE_SEQS_PER_STEP)
    top_idx = idx_s.reshape(n_seq, N_KV_HEADS, LANES)[:, :, :n_pick]
    sel_per_page = page_size // L_SEL
    pages_b = jnp.broadcast_to(page_table[:, None, :], (n_seq, N_KV_HEADS, n_pages))
    phys = jnp.take_along_axis(pages_b, top_idx // sel_per_page, axis=2).reshape(-1)
    halves = (top_idx % sel_per_page).reshape(-1)
    w_buf = state_kv_win.shape[2]
    assert w_buf == WINDOW
    win_t = _key_minor(state_kv_win[0])
    o3 = _sample_sel_win(phys, halves, q3, win_t, kvs_s[:, None, :], kvw_s[:, None, :], gates3, oc_s,
                         _key_minor(cache_kv_sel[0]), n_pick)
    attn_s = _fix_sample_heads(o3).astype(BF16)
    y_s, _ = ffn(xs, attn_s, d_s, n_seq, SAMPLE_MOE_TILE)

    w_p = min(WINDOW, seq)
    kv_row = (2, N_KV_HEADS, HEAD_DIM)
    new_col = kvwt_s.reshape(2, KV_HALF, n_seq).transpose(2, 0, 1)[..., None]
    win_new_t = jnp.concatenate([win_t[..., 1:], new_col], axis=-1)
    return (
        y_p.reshape(batch, seq, D_MODEL),
        y_s.reshape(n_seq, 1, D_MODEL),
        _from_key_minor(kvct_p)[None],
        kvc_s.reshape((1, n_seq, 1) + kv_row),
        _from_key_minor(kvst_p)[None],
        kvs_s.reshape((1, n_seq, 1) + kv_row),
        _from_key_minor(kvwt_p[..., seq - w_p:])[None],
        _from_key_minor(win_new_t)[None],
        u_p.reshape(batch, seq, POOL_DIM)[None, :, seq - POOL_BUF:],
        jnp.concatenate([state_pool[0], u_s[:, None, :]], axis=1)[None, :, 1:],
    )
```

```python
import functools
import math

import jax
import jax.numpy as jnp
import numpy as np
from jax import lax
from jax.experimental import pallas as pl
from jax.experimental.pallas import tpu as pltpu

D_MODEL = 1024
N_HEADS = 8
N_KV_HEADS = 2
HEAD_DIM = 64
GROUP = N_HEADS // N_KV_HEADS
NSA_DIM = N_HEADS * HEAD_DIM
KV_DIM = 2 * N_KV_HEADS * HEAD_DIM
KV_HALF = KV_DIM // 2
L_CMP = 32
L_SEL = 64
CMP_HIDDEN = 2 * HEAD_DIM
TOP_N = 16
WINDOW = 512
ROPE_THETA = 10000.0
POOL_WINDOWS = (2, 4, 8, 16)
POOL_DIM = D_MODEL // 2
POOL_GROUP_DIM = POOL_DIM // len(POOL_WINDOWS)
POOL_BUF = max(POOL_WINDOWS) - 1
POOL_HALO = POOL_BUF + 1
TOP_K = 4
SWIGLU_LIMIT = 7.0
SWIGLU_ALPHA = 1.702
RMS_EPS = 1e-6
NEG = -1e30
FORCED = 1e9

LANES = 128
QPAD_DIM = N_HEADS * LANES
VMEM_LIMIT = 56 * 1024 * 1024
Q_SCALE = HEAD_DIM ** -0.5 * math.log2(math.e)
ROW_TILE = 512
CMP_Q_TILE = 256
CMP_BLOCK_TILE = 256
MOE_TILE = 512
SAMPLE_MOE_TILE = 128
SAMPLE_SEQS_PER_STEP = 16
PAIR_UNROLL = 8
HEADS_PER_CHAIN = GROUP

BF16 = jnp.bfloat16
F32 = jnp.float32


def _params(n_grid, vmem=VMEM_LIMIT):
    return pltpu.CompilerParams(dimension_semantics=("arbitrary",) * n_grid, vmem_limit_bytes=vmem)


def _rms(x, g):
    r = lax.rsqrt(jnp.mean(x * x, axis=-1, keepdims=True) + RMS_EPS)
    return x * r * g


def _dot(a, b):
    return jnp.dot(a, b, preferred_element_type=F32)


def _dot_nt(a, b):
    return lax.dot_general(a, b, (((1,), (1,)), ((), ())), preferred_element_type=F32)


def _rope_chunk(x, cos, sin_signed):
    lane = lax.broadcasted_iota(jnp.int32, x.shape, 1)
    first = (lane % HEAD_DIM) < (HEAD_DIM // 2)
    swapped = jnp.where(first, pltpu.roll(x, LANES - HEAD_DIM // 2, 1), pltpu.roll(x, HEAD_DIM // 2, 1))
    return x * cos + swapped * sin_signed


def _project(x, g, cos, sin, pos, wq_ref, wkv_ref, wg_ref, wu_ref, outs):
    (q_ref, kvc_ref, kvct_ref, kvst_ref, kvwt_ref, ksb_ref, vs0_ref, vs1_ref, kwb_ref, vw0_ref, vw1_ref,
     gates_ref, u_ref) = outs
    xn = _rms(x, g).astype(BF16)
    q = _dot(xn, wq_ref[...])
    for h in range(N_HEADS):
        sl = slice(h * LANES, (h + 1) * LANES)
        q_ref[:, sl] = (_rope_chunk(q[:, sl], cos, sin) * Q_SCALE).astype(BF16)
    kv = _dot(xn, wkv_ref[...])
    n_sel = ksb_ref.shape[1] - KV_HALF
    blk = lax.broadcasted_iota(jnp.int32, (x.shape[0], n_sel), 1)
    for j, t_ref in enumerate((kvct_ref, kvst_ref, kvwt_ref)):
        k = _rope_chunk(kv[:, j * KV_DIM:j * KV_DIM + KV_HALF], cos, sin)
        v = kv[:, j * KV_DIM + KV_HALF:(j + 1) * KV_DIM]
        t_ref[0, 0] = k.T
        t_ref[0, 1] = v.T
        low = lax.broadcasted_iota(jnp.int32, v.shape, 1) < HEAD_DIM
        v0 = jnp.where(low, v, 1.0).astype(BF16)
        v1 = jnp.where(low, 1.0, v).astype(BF16)
        if j == 0:
            kvc_ref[:, :KV_HALF] = k
            kvc_ref[:, KV_HALF:] = v
        elif j == 1:
            ksb_ref[:, :KV_HALF] = k.astype(BF16)
            ksb_ref[:, KV_HALF:] = jnp.where(blk == pos // L_SEL, 1.0, 0.0).astype(BF16)
            vs0_ref[...] = v0
            vs1_ref[...] = v1
        else:
            kwb_ref[...] = k.astype(BF16)
            vw0_ref[...] = v0
            vw1_ref[...] = v1
    gates_ref[...] = jax.nn.sigmoid(_dot(xn, wg_ref[...]))
    u = _dot(xn, wu_ref[...])
    u_ref[...] = u
    return u


def _inproj_prompt_kernel(x_ref, xh_ref, cos_ref, sin_ref, g_ref, wq_ref, wkv_ref, wg_ref, wu_ref, *outs,
                          tiles_per_seq):
    d_ref = outs[-1]
    i = pl.program_id(0)
    tm = x_ref.shape[0]
    g = g_ref[...]
    seq_tile = i % tiles_per_seq
    pos = seq_tile * tm + lax.broadcasted_iota(jnp.int32, (tm, 1), 0)
    u = _project(x_ref[...], g, cos_ref[...], sin_ref[...], pos, wq_ref, wkv_ref, wg_ref, wu_ref, outs[:-1])
    uh = _dot(_rms(xh_ref[...], g).astype(BF16), wu_ref[...])
    uh = jnp.where(seq_tile == 0, 0.0, uh)
    ext = jnp.concatenate([uh, u], axis=0)
    for gi, w in enumerate(POOL_WINDOWS):
        sl = slice(gi * POOL_GROUP_DIM, (gi + 1) * POOL_GROUP_DIM)
        s = ext[:, sl]
        k = 1
        while k < w:
            s = s + pltpu.roll(s, k, 0)
            k *= 2
        cnt = jnp.minimum(w, pos + 1).astype(F32)
        d_ref[:, sl] = (s[POOL_HALO:, :] / cnt - u[:, sl]).astype(BF16)


def _inproj_sample_kernel(x_ref, sp_ref, cos_ref, sin_ref, g_ref, wq_ref, wkv_ref, wg_ref, wu_ref, *outs,
                          past_len):
    d_ref = outs[-1]
    pos = jnp.full((x_ref.shape[0], 1), past_len, jnp.int32)
    u = _project(x_ref[...], g_ref[...], cos_ref[...], sin_ref[...], pos, wq_ref, wkv_ref, wg_ref, wu_ref,
                 outs[:-1])
    for gi, w in enumerate(POOL_WINDOWS):
        sl = slice(gi * POOL_GROUP_DIM, (gi + 1) * POOL_GROUP_DIM)
        s = u[:, sl]
        for k in range(1, w):
            s = s + sp_ref[POOL_BUF - k][:, sl]
        cnt = float(min(w, past_len + 1))
        d_ref[:, sl] = (s / cnt - u[:, sl]).astype(BF16)


def _inproj_outs(n_seq, rows, tm, n_sel, row_map, seq_tile):
    t = n_seq * rows
    shapes, specs = [], []

    def add(shape, dtype, block, index_map):
        shapes.append(jax.ShapeDtypeStruct(shape, dtype))
        specs.append(pl.BlockSpec(block, index_map))

    t_map4 = lambda i: (seq_tile(i)[0], 0, 0, seq_tile(i)[1])
    add((t, QPAD_DIM), BF16, (tm, QPAD_DIM), row_map)
    add((t, KV_DIM), F32, (tm, KV_DIM), row_map)
    for _ in range(3):
        add((n_seq, 2, KV_HALF, rows), F32, (1, 2, KV_HALF, tm), t_map4)
    add((t, KV_HALF + n_sel), BF16, (tm, KV_HALF + n_sel), row_map)
    for _ in range(5):
        add((t, KV_HALF), BF16, (tm, KV_HALF), row_map)
    add((t, LANES), F32, (tm, LANES), row_map)
    add((t, POOL_DIM), F32, (tm, POOL_DIM), row_map)
    add((t, POOL_DIM), BF16, (tm, POOL_DIM), row_map)
    return tuple(shapes), tuple(specs)


def _weight_specs(shapes):
    return [pl.BlockSpec(s, lambda i, _n=len(s): (0,) * _n) for s in shapes]


def _inproj_prompt(x, cos, sin, g_mix, wq, wkv, wg, wu, batch, seq, tm):
    tiles_per_seq = seq // tm
    halo_blocks = tm // POOL_HALO
    in_specs = [
        pl.BlockSpec((tm, D_MODEL), lambda i: (i, 0)),
        pl.BlockSpec((POOL_HALO, D_MODEL), lambda i: (jnp.maximum(i * halo_blocks - 1, 0), 0)),
        pl.BlockSpec((tm, LANES), lambda i: (i % tiles_per_seq, 0)),
        pl.BlockSpec((tm, LANES), lambda i: (i % tiles_per_seq, 0)),
    ] + _weight_specs([g_mix.shape, wq.shape, wkv.shape, wg.shape, wu.shape])
    out_shape, out_specs = _inproj_outs(batch, seq, tm, seq // L_SEL, lambda i: (i, 0),
                                        lambda i: (i // tiles_per_seq, i % tiles_per_seq))
    return pl.pallas_call(
        functools.partial(_inproj_prompt_kernel, tiles_per_seq=tiles_per_seq),
        grid=(batch * tiles_per_seq,), in_specs=in_specs, out_specs=out_specs,
        out_shape=out_shape, compiler_params=_params(1), name="inproj_prompt",
    )(x, x, cos, sin, g_mix, wq, wkv, wg, wu)


def _inproj_sample(x, sp_t, cos, sin, g_mix, wq, wkv, wg, wu, past_len):
    t = x.shape[0]
    in_specs = [
        pl.BlockSpec((t, D_MODEL), lambda i: (0, 0)),
        pl.BlockSpec(sp_t.shape, lambda i: (0, 0, 0)),
        pl.BlockSpec((t, LANES), lambda i: (0, 0)),
        pl.BlockSpec((t, LANES), lambda i: (0, 0)),
    ] + _weight_specs([g_mix.shape, wq.shape, wkv.shape, wg.shape, wu.shape])
    out_shape, out_specs = _inproj_outs(1, t, t, LANES, lambda i: (0, 0), lambda i: (0, 0))
    return pl.pallas_call(
        functools.partial(_inproj_sample_kernel, past_len=past_len),
        grid=(1,), in_specs=in_specs, out_specs=out_specs,
        out_shape=out_shape, compiler_params=_params(1), name="inproj_sample",
    )(x, sp_t, cos, sin, g_mix, wq, wkv, wg, wu)


def _compress_rows(x_refs, pe_ref, w1_ref, w2_ref, o_ref):
    nb = o_ref.shape[0]
    for c, x_ref in enumerate(x_refs):
        acc = jnp.zeros((nb, w1_ref.shape[3]), F32)
        for l in range(L_CMP):
            xl = x_ref[pl.ds(l, nb, stride=L_CMP), :] + pe_ref[c, l]
            acc = acc + _dot(xl.astype(BF16), w1_ref[c, l])
        hid = acc * jax.nn.sigmoid(acc)
        o_ref[:, c * KV_HALF:(c + 1) * KV_HALF] = _dot(hid.astype(BF16), w2_ref[c]).astype(o_ref.dtype)


def _compress_kernel(xk_ref, xv_ref, pe_ref, w1_ref, w2_ref, o_ref):
    _compress_rows((xk_ref, xv_ref), pe_ref, w1_ref, w2_ref, o_ref)


def _compress(rows, pe, w1, w2, nb_tile):
    n_blocks = rows.shape[0] // L_CMP
    nb_tile = int(np.gcd(nb_tile, n_blocks))
    return pl.pallas_call(
        _compress_kernel,
        grid=(n_blocks // nb_tile,),
        in_specs=[pl.BlockSpec((nb_tile * L_CMP, KV_HALF), lambda i: (i, 0)),
                  pl.BlockSpec((nb_tile * L_CMP, KV_HALF), lambda i: (i, 1))]
        + _weight_specs([pe.shape, w1.shape, w2.shape]),
        out_specs=pl.BlockSpec((nb_tile, KV_DIM), lambda i: (i, 0)),
        out_shape=jax.ShapeDtypeStruct((n_blocks, KV_DIM), BF16),
        compiler_params=_params(1), name="compress",
    )(rows, rows, pe, w1, w2)


def _compress_pages_kernel(pt_ref, cache_ref, pet_ref, perm_ref, w1_ref, w2_ref, o_ref, buf_ref, sem_ref, xs_ref):
    b = pl.program_id(0)
    n_b = pl.num_programs(0)
    n_pages = buf_ref.shape[1]
    n_pairs = n_pages // 2
    nb = o_ref.shape[0]
    perm = perm_ref[...]

    def page_copy(bb, slot, j):
        return pltpu.make_async_copy(cache_ref.at[pt_ref[bb * n_pages + j]], buf_ref.at[slot, j], sem_ref.at[slot])

    @pl.when(b == 0)
    def _():
        for j in range(n_pages):
            page_copy(0, 0, j).start()

    @pl.when(b + 1 < n_b)
    def _():
        for j in range(n_pages):
            page_copy(b + 1, (b + 1) % 2, j).start()

    slot = b % 2
    for j in range(n_pages):
        page_copy(b, slot, j).wait()

    unroll = int(np.gcd(n_pairs, PAIR_UNROLL))

    def body(it, carry):
        for u in range(unroll):
            pp = it * unroll + u
            for c in range(2):
                xt = jnp.concatenate([buf_ref[slot, 2 * pp, c], buf_ref[slot, 2 * pp + 1, c]], axis=1) + pet_ref[c]
                xs_ref[c, pp] = _dot_nt(perm, xt.astype(BF16))
        return carry

    lax.fori_loop(0, n_pairs // unroll, body, 0)
    rows_per_l = 2 * LANES // L_CMP
    for c in range(2):
        acc = jnp.zeros((nb, w1_ref.shape[3]), F32)
        for l in range(L_CMP):
            xl = xs_ref[c, :, l * rows_per_l:(l + 1) * rows_per_l, :].reshape(nb, KV_HALF)
            acc = acc + _dot(xl.astype(BF16), w1_ref[c, l])
        hid = acc * jax.nn.sigmoid(acc)
        o_ref[:, c * KV_HALF:(c + 1) * KV_HALF] = _dot(hid.astype(BF16), w2_ref[c]).astype(o_ref.dtype)


def _compress_pages(cache_t, page_table, pe, w1, w2):
    _, _, _, page = cache_t.shape
    n_seq, n_pages = page_table.shape
    assert page == LANES and n_pages % 2 == 0
    per_page = page // L_CMP
    pet = jnp.tile(pe[:, :, 0, :].transpose(0, 2, 1), (1, 1, 2 * per_page))
    src = np.arange(2 * page)
    dst = (src % L_CMP) * (2 * per_page) + src // L_CMP
    perm = np.zeros((2 * page, 2 * page), np.float32)
    perm[dst, src] = 1.0
    perm = jnp.asarray(perm, BF16)
    fixed = lambda shape: pl.BlockSpec(shape, lambda b, pt, _n=len(shape): (0,) * _n)
    grid_spec = pltpu.PrefetchScalarGridSpec(
        num_scalar_prefetch=1, grid=(n_seq,),
        in_specs=[pl.BlockSpec(memory_space=pl.ANY)] + [fixed(a.shape) for a in (pet, perm, w1, w2)],
        out_specs=pl.BlockSpec((n_pages * per_page, KV_DIM), lambda b, pt: (b, 0)),
        scratch_shapes=[pltpu.VMEM((2, n_pages) + cache_t.shape[1:], F32), pltpu.SemaphoreType.DMA((2,)),
                        pltpu.VMEM((2, n_pages // 2, 2 * page, KV_HALF), F32)],
    )
    return pl.pallas_call(
        _compress_pages_kernel, grid_spec=grid_spec,
        out_shape=jax.ShapeDtypeStruct((n_seq * n_pages * per_page, KV_DIM), BF16),
        compiler_params=_params(1), name="compress_pages",
    )(page_table.reshape(-1), cache_t, pet, perm, w1, w2)


def _stack_heads(q, g):
    return jnp.concatenate([q[:, (GROUP * g + r) * LANES:(GROUP * g + r + 1) * LANES] for r in range(GROUP)], axis=0)


def _masked_softmax(s, mask):
    s = jnp.where(mask, s, NEG)
    e = jnp.exp2(s - jnp.max(s, axis=-1, keepdims=True))
    return e / jnp.sum(e, axis=-1, keepdims=True) * mask.astype(F32)


def _assemble_heads(o_heads):
    chunks = []
    for c in range(N_HEADS // 2):
        g = (2 * c) // GROUP
        a, b = o_heads[2 * c], o_heads[2 * c + 1]
        if g == 1:
            a = pltpu.roll(a, HEAD_DIM, 1)
        else:
            b = pltpu.roll(b, HEAD_DIM, 1)
        lane = lax.broadcasted_iota(jnp.int32, a.shape, 1)
        chunks.append(jnp.where(lane < HEAD_DIM, a, b))
    return jnp.concatenate(chunks, axis=1)


def _select_blocks(score, n_pick):
    n_blk = score.shape[1]
    lane = lax.broadcasted_iota(jnp.int32, score.shape, 1)
    work = score
    picks = []
    for _ in range(n_pick):
        m = jnp.max(work, axis=-1, keepdims=True)
        idx = jnp.min(jnp.where(work == m, lane, n_blk), axis=-1, keepdims=True)
        picks.append(idx)
        work = jnp.where(lane == idx, -jnp.inf, work)
    return picks


def _select_blocks_t(score_t, n_pick):
    n_blk = score_t.shape[0]
    blk = lax.broadcasted_iota(jnp.int32, score_t.shape, 0)
    work = score_t
    sel = jnp.zeros(score_t.shape, F32)
    for _ in range(n_pick):
        m = jnp.max(work, axis=0, keepdims=True)
        idx = jnp.min(jnp.where(work == m, blk, n_blk), axis=0, keepdims=True)
        hit = blk == idx
        sel = jnp.where(hit, 1.0, sel)
        work = jnp.where(hit, -jnp.inf, work)
    return sel


def _cmp_attention(qg, kc, vc, pos_rows, n_sel):
    s = _dot_nt(qg, kc)
    j = lax.broadcasted_iota(jnp.int32, s.shape, 1)
    n = 2 * (j % n_sel) + j // n_sel
    mask = (n * L_CMP + (L_CMP - 1)) <= pos_rows
    p = _masked_softmax(s, mask)
    return _dot(p.astype(BF16), vc), p


def _cmp_topk_kernel(q_ref, c_ref, gates_ref, oc_ref, msel_ref, *, n_sel, n_top):
    i = pl.program_id(1)
    tq = q_ref.shape[0]
    q = q_ref[...]
    comp = c_ref[0]
    kc, vc = comp[:, :LANES], comp[:, LANES:]
    gates = gates_ref[...]
    pos = i * tq + lax.broadcasted_iota(jnp.int32, (tq, 1), 0)
    pos_rows = jnp.concatenate([pos] * GROUP, axis=0)
    blk = lax.broadcasted_iota(jnp.int32, (n_sel, tq), 0)
    cur = (i * tq + lax.broadcasted_iota(jnp.int32, (1, tq), 1)) // L_SEL
    forced = (blk == 0) | (blk == cur) | (blk == cur - 1)
    o_heads = []
    for g in range(N_KV_HEADS):
        o, p = _cmp_attention(_stack_heads(q, g), kc, vc, pos_rows, n_sel)
        imp = p[0:tq]
        for r in range(1, GROUP):
            imp = imp + p[r * tq:(r + 1) * tq]
        imp = (imp[:, :n_sel] + imp[:, n_sel:]).T
        score = jnp.where(forced, FORCED, jnp.where(blk < cur, imp, NEG))
        sel = jnp.where(score > NEG / 2, _select_blocks_t(score, n_top), 0.0)
        msel_ref[:, g * n_sel:(g + 1) * n_sel] = ((1.0 - sel) * NEG).T.astype(BF16)
        for r in range(GROUP):
            h = GROUP * g + r
            o_heads.append(o[r * tq:(r + 1) * tq] * gates[:, h:h + 1])
    oc_ref[...] = _assemble_heads(o_heads)


def _cmp_topk_prompt(q, comp, gates, batch, seq, tq):
    n_sel = seq // L_SEL
    n_c = comp.shape[1]
    tiles = seq // tq
    row = lambda b, i: (b * tiles + i, 0)
    return pl.pallas_call(
        functools.partial(_cmp_topk_kernel, n_sel=n_sel, n_top=min(TOP_N, n_sel)),
        grid=(batch, tiles),
        in_specs=[pl.BlockSpec((tq, QPAD_DIM), row),
                  pl.BlockSpec((1, n_c, KV_DIM), lambda b, i: (b, 0, 0)),
                  pl.BlockSpec((tq, LANES), row)],
        out_specs=(pl.BlockSpec((tq, NSA_DIM), row), pl.BlockSpec((tq, N_KV_HEADS * n_sel), row)),
        out_shape=(jax.ShapeDtypeStruct((batch * seq, NSA_DIM), F32),
                   jax.ShapeDtypeStruct((batch * seq, N_KV_HEADS * n_sel), BF16)),
        compiler_params=_params(2), name="cmp_topk_prompt",
    )(q, comp, gates)


def _flash_update(carry, q, k, v, bias):
    m, acc = carry
    s = _dot_nt(q, k)
    if bias is not None:
        s = s + bias
    m_new = jnp.maximum(m, jnp.max(s, axis=-1, keepdims=True))
    p = jnp.exp2(s - m_new).astype(BF16)
    acc = jnp.exp2(m - m_new) * acc + _dot(p, v)
    return m_new, acc


def _sel_win_kernel(q_ref, ks_ref, vs0_ref, vs1_ref, kw_ref, vw0_ref, vw1_ref, msel_ref, gates_ref, oc_ref, o_ref,
                    *, n_sel):
    i = pl.program_id(1)
    tq = q_ref.shape[0]
    hpc = HEADS_PER_CHAIN
    n_chain = N_HEADS // hpc
    rows = hpc * tq
    q = q_ref[...]
    gates = gates_ref[...]
    vs_refs, vw_refs = (vs0_ref, vs1_ref), (vw0_ref, vw1_ref)
    pos = i * tq + lax.broadcasted_iota(jnp.int32, (tq, 1), 0)
    key = lax.broadcasted_iota(jnp.int32, (1, tq), 1)
    d0 = pl.multiple_of(i * tq, tq)
    p0 = pl.multiple_of(jnp.maximum(i - 1, 0) * tq, tq)
    bias_d = jnp.concatenate([jnp.where(i * tq + key <= pos, 0.0, NEG)] * hpc, axis=0)
    in_win = (pos - ((i - 1) * tq + key) < WINDOW) & (i > 0)
    bias_p = jnp.concatenate([jnp.where(in_win, 0.0, NEG)] * hpc, axis=0)
    groups = [(c * hpc) // GROUP for c in range(n_chain)]
    qcs, qas = [], []
    for c in range(n_chain):
        g = groups[c]
        qc = jnp.concatenate([q[:, h * LANES:(h + 1) * LANES] for h in range(c * hpc, (c + 1) * hpc)], axis=0)
        msel = jnp.concatenate([msel_ref[:, g * n_sel:(g + 1) * n_sel]] * hpc, axis=0)
        qcs.append(qc)
        qas.append(jnp.concatenate([qc, msel], axis=1))
    init = (jnp.full((rows, 1), NEG, F32), jnp.zeros((rows, LANES), F32))

    def tiles(k_ref, v_refs, start):
        return k_ref[pl.ds(start, tq), :], [v_ref[pl.ds(start, tq), :] for v_ref in v_refs]

    kd, vd = tiles(ks_ref, vs_refs, d0)
    sel = tuple(_flash_update(init, qas[c], kd, vd[groups[c]], bias_d) for c in range(n_chain))

    def body(kt, carries):
        k, v = tiles(ks_ref, vs_refs, pl.multiple_of(kt * tq, tq))
        return tuple(_flash_update(carries[c], qas[c], k, v[groups[c]], None) for c in range(n_chain))

    sel = lax.fori_loop(0, i, body, sel)
    kd, vd = tiles(kw_ref, vw_refs, d0)
    kp, vp = tiles(kw_ref, vw_refs, p0)
    o_heads = []
    for c in range(n_chain):
        g = groups[c]
        win = _flash_update(_flash_update(init, qcs[c], kd, vd[g], bias_d), qcs[c], kp, vp[g], bias_p)
        den = (1 - g) * HEAD_DIM
        o_s = sel[c][1] / sel[c][1][:, den:den + 1]
        o_w = win[1] / win[1][:, den:den + 1]
        for r in range(hpc):
            h = c * hpc + r
            rs = slice(r * tq, (r + 1) * tq)
            o_heads.append(o_s[rs] * gates[:, N_HEADS + h:N_HEADS + h + 1]
                           + o_w[rs] * gates[:, 2 * N_HEADS + h:2 * N_HEADS + h + 1])
    o_ref[...] = (_assemble_heads(o_heads) + oc_ref[...]).astype(o_ref.dtype)


def _sel_win_prompt(q, ksb, vs0, vs1, kwb, vw0, vw1, msel, gates, oc, batch, seq, tq):
    assert tq == WINDOW
    n_sel = seq // L_SEL
    tiles = seq // tq
    row = lambda b, i: (b * tiles + i, 0)
    whole = lambda b, i: (b, 0)
    return pl.pallas_call(
        functools.partial(_sel_win_kernel, n_sel=n_sel),
        grid=(batch, tiles),
        in_specs=[pl.BlockSpec((tq, QPAD_DIM), row),
                  pl.BlockSpec((seq, KV_HALF + n_sel), whole, pipeline_mode=pl.Buffered(1))]
        + [pl.BlockSpec((seq, KV_HALF), whole, pipeline_mode=pl.Buffered(1))] * 5
        + [pl.BlockSpec((tq, N_KV_HEADS * n_sel), row),
           pl.BlockSpec((tq, LANES), row),
           pl.BlockSpec((tq, NSA_DIM), row)],
        out_specs=pl.BlockSpec((tq, NSA_DIM), row),
        out_shape=jax.ShapeDtypeStruct((batch * seq, NSA_DIM), BF16),
        compiler_params=_params(2), name="sel_win_prompt",
    )(q, ksb, vs0, vs1, kwb, vw0, vw1, msel, gates, oc)


def _sample_cmp_kernel(q_ref, c_ref, gates_ref, oc_ref, idx_ref, *, n_blk, n_pick, past_len):
    n_sb = q_ref.shape[0]
    pos_rows = jnp.full((N_HEADS, 1), past_len, jnp.int32)
    imps = []
    for s in range(n_sb):
        comp = c_ref[s]
        o, p = _cmp_attention(q_ref[s], comp[:, :LANES], comp[:, LANES:], pos_rows, n_blk)
        oc_ref[s] = o * gates_ref[s][:, 0:1]
        imps += [jnp.sum(p[GROUP * g:GROUP * (g + 1)], axis=0, keepdims=True) for g in range(N_KV_HEADS)]
    imp = jnp.concatenate(imps, axis=0)
    imp = imp[:, :n_blk] + imp[:, n_blk:]
    cur = past_len // L_SEL
    blk = lax.broadcasted_iota(jnp.int32, imp.shape, 1)
    forced = (blk == 0) | (blk == cur) | (blk == cur - 1)
    score = jnp.where(forced, FORCED, jnp.where(blk < cur, imp, NEG))
    lane = lax.broadcasted_iota(jnp.int32, idx_ref.shape, 1)
    out = jnp.zeros(idx_ref.shape, jnp.int32)
    for k, idx in enumerate(_select_blocks(score, n_pick)):
        out = jnp.where(lane == k, idx, out)
    idx_ref[...] = out


def _sample_cmp(q3, comp, gates3, n_pick, past_len, seqs_per_step):
    n_seq = q3.shape[0]
    n_c = comp.shape[1]
    n_sb = int(np.gcd(seqs_per_step, n_seq))
    blk3 = lambda b: (b, 0, 0)
    return pl.pallas_call(
        functools.partial(_sample_cmp_kernel, n_blk=n_c // 2, n_pick=n_pick, past_len=past_len),
        grid=(n_seq // n_sb,),
        in_specs=[pl.BlockSpec((n_sb, N_HEADS, LANES), blk3),
                  pl.BlockSpec((n_sb, n_c, KV_DIM), blk3),
                  pl.BlockSpec((n_sb, N_HEADS, LANES), blk3)],
        out_specs=(pl.BlockSpec((n_sb, N_HEADS, LANES), blk3),
                   pl.BlockSpec((n_sb * N_KV_HEADS, LANES), lambda b: (b, 0))),
        out_shape=(jax.ShapeDtypeStruct((n_seq, N_HEADS, LANES), F32),
                   jax.ShapeDtypeStruct((n_seq * N_KV_HEADS, LANES), jnp.int32)),
        compiler_params=_params(1), name="sample_cmp",
    )(q3, comp, gates3)


def _softmax_with_new(s, valid, s_new):
    s = jnp.where(valid, s, NEG)
    m = jnp.maximum(jnp.max(s, axis=-1, keepdims=True), s_new)
    e = jnp.exp2(s - m) * valid.astype(F32)
    e_new = jnp.exp2(s_new - m)
    den = jnp.sum(e, axis=-1, keepdims=True) + e_new
    return e / den, e_new / den


def _sample_sel_win_kernel(pg_ref, hf_ref, q_ref, win_ref, kvs_ref, kvw_ref, gates_ref, oc_ref, cache_ref,
                           o_ref, buf_ref, sem_ref, *, n_pick):
    b = pl.program_id(0)
    n_b = pl.num_programs(0)
    n_dma = N_KV_HEADS * n_pick
    page = buf_ref.shape[-1]

    def page_copy(bb, slot, j):
        return pltpu.make_async_copy(cache_ref.at[pg_ref[bb * n_dma + j]], buf_ref.at[slot, j], sem_ref.at[slot])

    @pl.when(b == 0)
    def _():
        for j in range(n_dma):
            page_copy(0, 0, j).start()

    @pl.when(b + 1 < n_b)
    def _():
        for j in range(n_dma):
            page_copy(b + 1, (b + 1) % 2, j).start()

    q = q_ref[0]
    qf = q.astype(F32)
    gates = gates_ref[0]
    row = lax.broadcasted_iota(jnp.int32, (N_HEADS, LANES), 0)

    def new_key(kv_ref):
        kv = kv_ref[0].astype(BF16).astype(F32)
        return jnp.sum(qf * kv[:, :KV_HALF], axis=-1, keepdims=True), kv[:, KV_HALF:]

    win = win_ref[0]
    kt, vt = win[0].astype(BF16), win[1].astype(BF16)
    s_new, v_new = new_key(kvw_ref)
    lane_w = lax.broadcasted_iota(jnp.int32, (N_HEADS, kt.shape[1]), 1)
    p, p_new = _softmax_with_new(_dot(q, kt), lane_w >= 1, s_new)
    o_w = _dot_nt(p.astype(BF16), vt) + p_new.astype(BF16).astype(F32) * v_new

    slot = b % 2
    for j in range(n_dma):
        page_copy(b, slot, j).wait()
    s_new, v_new = new_key(kvs_ref)
    lane_p = lax.broadcasted_iota(jnp.int32, (N_HEADS, page), 1)
    o_s = jnp.zeros((N_HEADS, LANES), F32)
    for g in range(N_KV_HEADS):
        kts, vts, valids = [], [], []
        for k in range(n_pick):
            j = g * n_pick + k
            pg = buf_ref[slot, j]
            kts.append(pg[0].astype(BF16))
            vts.append(pg[1].astype(BF16))
            half = hf_ref[b * n_dma + j]
            valids.append((lane_p // L_SEL) == half)
        kt_all = jnp.concatenate(kts, axis=1)
        vt_all = jnp.concatenate(vts, axis=1)
        p, p_new = _softmax_with_new(_dot(q, kt_all), jnp.concatenate(valids, axis=1), s_new)
        o_g = _dot_nt(p.astype(BF16), vt_all) + p_new.astype(BF16).astype(F32) * v_new
        o_s = jnp.where(row // GROUP == g, o_g, o_s)
    o_ref[0] = oc_ref[0] + o_s * gates[:, 1:2] + o_w * gates[:, 2:3]


def _sample_sel_win(pages, halves, q3, win_t, kvs, kvw, gates3, oc, cache_t, n_pick):
    n_seq = q3.shape[0]
    n_dma = N_KV_HEADS * n_pick
    blk3 = lambda b, pg, hf: (b, 0, 0)
    grid_spec = pltpu.PrefetchScalarGridSpec(
        num_scalar_prefetch=2, grid=(n_seq,),
        in_specs=[pl.BlockSpec((1, N_HEADS, LANES), blk3),
                  pl.BlockSpec((1,) + win_t.shape[1:], lambda b, pg, hf: (b, 0, 0, 0)),
                  pl.BlockSpec((1, 1, KV_DIM), blk3),
                  pl.BlockSpec((1, 1, KV_DIM), blk3),
                  pl.BlockSpec((1, N_HEADS, LANES), blk3),
                  pl.BlockSpec((1, N_HEADS, LANES), blk3),
                  pl.BlockSpec(memory_space=pl.ANY)],
        out_specs=pl.BlockSpec((1, N_HEADS, LANES), blk3),
        scratch_shapes=[pltpu.VMEM((2, n_dma) + cache_t.shape[1:], F32), pltpu.SemaphoreType.DMA((2,))],
    )
    return pl.pallas_call(
        functools.partial(_sample_sel_win_kernel, n_pick=n_pick),
        grid_spec=grid_spec,
        out_shape=jax.ShapeDtypeStruct((n_seq, N_HEADS, LANES), F32),
        compiler_params=_params(1), name="sample_sel_win",
    )(pages, halves, q3, win_t, kvs, kvw, gates3, oc, cache_t)


def _tail_kernel(x_ref, attn_ref, d_ref, gmix_ref, wgm_ref, wpool_ref, pscale_ref,
                 wbn_ref, wbp_ref, wout_ref, gffn_ref, wr_ref, br_ref, h_ref, xn2_ref, route_ref, cout_ref, run_ref,
                 *, n_experts):
    i = pl.program_id(0)
    x = x_ref[...]
    attn = attn_ref[...]
    d = d_ref[...]
    tm = x.shape[0]
    xn = _rms(x, gmix_ref[...]).astype(BF16)
    gm = jax.nn.sigmoid(_dot(xn, wgm_ref[...]))
    pool = jnp.concatenate(
        [_dot(d[:, gi * POOL_GROUP_DIM:(gi + 1) * POOL_GROUP_DIM], wpool_ref[gi]) for gi in range(len(POOL_WINDOWS))],
        axis=1) * pscale_ref[...]
    merged = (gm[:, :D_MODEL] * _dot(attn, wbn_ref[...])
              + gm[:, D_MODEL:] * _dot(pool.astype(BF16), wbp_ref[...]))
    h = x + _dot(merged.astype(BF16), wout_ref[...])
    h_ref[...] = h
    xn2 = _rms(h, gffn_ref[...]).astype(BF16)
    xn2_ref[...] = xn2
    logits = _dot(xn2, wr_ref[...]) + br_ref[...]
    lane = lax.broadcasted_iota(jnp.int32, (tm, LANES), 1)
    work = jnp.where(lane < n_experts, logits, -jnp.inf)
    vals, onehots = [], []
    for _ in range(TOP_K):
        m = jnp.max(work, axis=-1, keepdims=True)
        idx = jnp.min(jnp.where(work == m, lane, LANES), axis=-1, keepdims=True)
        vals.append(m)
        onehots.append(lane == idx)
        work = jnp.where(lane == idx, -jnp.inf, work)
    es = [jnp.exp(v - vals[0]) for v in vals]
    tot = es[0] + es[1] + es[2] + es[3]

    @pl.when(i == 0)
    def _():
        run_ref[...] = jnp.zeros(run_ref.shape, F32)

    chosen = jnp.zeros((tm, LANES), F32)
    for k in range(TOP_K):
        chosen = jnp.where(onehots[k], 1.0, chosen)
    earlier = lax.broadcasted_iota(jnp.int32, (tm, tm), 0) > lax.broadcasted_iota(jnp.int32, (tm, tm), 1)
    before = run_ref[...] + _dot(jnp.where(earlier, 1.0, 0.0).astype(BF16), chosen.astype(BF16))
    route = jnp.zeros((tm, LANES), F32)
    lane_f = lane.astype(F32)
    for k in range(TOP_K):
        e_k = jnp.sum(jnp.where(onehots[k], lane_f, 0.0), axis=-1, keepdims=True)
        r_k = jnp.sum(jnp.where(onehots[k], before, 0.0), axis=-1, keepdims=True)
        route = jnp.where(lane == k, e_k, route)
        route = jnp.where(lane == TOP_K + k, es[k] / tot, route)
        route = jnp.where(lane == 2 * TOP_K + k, r_k, route)
    route_ref[...] = route
    run_ref[...] = run_ref[...] + jnp.sum(chosen, axis=0, keepdims=True)
    cout_ref[...] = run_ref[...]


def _tail(x, attn, d, weights, n_experts, tm):
    t = x.shape[0]
    row = lambda i: (i, 0)
    fixed = lambda i: (0, 0)
    return pl.pallas_call(
        functools.partial(_tail_kernel, n_experts=n_experts),
        grid=(t // tm,),
        in_specs=[pl.BlockSpec((tm, D_MODEL), row), pl.BlockSpec((tm, NSA_DIM), row),
                  pl.BlockSpec((tm, POOL_DIM), row)]
        + _weight_specs([w.shape for w in weights]),
        out_specs=(pl.BlockSpec((tm, D_MODEL), row), pl.BlockSpec((tm, D_MODEL), row),
                   pl.BlockSpec((tm, LANES), row), pl.BlockSpec((1, LANES), fixed)),
        out_shape=(jax.ShapeDtypeStruct((t, D_MODEL), F32), jax.ShapeDtypeStruct((t, D_MODEL), BF16),
                   jax.ShapeDtypeStruct((t, LANES), F32), jax.ShapeDtypeStruct((1, LANES), F32)),
        scratch_shapes=[pltpu.VMEM((1, LANES), F32)],
        compiler_params=_params(1), name="tail",
    )(x, attn, d, *weights)


def _moe_kernel(te_ref, tv_ref, tf_ref, x_ref, wg_ref, bg_ref, wu_ref, bu_ref, wd_ref, bd_ref, y_ref,
                wgb_ref, wub_ref, wdb_ref):
    t = pl.program_id(0)

    @pl.when(tf_ref[t] > 0)
    def _():
        wgb_ref[...] = wg_ref[0].astype(BF16)
        wub_ref[...] = wu_ref[0].astype(BF16)
        wdb_ref[...] = wd_ref[0].astype(BF16)

    @pl.when(tv_ref[t] > 0)
    def _():
        x = x_ref[...]
        gate = jnp.minimum(_dot(x, wgb_ref[...]) + bg_ref[0], SWIGLU_LIMIT)
        up = jnp.clip(_dot(x, wub_ref[...]) + bu_ref[0], -SWIGLU_LIMIT, SWIGLU_LIMIT)
        hid = (up + 1.0) * gate * jax.nn.sigmoid(SWIGLU_ALPHA * gate)
        y_ref[...] = (_dot(hid.astype(BF16), wdb_ref[...]) + bd_ref[0]).astype(y_ref.dtype)

    @pl.when(tv_ref[t] == 0)
    def _():
        y_ref[...] = jnp.zeros(y_ref.shape, y_ref.dtype)


def _moe(tile_e, tile_v, tile_f, xs, wg, bg, wu, bu, wd, bd, tme):
    p_pad = xs.shape[0]
    d_e = wg.shape[2]
    wspec = lambda shape: pl.BlockSpec((1,) + shape, lambda t, te, tv, tf: (te[t], 0, 0))
    grid_spec = pltpu.PrefetchScalarGridSpec(
        num_scalar_prefetch=3, grid=(p_pad // tme,),
        in_specs=[pl.BlockSpec((tme, D_MODEL), lambda t, te, tv, tf: (t, 0)),
                  wspec((D_MODEL, d_e)), wspec((1, d_e)), wspec((D_MODEL, d_e)), wspec((1, d_e)),
                  wspec((d_e, D_MODEL)), wspec((1, D_MODEL))],
        out_specs=pl.BlockSpec((tme, D_MODEL), lambda t, te, tv, tf: (t, 0)),
        scratch_shapes=[pltpu.VMEM((D_MODEL, d_e), BF16), pltpu.VMEM((D_MODEL, d_e), BF16),
                        pltpu.VMEM((d_e, D_MODEL), BF16)],
    )
    return pl.pallas_call(
        _moe_kernel, grid_spec=grid_spec,
        out_shape=jax.ShapeDtypeStruct((p_pad, D_MODEL), BF16),
        compiler_params=_params(1), name="moe_experts",
    )(tile_e, tile_v, tile_f, xs, wg, bg, wu, bu, wd, bd)


def _final_kernel(h_ref, y_ref, route_ref, g_ref, o_ref):
    route = route_ref[...]
    acc = h_ref[...]
    for k in range(TOP_K):
        acc = acc + y_ref[k].astype(F32) * route[:, TOP_K + k:TOP_K + k + 1]
    o_ref[...] = _rms(acc, g_ref[...])


def _final(h, yg, route, g_final, tm, first, n_tiles):
    src = lambda i: (i + first, 0)
    return pl.pallas_call(
        _final_kernel, grid=(n_tiles,),
        in_specs=[pl.BlockSpec((tm, D_MODEL), src), pl.BlockSpec((TOP_K, tm, D_MODEL), lambda i: (0, i + first, 0)),
                  pl.BlockSpec((tm, LANES), src), pl.BlockSpec((1, D_MODEL), lambda i: (0, 0))],
        out_specs=pl.BlockSpec((tm, D_MODEL), lambda i: (i, 0)),
        out_shape=jax.ShapeDtypeStruct((n_tiles * tm, D_MODEL), F32),
        compiler_params=_params(1), name="final_norm",
    )(h, yg, route, g_final)


def _rope_tables(pos):
    half = HEAD_DIM // 2
    inv = ROPE_THETA ** (-jnp.arange(half, dtype=F32) / half)
    ang = pos.astype(F32)[:, None] * inv[None, :]
    cos, sin = jnp.cos(ang), jnp.sin(ang)
    cos = jnp.concatenate([cos, cos] * (LANES // HEAD_DIM), axis=1)
    sin = jnp.concatenate([-sin, sin] * (LANES // HEAD_DIM), axis=1)
    return cos, sin


def _split_w_in(w_in):
    o_q = NSA_DIM
    o_g = o_q + 3 * KV_DIM
    o_u = o_g + 3 * N_HEADS
    o_m = o_u + POOL_DIM
    wq = w_in[:, :o_q].reshape(D_MODEL, N_HEADS, HEAD_DIM)
    zeros = jnp.zeros_like(wq)
    lo = jnp.concatenate([wq, zeros], axis=2)
    hi = jnp.concatenate([zeros, wq], axis=2)
    in_hi = (jnp.arange(N_HEADS) // GROUP == 1)[None, :, None]
    wq_pad = jnp.where(in_hi, hi, lo).reshape(D_MODEL, QPAD_DIM)
    wkv = w_in[:, o_q:o_g]
    wg = w_in[:, o_g:o_u].reshape(D_MODEL, N_HEADS, 3).transpose(0, 2, 1).reshape(D_MODEL, 3 * N_HEADS)
    wg = jnp.pad(wg, ((0, 0), (0, LANES - 3 * N_HEADS)))
    wu = w_in[:, o_u:o_m]
    wgm = w_in[:, o_m:]
    return [w.astype(BF16) for w in (wq_pad, wkv, wg, wu, wgm)]


def _compress_weights(cmp_pe, cmp_w1, cmp_w2):
    eye = jnp.eye(N_KV_HEADS, dtype=F32)
    w1 = jnp.einsum('cldh,gf->clgdfh', cmp_w1, eye).reshape(2, L_CMP, KV_HALF, N_KV_HEADS * CMP_HIDDEN)
    w2 = jnp.einsum('chd,gf->cghfd', cmp_w2, eye).reshape(2, N_KV_HEADS * CMP_HIDDEN, KV_HALF)
    pe = jnp.broadcast_to(cmp_pe.transpose(1, 0, 2)[:, :, None, :], (2, L_CMP, N_KV_HEADS, HEAD_DIM))
    return pe.reshape(2, L_CMP, 1, KV_HALF), w1.astype(BF16), w2.astype(BF16)


def _even_odd(comp, n_seq):
    n_c = comp.shape[0] // n_seq
    return comp.reshape(n_seq, n_c // 2, 2, KV_DIM).transpose(0, 2, 1, 3).reshape(n_seq, n_c, KV_DIM)


def _fix_sample_heads(o3):
    n_seq = o3.shape[0]
    o4 = o3.reshape(n_seq, N_KV_HEADS, GROUP, N_KV_HEADS, HEAD_DIM)
    return jnp.concatenate([o4[:, g, :, g, :] for g in range(N_KV_HEADS)], axis=1).reshape(n_seq, NSA_DIM)


def _gates_by_head(gates):
    g = gates[:, :3 * N_HEADS].reshape(-1, 3, N_HEADS).transpose(0, 2, 1)
    return jnp.pad(g, ((0, 0), (0, 0), (0, LANES - 3)))


def _key_minor(x):
    n, p = x.shape[:2]
    return x.transpose(0, 2, 3, 4, 1).reshape(n, 2, KV_HALF, p)


def _from_key_minor(xt):
    n, _, _, p = xt.shape
    return xt.reshape(n, 2, N_KV_HEADS, HEAD_DIM, p).transpose(0, 4, 1, 2, 3)


def _expert_order(route, counts, n_experts, tme):
    t = route.shape[0]
    n_pairs = t * TOP_K
    e = route[:, :TOP_K].astype(jnp.int32)
    rank = route[:, 2 * TOP_K:3 * TOP_K].astype(jnp.int32)
    counts = counts[0, :n_experts].astype(jnp.int32)
    padded = ((counts + tme - 1) // tme) * tme
    ends = jnp.cumsum(padded)
    off = ends - padded
    pad_before = off - (jnp.cumsum(counts) - counts)
    experts = jnp.arange(n_experts, dtype=jnp.int32)
    pair_slot = rank + jnp.sum(jnp.where(e[:, :, None] == experts, off, 0), axis=-1)
    n_tiles = -(-n_pairs // tme) + n_experts
    p_pad = n_tiles * tme
    tok = jnp.broadcast_to(jnp.arange(t, dtype=jnp.int32)[:, None], (t, TOP_K))
    _, tok_sorted = lax.sort((pair_slot.reshape(n_pairs), tok.reshape(n_pairs)), num_keys=1)
    slot = jnp.arange(p_pad, dtype=jnp.int32)
    slot_e = jnp.sum((slot[:, None] >= ends[None, :]).astype(jnp.int32), axis=-1)
    slot_e = jnp.minimum(slot_e, n_experts - 1)
    slot_pad = jnp.sum(jnp.where(slot_e[:, None] == experts, pad_before, 0), axis=-1)
    src_tok = tok_sorted[jnp.clip(slot - slot_pad, 0, n_pairs - 1)]
    tile_start = jnp.arange(n_tiles, dtype=jnp.int32) * tme
    tile_e = jnp.minimum(jnp.sum((tile_start[:, None] >= ends[None, :]).astype(jnp.int32), axis=-1), n_experts - 1)
    tile_v = (tile_start < ends[-1]).astype(jnp.int32)
    tile_f = jnp.concatenate([jnp.ones((1,), jnp.int32), (tile_e[1:] != tile_e[:-1]).astype(jnp.int32)])
    return pair_slot, src_tok, tile_e, tile_v, tile_f


def kernel(x_prompt, x_sample, cache_kv_cmp, cache_kv_sel, state_kv_win, state_pool, page_table, g_mix, w_in, cmp_pe, cmp_w1, cmp_w2, w_pool, pool_scale, w_br_nsa, w_br_pool, w_out, g_ffn, w_router, b_router, w_gate, b_gate, w_up, b_up, w_down, b_down, g_final):
    batch, seq, _ = x_prompt.shape
    n_seq, dec_seq, _ = x_sample.shape
    depth = g_mix.shape[0]
    assert depth == 1 and dec_seq == 1
    page_size = cache_kv_cmp.shape[2]
    n_pages = page_table.shape[1]
    past_len = n_pages * page_size
    n_experts = w_router.shape[2]
    t_p = batch * seq
    tm = min(ROW_TILE, seq)

    wq, wkv, wg, wu, wgm = _split_w_in(w_in[0])
    gmix = g_mix[0][None, :]
    pe, w1, w2 = _compress_weights(cmp_pe[0], cmp_w1[0], cmp_w2[0])
    cos_p, sin_p = _rope_tables(jnp.arange(seq, dtype=jnp.int32))
    cos_s, sin_s = _rope_tables(jnp.full((n_seq,), past_len, jnp.int32))
    wr = jnp.pad(w_router[0], ((0, 0), (0, LANES - n_experts))).astype(BF16)
    br = jnp.pad(b_router[0], (0, LANES - n_experts))[None, :]
    tail_w = [gmix, wgm, w_pool[0].astype(BF16), pool_scale[0][None, :], w_br_nsa[0].astype(BF16),
              w_br_pool[0].astype(BF16), w_out[0].astype(BF16), g_ffn[0][None, :], wr, br]

    xp = x_prompt.reshape(t_p, D_MODEL)
    (q_p, kvc_p, kvct_p, kvst_p, kvwt_p, ksb_p, vs0_p, vs1_p, kwb_p, vw0_p, vw1_p, gates_p, u_p, d_p) = _inproj_prompt(
        xp, cos_p, sin_p, gmix, wq, wkv, wg, wu, batch, seq, tm)
    comp_p = _even_odd(_compress(kvc_p, pe, w1, w2, CMP_BLOCK_TILE), batch)
    tq = min(CMP_Q_TILE, seq)
    oc_p, msel_p = _cmp_topk_prompt(q_p, comp_p, gates_p, batch, seq, tq)
    attn_p = _sel_win_prompt(q_p, ksb_p, vs0_p, vs1_p, kwb_p, vw0_p, vw1_p, msel_p, gates_p, oc_p, batch, seq, WINDOW)

    def route_rows(x, attn, d, tile, tme):
        h, xn2, route, counts = _tail(x, attn, d, tail_w, n_experts, tile)
        return (h, xn2, route, tile, tme) + _expert_order(route, counts, n_experts, tme)

    def expert_ffn(routed, x_sorted):
        h, _, route, tile, tme, pair_slot, _, tile_e, tile_v, tile_f = routed
        y_sorted = _moe(tile_e, tile_v, tile_f, x_sorted, w_gate[0], b_gate[0][:, None, :],
                        w_up[0], b_up[0][:, None, :], w_down[0], b_down[0][:, None, :], tme)
        return _final(h, y_sorted[pair_slot.T], route, g_final[None, :], tile, 0, h.shape[0] // tile)

    routed_p = route_rows(xp, attn_p, d_p, tm, MOE_TILE)
    src_tok_p = routed_p[6]
    xs, page_table, src_tok_p = lax.optimization_barrier((x_sample.reshape(n_seq, D_MODEL), page_table, src_tok_p))
    x_sorted_p = routed_p[1][src_tok_p]

    sp_t = state_pool[0].transpose(1, 0, 2)
    (q_s, kvc_s, _, kvst_s, kvwt_s, _, _, _, _, _, _, gates_s, u_s, d_s) = _inproj_sample(
        xs, sp_t, cos_s, sin_s, gmix, wq, wkv, wg, wu, past_len)
    kvs_s = kvst_s.reshape(KV_DIM, n_seq).T
    kvw_s = kvwt_s.reshape(KV_DIM, n_seq).T
    comp_s = _even_odd(_compress_pages(_key_minor(cache_kv_cmp[0]), page_table, pe, w1, w2), n_seq)
    q3 = q_s.reshape(n_seq, N_HEADS, LANES)
    gates3 = _gates_by_head(gates_s)
    n_blk = past_len // L_SEL
    n_pick = min(TOP_N, n_blk + 1) - 1
    oc_s, idx_s = _sample_cmp(q3, comp_s, gates3, n_pick, past_len, SAMPLE_SEQS_PER_STEP)
    top_idx = idx_s.reshape(n_seq, N_KV_HEADS, LANES)[:, :, :n_pick]
    sel_per_page = page_size // L_SEL
    pages_b = jnp.broadcast_to(page_table[:, None, :], (n_seq, N_KV_HEADS, n_pages))
    phys = jnp.take_along_axis(pages_b, top_idx // sel_per_page, axis=2).reshape(-1)
    halves = (top_idx % sel_per_page).reshape(-1)
    w_buf = state_kv_win.shape[2]
    assert w_buf == WINDOW
    win_t = _key_minor(state_kv_win[0])
    o3 = _sample_sel_win(phys, halves, q3, win_t, kvs_s[:, None, :], kvw_s[:, None, :], gates3, oc_s,
                         _key_minor(cache_kv_sel[0]), n_pick)
    attn_s = _fix_sample_heads(o3).astype(BF16)
    x_sorted_p, attn_s = lax.optimization_barrier((x_sorted_p, attn_s))
    y_p = expert_ffn(routed_p, x_sorted_p)
    routed_s = route_rows(xs, attn_s, d_s, n_seq, SAMPLE_MOE_TILE)
    y_s = expert_ffn(routed_s, routed_s[1][routed_s[6]])

    w_p = min(WINDOW, seq)
    kv_row = (2, N_KV_HEADS, HEAD_DIM)
    new_col = kvwt_s.reshape(2, KV_HALF, n_seq).transpose(2, 0, 1)[..., None]
    win_new_t = jnp.concatenate([win_t[..., 1:], new_col], axis=-1)
    return (
        y_p.reshape(batch, seq, D_MODEL),
        y_s.reshape(n_seq, 1, D_MODEL),
        _from_key_minor(kvct_p)[None],
        kvc_s.reshape((1, n_seq, 1) + kv_row),
        _from_key_minor(kvst_p)[None],
        kvs_s.reshape((1, n_seq, 1) + kv_row),
        _from_key_minor(kvwt_p[..., seq - w_p:])[None],
        _from_key_minor(win_new_t)[None],
        u_p.reshape(batch, seq, POOL_DIM)[None, :, seq - POOL_BUF:],
        jnp.concatenate([state_pool[0], u_s[:, None, :]], axis=1)[None, :, 1:],
    )
```

```python
import functools
import math

import jax
import jax.numpy as jnp
import numpy as np
from jax import lax
from jax.experimental import pallas as pl
from jax.experimental.pallas import tpu as pltpu

D_MODEL = 1024
N_HEADS = 8
N_KV_HEADS = 2
HEAD_DIM = 64
GROUP = N_HEADS // N_KV_HEADS
NSA_DIM = N_HEADS * HEAD_DIM
KV_DIM = 2 * N_KV_HEADS * HEAD_DIM
KV_HALF = KV_DIM // 2
L_CMP = 32
L_SEL = 64
CMP_HIDDEN = 2 * HEAD_DIM
TOP_N = 16
WINDOW = 512
ROPE_THETA = 10000.0
POOL_WINDOWS = (2, 4, 8, 16)
POOL_DIM = D_MODEL // 2
POOL_GROUP_DIM = POOL_DIM // len(POOL_WINDOWS)
POOL_BUF = max(POOL_WINDOWS) - 1
POOL_HALO = POOL_BUF + 1
TOP_K = 4
SWIGLU_LIMIT = 7.0
SWIGLU_ALPHA = 1.702
RMS_EPS = 1e-6
NEG = -1e30
FORCED = 1e9

LANES = 128
QPAD_DIM = N_HEADS * LANES
VMEM_LIMIT = 56 * 1024 * 1024
Q_SCALE = HEAD_DIM ** -0.5 * math.log2(math.e)
ROW_TILE = 512
CMP_Q_TILE = 512
CMP_BLOCK_TILE = 256
MOE_TILE = 512
SAMPLE_MOE_TILE = 128
SAMPLE_SEQS_PER_STEP = 16
PAIR_UNROLL = 8
HEADS_PER_CHAIN = GROUP

BF16 = jnp.bfloat16
F32 = jnp.float32


def _params(n_grid, vmem=VMEM_LIMIT):
    return pltpu.CompilerParams(dimension_semantics=("arbitrary",) * n_grid, vmem_limit_bytes=vmem)


def _rms(x, g):
    r = lax.rsqrt(jnp.mean(x * x, axis=-1, keepdims=True) + RMS_EPS)
    return x * r * g


def _dot(a, b):
    return jnp.dot(a, b, preferred_element_type=F32)


def _dot_nt(a, b):
    return lax.dot_general(a, b, (((1,), (1,)), ((), ())), preferred_element_type=F32)


def _rope_chunk(x, cos, sin_signed):
    lane = lax.broadcasted_iota(jnp.int32, x.shape, 1)
    first = (lane % HEAD_DIM) < (HEAD_DIM // 2)
    swapped = jnp.where(first, pltpu.roll(x, LANES - HEAD_DIM // 2, 1), pltpu.roll(x, HEAD_DIM // 2, 1))
    return x * cos + swapped * sin_signed


def _project(x, g, cos, sin, pos, wq_ref, wkv_ref, wg_ref, wu_ref, outs):
    (q_ref, kvc_ref, kvct_ref, kvst_ref, kvwt_ref, ksb_ref, vs0_ref, vs1_ref, kwb_ref, vw0_ref, vw1_ref,
     gates_ref, u_ref) = outs
    xn = _rms(x, g).astype(BF16)
    q = _dot(xn, wq_ref[...])
    for h in range(N_HEADS):
        sl = slice(h * LANES, (h + 1) * LANES)
        q_ref[:, sl] = (_rope_chunk(q[:, sl], cos, sin) * Q_SCALE).astype(BF16)
    kv = _dot(xn, wkv_ref[...])
    n_sel = ksb_ref.shape[1] - KV_HALF
    blk = lax.broadcasted_iota(jnp.int32, (x.shape[0], n_sel), 1)
    for j, t_ref in enumerate((kvct_ref, kvst_ref, kvwt_ref)):
        k = _rope_chunk(kv[:, j * KV_DIM:j * KV_DIM + KV_HALF], cos, sin)
        v = kv[:, j * KV_DIM + KV_HALF:(j + 1) * KV_DIM]
        t_ref[0, 0] = k.T
        t_ref[0, 1] = v.T
        low = lax.broadcasted_iota(jnp.int32, v.shape, 1) < HEAD_DIM
        v0 = jnp.where(low, v, 1.0).astype(BF16)
        v1 = jnp.where(low, 1.0, v).astype(BF16)
        if j == 0:
            kvc_ref[:, :KV_HALF] = k
            kvc_ref[:, KV_HALF:] = v
        elif j == 1:
            ksb_ref[:, :KV_HALF] = k.astype(BF16)
            ksb_ref[:, KV_HALF:] = jnp.where(blk == pos // L_SEL, 1.0, 0.0).astype(BF16)
            vs0_ref[...] = v0
            vs1_ref[...] = v1
        else:
            kwb_ref[...] = k.astype(BF16)
            vw0_ref[...] = v0
            vw1_ref[...] = v1
    gates_ref[...] = jax.nn.sigmoid(_dot(xn, wg_ref[...]))
    u = _dot(xn, wu_ref[...])
    u_ref[...] = u
    return u


def _inproj_prompt_kernel(x_ref, xh_ref, cos_ref, sin_ref, g_ref, wq_ref, wkv_ref, wg_ref, wu_ref, *outs,
                          tiles_per_seq):
    d_ref = outs[-1]
    i = pl.program_id(0)
    tm = x_ref.shape[0]
    g = g_ref[...]
    seq_tile = i % tiles_per_seq
    pos = seq_tile * tm + lax.broadcasted_iota(jnp.int32, (tm, 1), 0)
    u = _project(x_ref[...], g, cos_ref[...], sin_ref[...], pos, wq_ref, wkv_ref, wg_ref, wu_ref, outs[:-1])
    uh = _dot(_rms(xh_ref[...], g).astype(BF16), wu_ref[...])
    uh = jnp.where(seq_tile == 0, 0.0, uh)
    ext = jnp.concatenate([uh, u], axis=0)
    for gi, w in enumerate(POOL_WINDOWS):
        sl = slice(gi * POOL_GROUP_DIM, (gi + 1) * POOL_GROUP_DIM)
        s = ext[:, sl]
        k = 1
        while k < w:
            s = s + pltpu.roll(s, k, 0)
            k *= 2
        cnt = jnp.minimum(w, pos + 1).astype(F32)
        d_ref[:, sl] = (s[POOL_HALO:, :] / cnt - u[:, sl]).astype(BF16)


def _inproj_sample_kernel(x_ref, sp_ref, cos_ref, sin_ref, g_ref, wq_ref, wkv_ref, wg_ref, wu_ref, *outs,
                          past_len):
    d_ref = outs[-1]
    pos = jnp.full((x_ref.shape[0], 1), past_len, jnp.int32)
    u = _project(x_ref[...], g_ref[...], cos_ref[...], sin_ref[...], pos, wq_ref, wkv_ref, wg_ref, wu_ref,
                 outs[:-1])
    for gi, w in enumerate(POOL_WINDOWS):
        sl = slice(gi * POOL_GROUP_DIM, (gi + 1) * POOL_GROUP_DIM)
        s = u[:, sl]
        for k in range(1, w):
            s = s + sp_ref[POOL_BUF - k][:, sl]
        cnt = float(min(w, past_len + 1))
        d_ref[:, sl] = (s / cnt - u[:, sl]).astype(BF16)


def _inproj_outs(n_seq, rows, tm, n_sel, row_map, seq_tile):
    t = n_seq * rows
    shapes, specs = [], []

    def add(shape, dtype, block, index_map):
        shapes.append(jax.ShapeDtypeStruct(shape, dtype))
        specs.append(pl.BlockSpec(block, index_map))

    t_map4 = lambda i: (seq_tile(i)[0], 0, 0, seq_tile(i)[1])
    add((t, QPAD_DIM), BF16, (tm, QPAD_DIM), row_map)
    add((t, KV_DIM), F32, (tm, KV_DIM), row_map)
    for _ in range(3):
        add((n_seq, 2, KV_HALF, rows), F32, (1, 2, KV_HALF, tm), t_map4)
    add((t, KV_HALF + n_sel), BF16, (tm, KV_HALF + n_sel), row_map)
    for _ in range(5):
        add((t, KV_HALF), BF16, (tm, KV_HALF), row_map)
    add((t, LANES), F32, (tm, LANES), row_map)
    add((t, POOL_DIM), F32, (tm, POOL_DIM), row_map)
    add((t, POOL_DIM), BF16, (tm, POOL_DIM), row_map)
    return tuple(shapes), tuple(specs)


def _weight_specs(shapes):
    return [pl.BlockSpec(s, lambda i, _n=len(s): (0,) * _n) for s in shapes]


def _inproj_prompt(x, cos, sin, g_mix, wq, wkv, wg, wu, batch, seq, tm):
    tiles_per_seq = seq // tm
    halo_blocks = tm // POOL_HALO
    in_specs = [
        pl.BlockSpec((tm, D_MODEL), lambda i: (i, 0)),
        pl.BlockSpec((POOL_HALO, D_MODEL), lambda i: (jnp.maximum(i * halo_blocks - 1, 0), 0)),
        pl.BlockSpec((tm, LANES), lambda i: (i % tiles_per_seq, 0)),
        pl.BlockSpec((tm, LANES), lambda i: (i % tiles_per_seq, 0)),
    ] + _weight_specs([g_mix.shape, wq.shape, wkv.shape, wg.shape, wu.shape])
    out_shape, out_specs = _inproj_outs(batch, seq, tm, seq // L_SEL, lambda i: (i, 0),
                                        lambda i: (i // tiles_per_seq, i % tiles_per_seq))
    return pl.pallas_call(
        functools.partial(_inproj_prompt_kernel, tiles_per_seq=tiles_per_seq),
        grid=(batch * tiles_per_seq,), in_specs=in_specs, out_specs=out_specs,
        out_shape=out_shape, compiler_params=_params(1), name="inproj_prompt",
    )(x, x, cos, sin, g_mix, wq, wkv, wg, wu)


def _inproj_sample(x, sp_t, cos, sin, g_mix, wq, wkv, wg, wu, past_len):
    t = x.shape[0]
    in_specs = [
        pl.BlockSpec((t, D_MODEL), lambda i: (0, 0)),
        pl.BlockSpec(sp_t.shape, lambda i: (0, 0, 0)),
        pl.BlockSpec((t, LANES), lambda i: (0, 0)),
        pl.BlockSpec((t, LANES), lambda i: (0, 0)),
    ] + _weight_specs([g_mix.shape, wq.shape, wkv.shape, wg.shape, wu.shape])
    out_shape, out_specs = _inproj_outs(1, t, t, LANES, lambda i: (0, 0), lambda i: (0, 0))
    return pl.pallas_call(
        functools.partial(_inproj_sample_kernel, past_len=past_len),
        grid=(1,), in_specs=in_specs, out_specs=out_specs,
        out_shape=out_shape, compiler_params=_params(1), name="inproj_sample",
    )(x, sp_t, cos, sin, g_mix, wq, wkv, wg, wu)


def _compress_rows(x_refs, pe_ref, w1_ref, w2_ref, o_ref):
    nb = o_ref.shape[0]
    for c, x_ref in enumerate(x_refs):
        acc = jnp.zeros((nb, w1_ref.shape[3]), F32)
        for l in range(L_CMP):
            xl = x_ref[pl.ds(l, nb, stride=L_CMP), :] + pe_ref[c, l]
            acc = acc + _dot(xl.astype(BF16), w1_ref[c, l])
        hid = acc * jax.nn.sigmoid(acc)
        o_ref[:, c * KV_HALF:(c + 1) * KV_HALF] = _dot(hid.astype(BF16), w2_ref[c]).astype(o_ref.dtype)


def _compress_kernel(xk_ref, xv_ref, pe_ref, w1_ref, w2_ref, o_ref):
    _compress_rows((xk_ref, xv_ref), pe_ref, w1_ref, w2_ref, o_ref)


def _compress(rows, pe, w1, w2, nb_tile):
    n_blocks = rows.shape[0] // L_CMP
    nb_tile = int(np.gcd(nb_tile, n_blocks))
    return pl.pallas_call(
        _compress_kernel,
        grid=(n_blocks // nb_tile,),
        in_specs=[pl.BlockSpec((nb_tile * L_CMP, KV_HALF), lambda i: (i, 0)),
                  pl.BlockSpec((nb_tile * L_CMP, KV_HALF), lambda i: (i, 1))]
        + _weight_specs([pe.shape, w1.shape, w2.shape]),
        out_specs=pl.BlockSpec((nb_tile, KV_DIM), lambda i: (i, 0)),
        out_shape=jax.ShapeDtypeStruct((n_blocks, KV_DIM), BF16),
        compiler_params=_params(1), name="compress",
    )(rows, rows, pe, w1, w2)


def _compress_pages_kernel(pt_ref, cache_ref, pet_ref, perm_ref, w1_ref, w2_ref, o_ref, buf_ref, sem_ref, xs_ref):
    b = pl.program_id(0)
    n_b = pl.num_programs(0)
    n_pages = buf_ref.shape[1]
    n_pairs = n_pages // 2
    nb = o_ref.shape[0]
    perm = perm_ref[...]

    def page_copy(bb, slot, j):
        return pltpu.make_async_copy(cache_ref.at[pt_ref[bb * n_pages + j]], buf_ref.at[slot, j], sem_ref.at[slot])

    @pl.when(b == 0)
    def _():
        for j in range(n_pages):
            page_copy(0, 0, j).start()

    @pl.when(b + 1 < n_b)
    def _():
        for j in range(n_pages):
            page_copy(b + 1, (b + 1) % 2, j).start()

    slot = b % 2
    for j in range(n_pages):
        page_copy(b, slot, j).wait()

    unroll = int(np.gcd(n_pairs, PAIR_UNROLL))

    def body(it, carry):
        for u in range(unroll):
            pp = it * unroll + u
            for c in range(2):
                xt = jnp.concatenate([buf_ref[slot, 2 * pp, c], buf_ref[slot, 2 * pp + 1, c]], axis=1) + pet_ref[c]
                xs_ref[c, pp] = _dot_nt(perm, xt.astype(BF16))
        return carry

    lax.fori_loop(0, n_pairs // unroll, body, 0)
    rows_per_l = 2 * LANES // L_CMP
    for c in range(2):
        acc = jnp.zeros((nb, w1_ref.shape[3]), F32)
        for l in range(L_CMP):
            xl = xs_ref[c, :, l * rows_per_l:(l + 1) * rows_per_l, :].reshape(nb, KV_HALF)
            acc = acc + _dot(xl.astype(BF16), w1_ref[c, l])
        hid = acc * jax.nn.sigmoid(acc)
        o_ref[:, c * KV_HALF:(c + 1) * KV_HALF] = _dot(hid.astype(BF16), w2_ref[c]).astype(o_ref.dtype)


def _compress_pages(cache_t, page_table, pe, w1, w2):
    _, _, _, page = cache_t.shape
    n_seq, n_pages = page_table.shape
    assert page == LANES and n_pages % 2 == 0
    per_page = page // L_CMP
    pet = jnp.tile(pe[:, :, 0, :].transpose(0, 2, 1), (1, 1, 2 * per_page))
    src = np.arange(2 * page)
    dst = (src % L_CMP) * (2 * per_page) + src // L_CMP
    perm = np.zeros((2 * page, 2 * page), np.float32)
    perm[dst, src] = 1.0
    perm = jnp.asarray(perm, BF16)
    fixed = lambda shape: pl.BlockSpec(shape, lambda b, pt, _n=len(shape): (0,) * _n)
    grid_spec = pltpu.PrefetchScalarGridSpec(
        num_scalar_prefetch=1, grid=(n_seq,),
        in_specs=[pl.BlockSpec(memory_space=pl.ANY)] + [fixed(a.shape) for a in (pet, perm, w1, w2)],
        out_specs=pl.BlockSpec((n_pages * per_page, KV_DIM), lambda b, pt: (b, 0)),
        scratch_shapes=[pltpu.VMEM((2, n_pages) + cache_t.shape[1:], F32), pltpu.SemaphoreType.DMA((2,)),
                        pltpu.VMEM((2, n_pages // 2, 2 * page, KV_HALF), F32)],
    )
    return pl.pallas_call(
        _compress_pages_kernel, grid_spec=grid_spec,
        out_shape=jax.ShapeDtypeStruct((n_seq * n_pages * per_page, KV_DIM), BF16),
        compiler_params=_params(1), name="compress_pages",
    )(page_table.reshape(-1), cache_t, pet, perm, w1, w2)


def _stack_heads(q, g):
    return jnp.concatenate([q[:, (GROUP * g + r) * LANES:(GROUP * g + r + 1) * LANES] for r in range(GROUP)], axis=0)


def _masked_softmax(s, mask):
    s = jnp.where(mask, s, NEG)
    e = jnp.exp2(s - jnp.max(s, axis=-1, keepdims=True))
    return e / jnp.sum(e, axis=-1, keepdims=True) * mask.astype(F32)


def _assemble_heads(o_heads):
    chunks = []
    for c in range(N_HEADS // 2):
        g = (2 * c) // GROUP
        a, b = o_heads[2 * c], o_heads[2 * c + 1]
        if g == 1:
            a = pltpu.roll(a, HEAD_DIM, 1)
        else:
            b = pltpu.roll(b, HEAD_DIM, 1)
        lane = lax.broadcasted_iota(jnp.int32, a.shape, 1)
        chunks.append(jnp.where(lane < HEAD_DIM, a, b))
    return jnp.concatenate(chunks, axis=1)


def _select_blocks(score, n_pick):
    n_blk = score.shape[1]
    lane = lax.broadcasted_iota(jnp.int32, score.shape, 1)
    work = score
    picks = []
    for _ in range(n_pick):
        m = jnp.max(work, axis=-1, keepdims=True)
        idx = jnp.min(jnp.where(work == m, lane, n_blk), axis=-1, keepdims=True)
        picks.append(idx)
        work = jnp.where(lane == idx, -jnp.inf, work)
    return picks


def _select_blocks_t(score_t, n_pick):
    n_blk = score_t.shape[0]
    blk = lax.broadcasted_iota(jnp.int32, score_t.shape, 0)
    work = score_t
    sel = jnp.zeros(score_t.shape, F32)
    for _ in range(n_pick):
        m = jnp.max(work, axis=0, keepdims=True)
        idx = jnp.min(jnp.where(work == m, blk, n_blk), axis=0, keepdims=True)
        hit = blk == idx
        sel = jnp.where(hit, 1.0, sel)
        work = jnp.where(hit, -jnp.inf, work)
    return sel


def _cmp_attention(qg, kc, vc, pos_rows, n_sel):
    s = _dot_nt(qg, kc)
    j = lax.broadcasted_iota(jnp.int32, s.shape, 1)
    n = 2 * (j % n_sel) + j // n_sel
    mask = (n * L_CMP + (L_CMP - 1)) <= pos_rows
    p = _masked_softmax(s, mask)
    return _dot(p.astype(BF16), vc), p


def _cmp_topk_kernel(q_ref, c_ref, gates_ref, oc_ref, msel_ref, *, n_sel, n_top):
    i = pl.program_id(1)
    tq = q_ref.shape[0]
    q = q_ref[...]
    comp = c_ref[0]
    kc, vc = comp[:, :LANES], comp[:, LANES:]
    gates = gates_ref[...]
    pos = i * tq + lax.broadcasted_iota(jnp.int32, (tq, 1), 0)
    pos_rows = jnp.concatenate([pos] * GROUP, axis=0)
    blk = lax.broadcasted_iota(jnp.int32, (n_sel, tq), 0)
    cur = (i * tq + lax.broadcasted_iota(jnp.int32, (1, tq), 1)) // L_SEL
    forced = (blk == 0) | (blk == cur) | (blk == cur - 1)
    o_heads = []
    for g in range(N_KV_HEADS):
        o, p = _cmp_attention(_stack_heads(q, g), kc, vc, pos_rows, n_sel)
        imp = p[0:tq]
        for r in range(1, GROUP):
            imp = imp + p[r * tq:(r + 1) * tq]
        imp = (imp[:, :n_sel] + imp[:, n_sel:]).T
        score = jnp.where(forced, FORCED, jnp.where(blk < cur, imp, NEG))
        sel = jnp.where(score > NEG / 2, _select_blocks_t(score, n_top), 0.0)
        msel_ref[:, g * n_sel:(g + 1) * n_sel] = ((1.0 - sel) * NEG).T.astype(BF16)
        for r in range(GROUP):
            h = GROUP * g + r
            o_heads.append(o[r * tq:(r + 1) * tq] * gates[:, h:h + 1])
    oc_ref[...] = _assemble_heads(o_heads)


def _cmp_topk_prompt(q, comp, gates, batch, seq, tq):
    n_sel = seq // L_SEL
    n_c = comp.shape[1]
    tiles = seq // tq
    row = lambda b, i: (b * tiles + i, 0)
    return pl.pallas_call(
        functools.partial(_cmp_topk_kernel, n_sel=n_sel, n_top=min(TOP_N, n_sel)),
        grid=(batch, tiles),
        in_specs=[pl.BlockSpec((tq, QPAD_DIM), row),
                  pl.BlockSpec((1, n_c, KV_DIM), lambda b, i: (b, 0, 0)),
                  pl.BlockSpec((tq, LANES), row)],
        out_specs=(pl.BlockSpec((tq, NSA_DIM), row), pl.BlockSpec((tq, N_KV_HEADS * n_sel), row)),
        out_shape=(jax.ShapeDtypeStruct((batch * seq, NSA_DIM), F32),
                   jax.ShapeDtypeStruct((batch * seq, N_KV_HEADS * n_sel), BF16)),
        compiler_params=_params(2), name="cmp_topk_prompt",
    )(q, comp, gates)


def _flash_update(carry, q, k, v, bias):
    m, acc = carry
    s = _dot_nt(q, k)
    if bias is not None:
        s = s + bias
    m_new = jnp.maximum(m, jnp.max(s, axis=-1, keepdims=True))
    p = jnp.exp2(s - m_new).astype(BF16)
    acc = jnp.exp2(m - m_new) * acc + _dot(p, v)
    return m_new, acc


def _sel_win_kernel(q_ref, ks_ref, vs0_ref, vs1_ref, kw_ref, vw0_ref, vw1_ref, msel_ref, gates_ref, oc_ref, o_ref,
                    *, n_sel):
    i = pl.program_id(1)
    tq = q_ref.shape[0]
    hpc = HEADS_PER_CHAIN
    n_chain = N_HEADS // hpc
    rows = hpc * tq
    q = q_ref[...]
    gates = gates_ref[...]
    vs_refs, vw_refs = (vs0_ref, vs1_ref), (vw0_ref, vw1_ref)
    pos = i * tq + lax.broadcasted_iota(jnp.int32, (tq, 1), 0)
    key = lax.broadcasted_iota(jnp.int32, (1, tq), 1)
    d0 = pl.multiple_of(i * tq, tq)
    p0 = pl.multiple_of(jnp.maximum(i - 1, 0) * tq, tq)
    bias_d = jnp.concatenate([jnp.where(i * tq + key <= pos, 0.0, NEG)] * hpc, axis=0)
    in_win = (pos - ((i - 1) * tq + key) < WINDOW) & (i > 0)
    bias_p = jnp.concatenate([jnp.where(in_win, 0.0, NEG)] * hpc, axis=0)
    groups = [(c * hpc) // GROUP for c in range(n_chain)]
    qcs, qas = [], []
    for c in range(n_chain):
        g = groups[c]
        qc = jnp.concatenate([q[:, h * LANES:(h + 1) * LANES] for h in range(c * hpc, (c + 1) * hpc)], axis=0)
        msel = jnp.concatenate([msel_ref[:, g * n_sel:(g + 1) * n_sel]] * hpc, axis=0)
        qcs.append(qc)
        qas.append(jnp.concatenate([qc, msel], axis=1))
    init = (jnp.full((rows, 1), NEG, F32), jnp.zeros((rows, LANES), F32))

    def tiles(k_ref, v_refs, start):
        return k_ref[pl.ds(start, tq), :], [v_ref[pl.ds(start, tq), :] for v_ref in v_refs]

    kd, vd = tiles(ks_ref, vs_refs, d0)
    sel = tuple(_flash_update(init, qas[c], kd, vd[groups[c]], bias_d) for c in range(n_chain))

    def body(kt, carries):
        k, v = tiles(ks_ref, vs_refs, pl.multiple_of(kt * tq, tq))
        return tuple(_flash_update(carries[c], qas[c], k, v[groups[c]], None) for c in range(n_chain))

    sel = lax.fori_loop(0, i, body, sel)
    kd, vd = tiles(kw_ref, vw_refs, d0)
    kp, vp = tiles(kw_ref, vw_refs, p0)
    o_heads = []
    for c in range(n_chain):
        g = groups[c]
        win = _flash_update(_flash_update(init, qcs[c], kd, vd[g], bias_d), qcs[c], kp, vp[g], bias_p)
        den = (1 - g) * HEAD_DIM
        o_s = sel[c][1] / sel[c][1][:, den:den + 1]
        o_w = win[1] / win[1][:, den:den + 1]
        for r in range(hpc):
            h = c * hpc + r
            rs = slice(r * tq, (r + 1) * tq)
            o_heads.append(o_s[rs] * gates[:, N_HEADS + h:N_HEADS + h + 1]
                           + o_w[rs] * gates[:, 2 * N_HEADS + h:2 * N_HEADS + h + 1])
    o_ref[...] = (_assemble_heads(o_heads) + oc_ref[...]).astype(o_ref.dtype)


def _sel_win_prompt(q, ksb, vs0, vs1, kwb, vw0, vw1, msel, gates, oc, batch, seq, tq):
    assert tq == WINDOW
    n_sel = seq // L_SEL
    tiles = seq // tq
    row = lambda b, i: (b * tiles + i, 0)
    whole = lambda b, i: (b, 0)
    return pl.pallas_call(
        functools.partial(_sel_win_kernel, n_sel=n_sel),
        grid=(batch, tiles),
        in_specs=[pl.BlockSpec((tq, QPAD_DIM), row),
                  pl.BlockSpec((seq, KV_HALF + n_sel), whole, pipeline_mode=pl.Buffered(1))]
        + [pl.BlockSpec((seq, KV_HALF), whole, pipeline_mode=pl.Buffered(1))] * 5
        + [pl.BlockSpec((tq, N_KV_HEADS * n_sel), row),
           pl.BlockSpec((tq, LANES), row),
           pl.BlockSpec((tq, NSA_DIM), row)],
        out_specs=pl.BlockSpec((tq, NSA_DIM), row),
        out_shape=jax.ShapeDtypeStruct((batch * seq, NSA_DIM), BF16),
        compiler_params=_params(2), name="sel_win_prompt",
    )(q, ksb, vs0, vs1, kwb, vw0, vw1, msel, gates, oc)


def _sample_cmp_kernel(q_ref, c_ref, gates_ref, oc_ref, idx_ref, *, n_blk, n_pick, past_len):
    n_sb = q_ref.shape[0]
    pos_rows = jnp.full((N_HEADS, 1), past_len, jnp.int32)
    imps = []
    for s in range(n_sb):
        comp = c_ref[s]
        o, p = _cmp_attention(q_ref[s], comp[:, :LANES], comp[:, LANES:], pos_rows, n_blk)
        oc_ref[s] = o * gates_ref[s][:, 0:1]
        imps += [jnp.sum(p[GROUP * g:GROUP * (g + 1)], axis=0, keepdims=True) for g in range(N_KV_HEADS)]
    imp = jnp.concatenate(imps, axis=0)
    imp = imp[:, :n_blk] + imp[:, n_blk:]
    cur = past_len // L_SEL
    blk = lax.broadcasted_iota(jnp.int32, imp.shape, 1)
    forced = (blk == 0) | (blk == cur) | (blk == cur - 1)
    score = jnp.where(forced, FORCED, jnp.where(blk < cur, imp, NEG))
    lane = lax.broadcasted_iota(jnp.int32, idx_ref.shape, 1)
    out = jnp.zeros(idx_ref.shape, jnp.int32)
    for k, idx in enumerate(_select_blocks(score, n_pick)):
        out = jnp.where(lane == k, idx, out)
    idx_ref[...] = out


def _sample_cmp(q3, comp, gates3, n_pick, past_len, seqs_per_step):
    n_seq = q3.shape[0]
    n_c = comp.shape[1]
    n_sb = int(np.gcd(seqs_per_step, n_seq))
    blk3 = lambda b: (b, 0, 0)
    return pl.pallas_call(
        functools.partial(_sample_cmp_kernel, n_blk=n_c // 2, n_pick=n_pick, past_len=past_len),
        grid=(n_seq // n_sb,),
        in_specs=[pl.BlockSpec((n_sb, N_HEADS, LANES), blk3),
                  pl.BlockSpec((n_sb, n_c, KV_DIM), blk3),
                  pl.BlockSpec((n_sb, N_HEADS, LANES), blk3)],
        out_specs=(pl.BlockSpec((n_sb, N_HEADS, LANES), blk3),
                   pl.BlockSpec((n_sb * N_KV_HEADS, LANES), lambda b: (b, 0))),
        out_shape=(jax.ShapeDtypeStruct((n_seq, N_HEADS, LANES), F32),
                   jax.ShapeDtypeStruct((n_seq * N_KV_HEADS, LANES), jnp.int32)),
        compiler_params=_params(1), name="sample_cmp",
    )(q3, comp, gates3)


def _softmax_with_new(s, valid, s_new):
    s = jnp.where(valid, s, NEG)
    m = jnp.maximum(jnp.max(s, axis=-1, keepdims=True), s_new)
    e = jnp.exp2(s - m) * valid.astype(F32)
    e_new = jnp.exp2(s_new - m)
    den = jnp.sum(e, axis=-1, keepdims=True) + e_new
    return e / den, e_new / den


def _sample_sel_win_kernel(pg_ref, hf_ref, q_ref, win_ref, kvs_ref, kvw_ref, gates_ref, oc_ref, cache_ref,
                           o_ref, buf_ref, sem_ref, *, n_pick):
    b = pl.program_id(0)
    n_b = pl.num_programs(0)
    n_dma = N_KV_HEADS * n_pick
    page = buf_ref.shape[-1]

    def page_copy(bb, slot, j):
        return pltpu.make_async_copy(cache_ref.at[pg_ref[bb * n_dma + j]], buf_ref.at[slot, j], sem_ref.at[slot])

    @pl.when(b == 0)
    def _():
        for j in range(n_dma):
            page_copy(0, 0, j).start()

    @pl.when(b + 1 < n_b)
    def _():
        for j in range(n_dma):
            page_copy(b + 1, (b + 1) % 2, j).start()

    q = q_ref[0]
    qf = q.astype(F32)
    gates = gates_ref[0]
    row = lax.broadcasted_iota(jnp.int32, (N_HEADS, LANES), 0)

    def new_key(kv_ref):
        kv = kv_ref[0].astype(BF16).astype(F32)
        return jnp.sum(qf * kv[:, :KV_HALF], axis=-1, keepdims=True), kv[:, KV_HALF:]

    win = win_ref[0]
    kt, vt = win[0].astype(BF16), win[1].astype(BF16)
    s_new, v_new = new_key(kvw_ref)
    lane_w = lax.broadcasted_iota(jnp.int32, (N_HEADS, kt.shape[1]), 1)
    p, p_new = _softmax_with_new(_dot(q, kt), lane_w >= 1, s_new)
    o_w = _dot_nt(p.astype(BF16), vt) + p_new.astype(BF16).astype(F32) * v_new

    slot = b % 2
    for j in range(n_dma):
        page_copy(b, slot, j).wait()
    s_new, v_new = new_key(kvs_ref)
    lane_p = lax.broadcasted_iota(jnp.int32, (N_HEADS, page), 1)
    o_s = jnp.zeros((N_HEADS, LANES), F32)
    for g in range(N_KV_HEADS):
        kts, vts, valids = [], [], []
        for k in range(n_pick):
            j = g * n_pick + k
            pg = buf_ref[slot, j]
            kts.append(pg[0].astype(BF16))
            vts.append(pg[1].astype(BF16))
            half = hf_ref[b * n_dma + j]
            valids.append((lane_p // L_SEL) == half)
        kt_all = jnp.concatenate(kts, axis=1)
        vt_all = jnp.concatenate(vts, axis=1)
        p, p_new = _softmax_with_new(_dot(q, kt_all), jnp.concatenate(valids, axis=1), s_new)
        o_g = _dot_nt(p.astype(BF16), vt_all) + p_new.astype(BF16).astype(F32) * v_new
        o_s = jnp.where(row // GROUP == g, o_g, o_s)
    o_ref[0] = oc_ref[0] + o_s * gates[:, 1:2] + o_w * gates[:, 2:3]


def _sample_sel_win(pages, halves, q3, win_t, kvs, kvw, gates3, oc, cache_t, n_pick):
    n_seq = q3.shape[0]
    n_dma = N_KV_HEADS * n_pick
    blk3 = lambda b, pg, hf: (b, 0, 0)
    grid_spec = pltpu.PrefetchScalarGridSpec(
        num_scalar_prefetch=2, grid=(n_seq,),
        in_specs=[pl.BlockSpec((1, N_HEADS, LANES), blk3),
                  pl.BlockSpec((1,) + win_t.shape[1:], lambda b, pg, hf: (b, 0, 0, 0)),
                  pl.BlockSpec((1, 1, KV_DIM), blk3),
                  pl.BlockSpec((1, 1, KV_DIM), blk3),
                  pl.BlockSpec((1, N_HEADS, LANES), blk3),
                  pl.BlockSpec((1, N_HEADS, LANES), blk3),
                  pl.BlockSpec(memory_space=pl.ANY)],
        out_specs=pl.BlockSpec((1, N_HEADS, LANES), blk3),
        scratch_shapes=[pltpu.VMEM((2, n_dma) + cache_t.shape[1:], F32), pltpu.SemaphoreType.DMA((2,))],
    )
    return pl.pallas_call(
        functools.partial(_sample_sel_win_kernel, n_pick=n_pick),
        grid_spec=grid_spec,
        out_shape=jax.ShapeDtypeStruct((n_seq, N_HEADS, LANES), F32),
        compiler_params=_params(1), name="sample_sel_win",
    )(pages, halves, q3, win_t, kvs, kvw, gates3, oc, cache_t)


def _tail_kernel(x_ref, attn_ref, d_ref, gmix_ref, wgm_ref, wpool_ref, pscale_ref,
                 wbn_ref, wbp_ref, wout_ref, gffn_ref, wr_ref, br_ref, h_ref, xn2_ref, route_ref, cout_ref, run_ref,
                 *, n_experts):
    i = pl.program_id(0)
    x = x_ref[...]
    attn = attn_ref[...]
    d = d_ref[...]
    tm = x.shape[0]
    xn = _rms(x, gmix_ref[...]).astype(BF16)
    gm = jax.nn.sigmoid(_dot(xn, wgm_ref[...]))
    pool = jnp.concatenate(
        [_dot(d[:, gi * POOL_GROUP_DIM:(gi + 1) * POOL_GROUP_DIM], wpool_ref[gi]) for gi in range(len(POOL_WINDOWS))],
        axis=1) * pscale_ref[...]
    merged = (gm[:, :D_MODEL] * _dot(attn, wbn_ref[...])
              + gm[:, D_MODEL:] * _dot(pool.astype(BF16), wbp_ref[...]))
    h = x + _dot(merged.astype(BF16), wout_ref[...])
    h_ref[...] = h
    xn2 = _rms(h, gffn_ref[...]).astype(BF16)
    xn2_ref[...] = xn2
    logits = _dot(xn2, wr_ref[...]) + br_ref[...]
    lane = lax.broadcasted_iota(jnp.int32, (tm, LANES), 1)
    work = jnp.where(lane < n_experts, logits, -jnp.inf)
    vals, onehots = [], []
    for _ in range(TOP_K):
        m = jnp.max(work, axis=-1, keepdims=True)
        idx = jnp.min(jnp.where(work == m, lane, LANES), axis=-1, keepdims=True)
        vals.append(m)
        onehots.append(lane == idx)
        work = jnp.where(lane == idx, -jnp.inf, work)
    es = [jnp.exp(v - vals[0]) for v in vals]
    tot = es[0] + es[1] + es[2] + es[3]

    @pl.when(i == 0)
    def _():
        run_ref[...] = jnp.zeros(run_ref.shape, F32)

    chosen = jnp.zeros((tm, LANES), F32)
    for k in range(TOP_K):
        chosen = jnp.where(onehots[k], 1.0, chosen)
    earlier = lax.broadcasted_iota(jnp.int32, (tm, tm), 0) > lax.broadcasted_iota(jnp.int32, (tm, tm), 1)
    before = run_ref[...] + _dot(jnp.where(earlier, 1.0, 0.0).astype(BF16), chosen.astype(BF16))
    route = jnp.zeros((tm, LANES), F32)
    lane_f = lane.astype(F32)
    for k in range(TOP_K):
        e_k = jnp.sum(jnp.where(onehots[k], lane_f, 0.0), axis=-1, keepdims=True)
        r_k = jnp.sum(jnp.where(onehots[k], before, 0.0), axis=-1, keepdims=True)
        route = jnp.where(lane == k, e_k, route)
        route = jnp.where(lane == TOP_K + k, es[k] / tot, route)
        route = jnp.where(lane == 2 * TOP_K + k, r_k, route)
    route_ref[...] = route
    run_ref[...] = run_ref[...] + jnp.sum(chosen, axis=0, keepdims=True)
    cout_ref[...] = run_ref[...]


def _tail(x, attn, d, weights, n_experts, tm):
    t = x.shape[0]
    row = lambda i: (i, 0)
    fixed = lambda i: (0, 0)
    return pl.pallas_call(
        functools.partial(_tail_kernel, n_experts=n_experts),
        grid=(t // tm,),
        in_specs=[pl.BlockSpec((tm, D_MODEL), row), pl.BlockSpec((tm, NSA_DIM), row),
                  pl.BlockSpec((tm, POOL_DIM), row)]
        + _weight_specs([w.shape for w in weights]),
        out_specs=(pl.BlockSpec((tm, D_MODEL), row), pl.BlockSpec((tm, D_MODEL), row),
                   pl.BlockSpec((tm, LANES), row), pl.BlockSpec((1, LANES), fixed)),
        out_shape=(jax.ShapeDtypeStruct((t, D_MODEL), F32), jax.ShapeDtypeStruct((t, D_MODEL), BF16),
                   jax.ShapeDtypeStruct((t, LANES), F32), jax.ShapeDtypeStruct((1, LANES), F32)),
        scratch_shapes=[pltpu.VMEM((1, LANES), F32)],
        compiler_params=_params(1), name="tail",
    )(x, attn, d, *weights)


def _moe_kernel(te_ref, tv_ref, tf_ref, x_ref, wg_ref, bg_ref, wu_ref, bu_ref, wd_ref, bd_ref, y_ref,
                wgb_ref, wub_ref, wdb_ref):
    t = pl.program_id(0)

    @pl.when(tf_ref[t] > 0)
    def _():
        wgb_ref[...] = wg_ref[0].astype(BF16)
        wub_ref[...] = wu_ref[0].astype(BF16)
        wdb_ref[...] = wd_ref[0].astype(BF16)

    @pl.when(tv_ref[t] > 0)
    def _():
        x = x_ref[...]
        gate = jnp.minimum(_dot(x, wgb_ref[...]) + bg_ref[0], SWIGLU_LIMIT)
        up = jnp.clip(_dot(x, wub_ref[...]) + bu_ref[0], -SWIGLU_LIMIT, SWIGLU_LIMIT)
        hid = (up + 1.0) * gate * jax.nn.sigmoid(SWIGLU_ALPHA * gate)
        y_ref[...] = (_dot(hid.astype(BF16), wdb_ref[...]) + bd_ref[0]).astype(y_ref.dtype)

    @pl.when(tv_ref[t] == 0)
    def _():
        y_ref[...] = jnp.zeros(y_ref.shape, y_ref.dtype)


def _moe(tile_e, tile_v, tile_f, xs, wg, bg, wu, bu, wd, bd, tme):
    p_pad = xs.shape[0]
    d_e = wg.shape[2]
    wspec = lambda shape: pl.BlockSpec((1,) + shape, lambda t, te, tv, tf: (te[t], 0, 0))
    grid_spec = pltpu.PrefetchScalarGridSpec(
        num_scalar_prefetch=3, grid=(p_pad // tme,),
        in_specs=[pl.BlockSpec((tme, D_MODEL), lambda t, te, tv, tf: (t, 0)),
                  wspec((D_MODEL, d_e)), wspec((1, d_e)), wspec((D_MODEL, d_e)), wspec((1, d_e)),
                  wspec((d_e, D_MODEL)), wspec((1, D_MODEL))],
        out_specs=pl.BlockSpec((tme, D_MODEL), lambda t, te, tv, tf: (t, 0)),
        scratch_shapes=[pltpu.VMEM((D_MODEL, d_e), BF16), pltpu.VMEM((D_MODEL, d_e), BF16),
                        pltpu.VMEM((d_e, D_MODEL), BF16)],
    )
    return pl.pallas_call(
        _moe_kernel, grid_spec=grid_spec,
        out_shape=jax.ShapeDtypeStruct((p_pad, D_MODEL), BF16),
        compiler_params=_params(1), name="moe_experts",
    )(tile_e, tile_v, tile_f, xs, wg, bg, wu, bu, wd, bd)


def _final_kernel(h_ref, y_ref, route_ref, g_ref, o_ref):
    route = route_ref[...]
    acc = h_ref[...]
    for k in range(TOP_K):
        acc = acc + y_ref[k].astype(F32) * route[:, TOP_K + k:TOP_K + k + 1]
    o_ref[...] = _rms(acc, g_ref[...])


def _final(h, yg, route, g_final, tm, first, n_tiles):
    src = lambda i: (i + first, 0)
    return pl.pallas_call(
        _final_kernel, grid=(n_tiles,),
        in_specs=[pl.BlockSpec((tm, D_MODEL), src), pl.BlockSpec((TOP_K, tm, D_MODEL), lambda i: (0, i + first, 0)),
                  pl.BlockSpec((tm, LANES), src), pl.BlockSpec((1, D_MODEL), lambda i: (0, 0))],
        out_specs=pl.BlockSpec((tm, D_MODEL), lambda i: (i, 0)),
        out_shape=jax.ShapeDtypeStruct((n_tiles * tm, D_MODEL), F32),
        compiler_params=_params(1), name="final_norm",
    )(h, yg, route, g_final)


def _rope_tables(pos):
    half = HEAD_DIM // 2
    inv = ROPE_THETA ** (-jnp.arange(half, dtype=F32) / half)
    ang = pos.astype(F32)[:, None] * inv[None, :]
    cos, sin = jnp.cos(ang), jnp.sin(ang)
    cos = jnp.concatenate([cos, cos] * (LANES // HEAD_DIM), axis=1)
    sin = jnp.concatenate([-sin, sin] * (LANES // HEAD_DIM), axis=1)
    return cos, sin


def _split_w_in(w_in):
    o_q = NSA_DIM
    o_g = o_q + 3 * KV_DIM
    o_u = o_g + 3 * N_HEADS
    o_m = o_u + POOL_DIM
    wq = w_in[:, :o_q].reshape(D_MODEL, N_HEADS, HEAD_DIM)
    zeros = jnp.zeros_like(wq)
    lo = jnp.concatenate([wq, zeros], axis=2)
    hi = jnp.concatenate([zeros, wq], axis=2)
    in_hi = (jnp.arange(N_HEADS) // GROUP == 1)[None, :, None]
    wq_pad = jnp.where(in_hi, hi, lo).reshape(D_MODEL, QPAD_DIM)
    wkv = w_in[:, o_q:o_g]
    wg = w_in[:, o_g:o_u].reshape(D_MODEL, N_HEADS, 3).transpose(0, 2, 1).reshape(D_MODEL, 3 * N_HEADS)
    wg = jnp.pad(wg, ((0, 0), (0, LANES - 3 * N_HEADS)))
    wu = w_in[:, o_u:o_m]
    wgm = w_in[:, o_m:]
    return [w.astype(BF16) for w in (wq_pad, wkv, wg, wu, wgm)]


def _compress_weights(cmp_pe, cmp_w1, cmp_w2):
    eye = jnp.eye(N_KV_HEADS, dtype=F32)
    w1 = jnp.einsum('cldh,gf->clgdfh', cmp_w1, eye).reshape(2, L_CMP, KV_HALF, N_KV_HEADS * CMP_HIDDEN)
    w2 = jnp.einsum('chd,gf->cghfd', cmp_w2, eye).reshape(2, N_KV_HEADS * CMP_HIDDEN, KV_HALF)
    pe = jnp.broadcast_to(cmp_pe.transpose(1, 0, 2)[:, :, None, :], (2, L_CMP, N_KV_HEADS, HEAD_DIM))
    return pe.reshape(2, L_CMP, 1, KV_HALF), w1.astype(BF16), w2.astype(BF16)


def _even_odd(comp, n_seq):
    n_c = comp.shape[0] // n_seq
    return comp.reshape(n_seq, n_c // 2, 2, KV_DIM).transpose(0, 2, 1, 3).reshape(n_seq, n_c, KV_DIM)


def _fix_sample_heads(o3):
    n_seq = o3.shape[0]
    o4 = o3.reshape(n_seq, N_KV_HEADS, GROUP, N_KV_HEADS, HEAD_DIM)
    return jnp.concatenate([o4[:, g, :, g, :] for g in range(N_KV_HEADS)], axis=1).reshape(n_seq, NSA_DIM)


def _gates_by_head(gates):
    g = gates[:, :3 * N_HEADS].reshape(-1, 3, N_HEADS).transpose(0, 2, 1)
    return jnp.pad(g, ((0, 0), (0, 0), (0, LANES - 3)))


def _key_minor(x):
    n, p = x.shape[:2]
    return x.transpose(0, 2, 3, 4, 1).reshape(n, 2, KV_HALF, p)


def _from_key_minor(xt):
    n, _, _, p = xt.shape
    return xt.reshape(n, 2, N_KV_HEADS, HEAD_DIM, p).transpose(0, 4, 1, 2, 3)


def _expert_order(route, counts, n_experts, tme):
    t = route.shape[0]
    n_pairs = t * TOP_K
    e = route[:, :TOP_K].astype(jnp.int32)
    rank = route[:, 2 * TOP_K:3 * TOP_K].astype(jnp.int32)
    counts = counts[0, :n_experts].astype(jnp.int32)
    padded = ((counts + tme - 1) // tme) * tme
    ends = jnp.cumsum(padded)
    off = ends - padded
    pad_before = off - (jnp.cumsum(counts) - counts)
    experts = jnp.arange(n_experts, dtype=jnp.int32)
    pair_slot = rank + jnp.sum(jnp.where(e[:, :, None] == experts, off, 0), axis=-1)
    n_tiles = -(-n_pairs // tme) + n_experts
    p_pad = n_tiles * tme
    tok = jnp.broadcast_to(jnp.arange(t, dtype=jnp.int32)[:, None], (t, TOP_K))
    _, tok_sorted = lax.sort((pair_slot.reshape(n_pairs), tok.reshape(n_pairs)), num_keys=1)
    slot = jnp.arange(p_pad, dtype=jnp.int32)
    slot_e = jnp.sum((slot[:, None] >= ends[None, :]).astype(jnp.int32), axis=-1)
    slot_e = jnp.minimum(slot_e, n_experts - 1)
    slot_pad = jnp.sum(jnp.where(slot_e[:, None] == experts, pad_before, 0), axis=-1)
    src_tok = tok_sorted[jnp.clip(slot - slot_pad, 0, n_pairs - 1)]
    tile_start = jnp.arange(n_tiles, dtype=jnp.int32) * tme
    tile_e = jnp.minimum(jnp.sum((tile_start[:, None] >= ends[None, :]).astype(jnp.int32), axis=-1), n_experts - 1)
    tile_v = (tile_start < ends[-1]).astype(jnp.int32)
    tile_f = jnp.concatenate([jnp.ones((1,), jnp.int32), (tile_e[1:] != tile_e[:-1]).astype(jnp.int32)])
    return pair_slot, src_tok, tile_e, tile_v, tile_f


def kernel(x_prompt, x_sample, cache_kv_cmp, cache_kv_sel, state_kv_win, state_pool, page_table, g_mix, w_in, cmp_pe, cmp_w1, cmp_w2, w_pool, pool_scale, w_br_nsa, w_br_pool, w_out, g_ffn, w_router, b_router, w_gate, b_gate, w_up, b_up, w_down, b_down, g_final):
    batch, seq, _ = x_prompt.shape
    n_seq, dec_seq, _ = x_sample.shape
    depth = g_mix.shape[0]
    assert depth == 1 and dec_seq == 1
    page_size = cache_kv_cmp.shape[2]
    n_pages = page_table.shape[1]
    past_len = n_pages * page_size
    n_experts = w_router.shape[2]
    t_p = batch * seq
    tm = min(ROW_TILE, seq)

    wq, wkv, wg, wu, wgm = _split_w_in(w_in[0])
    gmix = g_mix[0][None, :]
    pe, w1, w2 = _compress_weights(cmp_pe[0], cmp_w1[0], cmp_w2[0])
    cos_p, sin_p = _rope_tables(jnp.arange(seq, dtype=jnp.int32))
    cos_s, sin_s = _rope_tables(jnp.full((n_seq,), past_len, jnp.int32))
    wr = jnp.pad(w_router[0], ((0, 0), (0, LANES - n_experts))).astype(BF16)
    br = jnp.pad(b_router[0], (0, LANES - n_experts))[None, :]
    tail_w = [gmix, wgm, w_pool[0].astype(BF16), pool_scale[0][None, :], w_br_nsa[0].astype(BF16),
              w_br_pool[0].astype(BF16), w_out[0].astype(BF16), g_ffn[0][None, :], wr, br]

    xp = x_prompt.reshape(t_p, D_MODEL)
    (q_p, kvc_p, kvct_p, kvst_p, kvwt_p, ksb_p, vs0_p, vs1_p, kwb_p, vw0_p, vw1_p, gates_p, u_p, d_p) = _inproj_prompt(
        xp, cos_p, sin_p, gmix, wq, wkv, wg, wu, batch, seq, tm)
    comp_p = _even_odd(_compress(kvc_p, pe, w1, w2, CMP_BLOCK_TILE), batch)
    tq = min(CMP_Q_TILE, seq)
    oc_p, msel_p = _cmp_topk_prompt(q_p, comp_p, gates_p, batch, seq, tq)
    attn_p = _sel_win_prompt(q_p, ksb_p, vs0_p, vs1_p, kwb_p, vw0_p, vw1_p, msel_p, gates_p, oc_p, batch, seq, WINDOW)

    def ffn(x, attn, d, tile, tme, after=None):
        h, xn2, route, counts = _tail(x, attn, d, tail_w, n_experts, tile)
        pair_slot, src_tok, tile_e, tile_v, tile_f = _expert_order(route, counts, n_experts, tme)
        tied = after(src_tok) if after is not None else None
        y_sorted = _moe(tile_e, tile_v, tile_f, xn2[src_tok], w_gate[0], b_gate[0][:, None, :],
                        w_up[0], b_up[0][:, None, :], w_down[0], b_down[0][:, None, :], tme)
        y = _final(h, y_sorted[pair_slot.T], route, g_final[None, :], tile, 0, x.shape[0] // tile)
        return y, tied

    tie = lambda order: lax.optimization_barrier((x_sample.reshape(n_seq, D_MODEL), page_table, order))[:2]
    y_p, (xs, page_table) = ffn(xp, attn_p, d_p, tm, MOE_TILE, after=tie)

    sp_t = state_pool[0].transpose(1, 0, 2)
    (q_s, kvc_s, _, kvst_s, kvwt_s, _, _, _, _, _, _, gates_s, u_s, d_s) = _inproj_sample(
        xs, sp_t, cos_s, sin_s, gmix, wq, wkv, wg, wu, past_len)
    kvs_s = kvst_s.reshape(KV_DIM, n_seq).T
    kvw_s = kvwt_s.reshape(KV_DIM, n_seq).T
    comp_s = _even_odd(_compress_pages(_key_minor(cache_kv_cmp[0]), page_table, pe, w1, w2), n_seq)
    q3 = q_s.reshape(n_seq, N_HEADS, LANES)
    gates3 = _gates_by_head(gates_s)
    n_blk = past_len // L_SEL
    n_pick = min(TOP_N, n_blk + 1) - 1
    oc_s, idx_s = _sample_cmp(q3, comp_s, gates3, n_pick, past_len, SAMPLE_SEQS_PER_STEP)
    top_idx = idx_s.reshape(n_seq, N_KV_HEADS, LANES)[:, :, :n_pick]
    sel_per_page = page_size // L_SEL
    pages_b = jnp.broadcast_to(page_table[:, None, :], (n_seq, N_KV_HEADS, n_pages))
    phys = jnp.take_along_axis(pages_b, top_idx // sel_per_page, axis=2).reshape(-1)
    halves = (top_idx % sel_per_page).reshape(-1)
    w_buf = state_kv_win.shape[2]
    assert w_buf == WINDOW
    win_t = _key_minor(state_kv_win[0])
    o3 = _sample_sel_win(phys, halves, q3, win_t, kvs_s[:, None, :], kvw_s[:, None, :], gates3, oc_s,
                         _key_minor(cache_kv_sel[0]), n_pick)
    attn_s = _fix_sample_heads(o3).astype(BF16)
    y_s, _ = ffn(xs, attn_s, d_s, n_seq, SAMPLE_MOE_TILE)

    w_p = min(WINDOW, seq)
    kv_row = (2, N_KV_HEADS, HEAD_DIM)
    new_col = kvwt_s.reshape(2, KV_HALF, n_seq).transpose(2, 0, 1)[..., None]
    win_new_t = jnp.concatenate([win_t[..., 1:], new_col], axis=-1)
    return (
        y_p.reshape(batch, seq, D_MODEL),
        y_s.reshape(n_seq, 1, D_MODEL),
        _from_key_minor(kvct_p)[None],
        kvc_s.reshape((1, n_seq, 1) + kv_row),
        _from_key_minor(kvst_p)[None],
        kvs_s.reshape((1, n_seq, 1) + kv_row),
        _from_key_minor(kvwt_p[..., seq - w_p:])[None],
        _from_key_minor(win_new_t)[None],
        u_p.reshape(batch, seq, POOL_DIM)[None, :, seq - POOL_BUF:],
        jnp.concatenate([state_pool[0], u_s[:, None, :]], axis=1)[None, :, 1:],
    )
```

```python
import functools
import math

import jax
import jax.numpy as jnp
import numpy as np
from jax import lax
from jax.experimental import pallas as pl
from jax.experimental.pallas import tpu as pltpu

D_MODEL = 1024
N_HEADS = 8
N_KV_HEADS = 2
HEAD_DIM = 64
GROUP = N_HEADS // N_KV_HEADS
NSA_DIM = N_HEADS * HEAD_DIM
KV_DIM = 2 * N_KV_HEADS * HEAD_DIM
KV_HALF = KV_DIM // 2
L_CMP = 32
L_SEL = 64
CMP_HIDDEN = 2 * HEAD_DIM
TOP_N = 16
WINDOW = 512
ROPE_THETA = 10000.0
POOL_WINDOWS = (2, 4, 8, 16)
POOL_DIM = D_MODEL // 2
POOL_GROUP_DIM = POOL_DIM // len(POOL_WINDOWS)
POOL_BUF = max(POOL_WINDOWS) - 1
POOL_HALO = POOL_BUF + 1
TOP_K = 4
SWIGLU_LIMIT = 7.0
SWIGLU_ALPHA = 1.702
RMS_EPS = 1e-6
NEG = -1e30
FORCED = 1e9

LANES = 128
QPAD_DIM = N_HEADS * LANES
VMEM_LIMIT = 56 * 1024 * 1024
Q_SCALE = HEAD_DIM ** -0.5 * math.log2(math.e)
ROW_TILE = 512
CMP_Q_TILE = 512
CMP_BLOCK_TILE = 256
MOE_TILE = 512
SAMPLE_MOE_TILE = 128
SAMPLE_SEQS_PER_STEP = 16
PAIR_UNROLL = 8
HEADS_PER_CHAIN = GROUP

BF16 = jnp.bfloat16
F32 = jnp.float32


def _params(n_grid, vmem=VMEM_LIMIT):
    return pltpu.CompilerParams(dimension_semantics=("arbitrary",) * n_grid, vmem_limit_bytes=vmem)


def _rms(x, g):
    r = lax.rsqrt(jnp.mean(x * x, axis=-1, keepdims=True) + RMS_EPS)
    return x * r * g


def _dot(a, b):
    return jnp.dot(a, b, preferred_element_type=F32)


def _dot_nt(a, b):
    return lax.dot_general(a, b, (((1,), (1,)), ((), ())), preferred_element_type=F32)


def _rope_chunk(x, cos, sin_signed):
    lane = lax.broadcasted_iota(jnp.int32, x.shape, 1)
    first = (lane % HEAD_DIM) < (HEAD_DIM // 2)
    swapped = jnp.where(first, pltpu.roll(x, LANES - HEAD_DIM // 2, 1), pltpu.roll(x, HEAD_DIM // 2, 1))
    return x * cos + swapped * sin_signed


def _project(x, g, cos, sin, pos, wq_ref, wkv_ref, wg_ref, wu_ref, outs):
    (q_ref, kvc_ref, kvct_ref, kvst_ref, kvwt_ref, ksb_ref, vs0_ref, vs1_ref, kwb_ref, vw0_ref, vw1_ref,
     gates_ref, u_ref) = outs
    xn = _rms(x, g).astype(BF16)
    q = _dot(xn, wq_ref[...])
    for h in range(N_HEADS):
        sl = slice(h * LANES, (h + 1) * LANES)
        q_ref[:, sl] = (_rope_chunk(q[:, sl], cos, sin) * Q_SCALE).astype(BF16)
    kv = _dot(xn, wkv_ref[...])
    n_sel = ksb_ref.shape[1] - KV_HALF
    blk = lax.broadcasted_iota(jnp.int32, (x.shape[0], n_sel), 1)
    for j, t_ref in enumerate((kvct_ref, kvst_ref, kvwt_ref)):
        k = _rope_chunk(kv[:, j * KV_DIM:j * KV_DIM + KV_HALF], cos, sin)
        v = kv[:, j * KV_DIM + KV_HALF:(j + 1) * KV_DIM]
        t_ref[0, 0] = k.T
        t_ref[0, 1] = v.T
        low = lax.broadcasted_iota(jnp.int32, v.shape, 1) < HEAD_DIM
        v0 = jnp.where(low, v, 1.0).astype(BF16)
        v1 = jnp.where(low, 1.0, v).astype(BF16)
        if j == 0:
            kvc_ref[:, :KV_HALF] = k
            kvc_ref[:, KV_HALF:] = v
        elif j == 1:
            ksb_ref[:, :KV_HALF] = k.astype(BF16)
            ksb_ref[:, KV_HALF:] = jnp.where(blk == pos // L_SEL, 1.0, 0.0).astype(BF16)
            vs0_ref[...] = v0
            vs1_ref[...] = v1
        else:
            kwb_ref[...] = k.astype(BF16)
            vw0_ref[...] = v0
            vw1_ref[...] = v1
    gates_ref[...] = jax.nn.sigmoid(_dot(xn, wg_ref[...]))
    u = _dot(xn, wu_ref[...])
    u_ref[...] = u
    return u


def _inproj_prompt_kernel(x_ref, xh_ref, cos_ref, sin_ref, g_ref, wq_ref, wkv_ref, wg_ref, wu_ref, *outs,
                          tiles_per_seq):
    d_ref = outs[-1]
    i = pl.program_id(0)
    tm = x_ref.shape[0]
    g = g_ref[...]
    seq_tile = i % tiles_per_seq
    pos = seq_tile * tm + lax.broadcasted_iota(jnp.int32, (tm, 1), 0)
    u = _project(x_ref[...], g, cos_ref[...], sin_ref[...], pos, wq_ref, wkv_ref, wg_ref, wu_ref, outs[:-1])
    uh = _dot(_rms(xh_ref[...], g).astype(BF16), wu_ref[...])
    uh = jnp.where(seq_tile == 0, 0.0, uh)
    ext = jnp.concatenate([uh, u], axis=0)
    for gi, w in enumerate(POOL_WINDOWS):
        sl = slice(gi * POOL_GROUP_DIM, (gi + 1) * POOL_GROUP_DIM)
        s = ext[:, sl]
        k = 1
        while k < w:
            s = s + pltpu.roll(s, k, 0)
            k *= 2
        cnt = jnp.minimum(w, pos + 1).astype(F32)
        d_ref[:, sl] = (s[POOL_HALO:, :] / cnt - u[:, sl]).astype(BF16)


def _inproj_sample_kernel(x_ref, sp_ref, cos_ref, sin_ref, g_ref, wq_ref, wkv_ref, wg_ref, wu_ref, *outs,
                          past_len):
    d_ref = outs[-1]
    pos = jnp.full((x_ref.shape[0], 1), past_len, jnp.int32)
    u = _project(x_ref[...], g_ref[...], cos_ref[...], sin_ref[...], pos, wq_ref, wkv_ref, wg_ref, wu_ref,
                 outs[:-1])
    for gi, w in enumerate(POOL_WINDOWS):
        sl = slice(gi * POOL_GROUP_DIM, (gi + 1) * POOL_GROUP_DIM)
        s = u[:, sl]
        for k in range(1, w):
            s = s + sp_ref[POOL_BUF - k][:, sl]
        cnt = float(min(w, past_len + 1))
        d_ref[:, sl] = (s / cnt - u[:, sl]).astype(BF16)


def _inproj_outs(n_seq, rows, tm, n_sel, row_map, seq_tile):
    t = n_seq * rows
    shapes, specs = [], []

    def add(shape, dtype, block, index_map):
        shapes.append(jax.ShapeDtypeStruct(shape, dtype))
        specs.append(pl.BlockSpec(block, index_map))

    t_map4 = lambda i: (seq_tile(i)[0], 0, 0, seq_tile(i)[1])
    add((t, QPAD_DIM), BF16, (tm, QPAD_DIM), row_map)
    add((t, KV_DIM), F32, (tm, KV_DIM), row_map)
    for _ in range(3):
        add((n_seq, 2, KV_HALF, rows), F32, (1, 2, KV_HALF, tm), t_map4)
    add((t, KV_HALF + n_sel), BF16, (tm, KV_HALF + n_sel), row_map)
    for _ in range(5):
        add((t, KV_HALF), BF16, (tm, KV_HALF), row_map)
    add((t, LANES), F32, (tm, LANES), row_map)
    add((t, POOL_DIM), F32, (tm, POOL_DIM), row_map)
    add((t, POOL_DIM), BF16, (tm, POOL_DIM), row_map)
    return tuple(shapes), tuple(specs)


def _weight_specs(shapes):
    return [pl.BlockSpec(s, lambda i, _n=len(s): (0,) * _n) for s in shapes]


def _inproj_prompt(x, cos, sin, g_mix, wq, wkv, wg, wu, batch, seq, tm):
    tiles_per_seq = seq // tm
    halo_blocks = tm // POOL_HALO
    in_specs = [
        pl.BlockSpec((tm, D_MODEL), lambda i: (i, 0)),
        pl.BlockSpec((POOL_HALO, D_MODEL), lambda i: (jnp.maximum(i * halo_blocks - 1, 0), 0)),
        pl.BlockSpec((tm, LANES), lambda i: (i % tiles_per_seq, 0)),
        pl.BlockSpec((tm, LANES), lambda i: (i % tiles_per_seq, 0)),
    ] + _weight_specs([g_mix.shape, wq.shape, wkv.shape, wg.shape, wu.shape])
    out_shape, out_specs = _inproj_outs(batch, seq, tm, seq // L_SEL, lambda i: (i, 0),
                                        lambda i: (i // tiles_per_seq, i % tiles_per_seq))
    return pl.pallas_call(
        functools.partial(_inproj_prompt_kernel, tiles_per_seq=tiles_per_seq),
        grid=(batch * tiles_per_seq,), in_specs=in_specs, out_specs=out_specs,
        out_shape=out_shape, compiler_params=_params(1), name="inproj_prompt",
    )(x, x, cos, sin, g_mix, wq, wkv, wg, wu)


def _inproj_sample(x, sp_t, cos, sin, g_mix, wq, wkv, wg, wu, past_len):
    t = x.shape[0]
    in_specs = [
        pl.BlockSpec((t, D_MODEL), lambda i: (0, 0)),
        pl.BlockSpec(sp_t.shape, lambda i: (0, 0, 0)),
        pl.BlockSpec((t, LANES), lambda i: (0, 0)),
        pl.BlockSpec((t, LANES), lambda i: (0, 0)),
    ] + _weight_specs([g_mix.shape, wq.shape, wkv.shape, wg.shape, wu.shape])
    out_shape, out_specs = _inproj_outs(1, t, t, LANES, lambda i: (0, 0), lambda i: (0, 0))
    return pl.pallas_call(
        functools.partial(_inproj_sample_kernel, past_len=past_len),
        grid=(1,), in_specs=in_specs, out_specs=out_specs,
        out_shape=out_shape, compiler_params=_params(1), name="inproj_sample",
    )(x, sp_t, cos, sin, g_mix, wq, wkv, wg, wu)


def _compress_rows(x_refs, pe_ref, w1_ref, w2_ref, o_ref):
    nb = o_ref.shape[0]
    for c, x_ref in enumerate(x_refs):
        acc = jnp.zeros((nb, w1_ref.shape[3]), F32)
        for l in range(L_CMP):
            xl = x_ref[pl.ds(l, nb, stride=L_CMP), :] + pe_ref[c, l]
            acc = acc + _dot(xl.astype(BF16), w1_ref[c, l])
        hid = acc * jax.nn.sigmoid(acc)
        o_ref[:, c * KV_HALF:(c + 1) * KV_HALF] = _dot(hid.astype(BF16), w2_ref[c]).astype(o_ref.dtype)


def _compress_kernel(xk_ref, xv_ref, pe_ref, w1_ref, w2_ref, o_ref):
    _compress_rows((xk_ref, xv_ref), pe_ref, w1_ref, w2_ref, o_ref)


def _compress(rows, pe, w1, w2, nb_tile):
    n_blocks = rows.shape[0] // L_CMP
    nb_tile = int(np.gcd(nb_tile, n_blocks))
    return pl.pallas_call(
        _compress_kernel,
        grid=(n_blocks // nb_tile,),
        in_specs=[pl.BlockSpec((nb_tile * L_CMP, KV_HALF), lambda i: (i, 0)),
                  pl.BlockSpec((nb_tile * L_CMP, KV_HALF), lambda i: (i, 1))]
        + _weight_specs([pe.shape, w1.shape, w2.shape]),
        out_specs=pl.BlockSpec((nb_tile, KV_DIM), lambda i: (i, 0)),
        out_shape=jax.ShapeDtypeStruct((n_blocks, KV_DIM), BF16),
        compiler_params=_params(1), name="compress",
    )(rows, rows, pe, w1, w2)


def _compress_pages_kernel(pt_ref, cache_ref, pet_ref, perm_ref, w1_ref, w2_ref, o_ref, buf_ref, sem_ref, xs_ref):
    b = pl.program_id(0)
    n_b = pl.num_programs(0)
    n_pages = buf_ref.shape[1]
    n_pairs = n_pages // 2
    nb = o_ref.shape[0]
    perm = perm_ref[...]

    def page_copy(bb, slot, j):
        return pltpu.make_async_copy(cache_ref.at[pt_ref[bb * n_pages + j]], buf_ref.at[slot, j], sem_ref.at[slot])

    @pl.when(b == 0)
    def _():
        for j in range(n_pages):
            page_copy(0, 0, j).start()

    @pl.when(b + 1 < n_b)
    def _():
        for j in range(n_pages):
            page_copy(b + 1, (b + 1) % 2, j).start()

    slot = b % 2
    for j in range(n_pages):
        page_copy(b, slot, j).wait()

    unroll = int(np.gcd(n_pairs, PAIR_UNROLL))

    def body(it, carry):
        for u in range(unroll):
            pp = it * unroll + u
            for c in range(2):
                xt = jnp.concatenate([buf_ref[slot, 2 * pp, c], buf_ref[slot, 2 * pp + 1, c]], axis=1) + pet_ref[c]
                xs_ref[c, pp] = _dot_nt(perm, xt.astype(BF16))
        return carry

    lax.fori_loop(0, n_pairs // unroll, body, 0)
    rows_per_l = 2 * LANES // L_CMP
    for c in range(2):
        acc = jnp.zeros((nb, w1_ref.shape[3]), F32)
        for l in range(L_CMP):
            xl = xs_ref[c, :, l * rows_per_l:(l + 1) * rows_per_l, :].reshape(nb, KV_HALF)
            acc = acc + _dot(xl.astype(BF16), w1_ref[c, l])
        hid = acc * jax.nn.sigmoid(acc)
        o_ref[:, c * KV_HALF:(c + 1) * KV_HALF] = _dot(hid.astype(BF16), w2_ref[c]).astype(o_ref.dtype)


def _compress_pages(cache_t, page_table, pe, w1, w2):
    _, _, _, page = cache_t.shape
    n_seq, n_pages = page_table.shape
    assert page == LANES and n_pages % 2 == 0
    per_page = page // L_CMP
    pet = jnp.tile(pe[:, :, 0, :].transpose(0, 2, 1), (1, 1, 2 * per_page))
    src = np.arange(2 * page)
    dst = (src % L_CMP) * (2 * per_page) + src // L_CMP
    perm = np.zeros((2 * page, 2 * page), np.float32)
    perm[dst, src] = 1.0
    perm = jnp.asarray(perm, BF16)
    fixed = lambda shape: pl.BlockSpec(shape, lambda b, pt, _n=len(shape): (0,) * _n)
    grid_spec = pltpu.PrefetchScalarGridSpec(
        num_scalar_prefetch=1, grid=(n_seq,),
        in_specs=[pl.BlockSpec(memory_space=pl.ANY)] + [fixed(a.shape) for a in (pet, perm, w1, w2)],
        out_specs=pl.BlockSpec((n_pages * per_page, KV_DIM), lambda b, pt: (b, 0)),
        scratch_shapes=[pltpu.VMEM((2, n_pages) + cache_t.shape[1:], F32), pltpu.SemaphoreType.DMA((2,)),
                        pltpu.VMEM((2, n_pages // 2, 2 * page, KV_HALF), F32)],
    )
    return pl.pallas_call(
        _compress_pages_kernel, grid_spec=grid_spec,
        out_shape=jax.ShapeDtypeStruct((n_seq * n_pages * per_page, KV_DIM), BF16),
        compiler_params=_params(1), name="compress_pages",
    )(page_table.reshape(-1), cache_t, pet, perm, w1, w2)


def _stack_heads(q, g):
    return jnp.concatenate([q[:, (GROUP * g + r) * LANES:(GROUP * g + r + 1) * LANES] for r in range(GROUP)], axis=0)


def _masked_softmax(s, mask):
    s = jnp.where(mask, s, NEG)
    e = jnp.exp2(s - jnp.max(s, axis=-1, keepdims=True))
    return e / jnp.sum(e, axis=-1, keepdims=True) * mask.astype(F32)


def _assemble_heads(o_heads):
    chunks = []
    for c in range(N_HEADS // 2):
        g = (2 * c) // GROUP
        a, b = o_heads[2 * c], o_heads[2 * c + 1]
        if g == 1:
            a = pltpu.roll(a, HEAD_DIM, 1)
        else:
            b = pltpu.roll(b, HEAD_DIM, 1)
        lane = lax.broadcasted_iota(jnp.int32, a.shape, 1)
        chunks.append(jnp.where(lane < HEAD_DIM, a, b))
    return jnp.concatenate(chunks, axis=1)


def _select_blocks(score, n_pick):
    n_blk = score.shape[1]
    lane = lax.broadcasted_iota(jnp.int32, score.shape, 1)
    work = score
    picks = []
    for _ in range(n_pick):
        m = jnp.max(work, axis=-1, keepdims=True)
        idx = jnp.min(jnp.where(work == m, lane, n_blk), axis=-1, keepdims=True)
        picks.append(idx)
        work = jnp.where(lane == idx, -jnp.inf, work)
    return picks


def _select_blocks_t(score_t, n_pick):
    n_blk = score_t.shape[0]
    blk = lax.broadcasted_iota(jnp.int32, score_t.shape, 0)
    work = score_t
    sel = jnp.zeros(score_t.shape, F32)
    for _ in range(n_pick):
        m = jnp.max(work, axis=0, keepdims=True)
        idx = jnp.min(jnp.where(work == m, blk, n_blk), axis=0, keepdims=True)
        hit = blk == idx
        sel = jnp.where(hit, 1.0, sel)
        work = jnp.where(hit, -jnp.inf, work)
    return sel


def _cmp_attention(qg, kc, vc, pos_rows, n_sel):
    s = _dot_nt(qg, kc)
    j = lax.broadcasted_iota(jnp.int32, s.shape, 1)
    n = 2 * (j % n_sel) + j // n_sel
    mask = (n * L_CMP + (L_CMP - 1)) <= pos_rows
    p = _masked_softmax(s, mask)
    return _dot(p.astype(BF16), vc), p


def _cmp_topk_kernel(q_ref, c_ref, gates_ref, oc_ref, msel_ref, *, n_sel, n_top):
    i = pl.program_id(1)
    tq = q_ref.shape[0]
    q = q_ref[...]
    comp = c_ref[0]
    kc, vc = comp[:, :LANES], comp[:, LANES:]
    gates = gates_ref[...]
    pos = i * tq + lax.broadcasted_iota(jnp.int32, (tq, 1), 0)
    pos_rows = jnp.concatenate([pos] * GROUP, axis=0)
    blk = lax.broadcasted_iota(jnp.int32, (n_sel, tq), 0)
    cur = (i * tq + lax.broadcasted_iota(jnp.int32, (1, tq), 1)) // L_SEL
    forced = (blk == 0) | (blk == cur) | (blk == cur - 1)
    o_heads = []
    for g in range(N_KV_HEADS):
        o, p = _cmp_attention(_stack_heads(q, g), kc, vc, pos_rows, n_sel)
        imp = p[0:tq]
        for r in range(1, GROUP):
            imp = imp + p[r * tq:(r + 1) * tq]
        imp = (imp[:, :n_sel] + imp[:, n_sel:]).T
        score = jnp.where(forced, FORCED, jnp.where(blk < cur, imp, NEG))
        sel = jnp.where(score > NEG / 2, _select_blocks_t(score, n_top), 0.0)
        msel_ref[:, g * n_sel:(g + 1) * n_sel] = ((1.0 - sel) * NEG).T.astype(BF16)
        for r in range(GROUP):
            h = GROUP * g + r
            o_heads.append(o[r * tq:(r + 1) * tq] * gates[:, h:h + 1])
    oc_ref[...] = _assemble_heads(o_heads)


def _cmp_topk_prompt(q, comp, gates, batch, seq, tq):
    n_sel = seq // L_SEL
    n_c = comp.shape[1]
    tiles = seq // tq
    row = lambda b, i: (b * tiles + i, 0)
    return pl.pallas_call(
        functools.partial(_cmp_topk_kernel, n_sel=n_sel, n_top=min(TOP_N, n_sel)),
        grid=(batch, tiles),
        in_specs=[pl.BlockSpec((tq, QPAD_DIM), row),
                  pl.BlockSpec((1, n_c, KV_DIM), lambda b, i: (b, 0, 0)),
                  pl.BlockSpec((tq, LANES), row)],
        out_specs=(pl.BlockSpec((tq, NSA_DIM), row), pl.BlockSpec((tq, N_KV_HEADS * n_sel), row)),
        out_shape=(jax.ShapeDtypeStruct((batch * seq, NSA_DIM), F32),
                   jax.ShapeDtypeStruct((batch * seq, N_KV_HEADS * n_sel), BF16)),
        compiler_params=_params(2), name="cmp_topk_prompt",
    )(q, comp, gates)


def _flash_update(carry, q, k, v, bias):
    m, acc = carry
    s = _dot_nt(q, k)
    if bias is not None:
        s = s + bias
    m_new = jnp.maximum(m, jnp.max(s, axis=-1, keepdims=True))
    p = jnp.exp2(s - m_new).astype(BF16)
    acc = jnp.exp2(m - m_new) * acc + _dot(p, v)
    return m_new, acc


def _sel_win_kernel(q_ref, ks_ref, vs0_ref, vs1_ref, kw_ref, vw0_ref, vw1_ref, msel_ref, gates_ref, oc_ref, o_ref,
                    *, n_sel):
    i = pl.program_id(1)
    tq = q_ref.shape[0]
    hpc = HEADS_PER_CHAIN
    n_chain = N_HEADS // hpc
    rows = hpc * tq
    q = q_ref[...]
    gates = gates_ref[...]
    vs_refs, vw_refs = (vs0_ref, vs1_ref), (vw0_ref, vw1_ref)
    pos = i * tq + lax.broadcasted_iota(jnp.int32, (tq, 1), 0)
    key = lax.broadcasted_iota(jnp.int32, (1, tq), 1)
    d0 = pl.multiple_of(i * tq, tq)
    p0 = pl.multiple_of(jnp.maximum(i - 1, 0) * tq, tq)
    bias_d = jnp.concatenate([jnp.where(i * tq + key <= pos, 0.0, NEG)] * hpc, axis=0)
    in_win = (pos - ((i - 1) * tq + key) < WINDOW) & (i > 0)
    bias_p = jnp.concatenate([jnp.where(in_win, 0.0, NEG)] * hpc, axis=0)
    groups = [(c * hpc) // GROUP for c in range(n_chain)]
    qcs, qas = [], []
    for c in range(n_chain):
        g = groups[c]
        qc = jnp.concatenate([q[:, h * LANES:(h + 1) * LANES] for h in range(c * hpc, (c + 1) * hpc)], axis=0)
        msel = jnp.concatenate([msel_ref[:, g * n_sel:(g + 1) * n_sel]] * hpc, axis=0)
        qcs.append(qc)
        qas.append(jnp.concatenate([qc, msel], axis=1))
    init = (jnp.full((rows, 1), NEG, F32), jnp.zeros((rows, LANES), F32))

    def tiles(k_ref, v_refs, start):
        return k_ref[pl.ds(start, tq), :], [v_ref[pl.ds(start, tq), :] for v_ref in v_refs]

    kd, vd = tiles(ks_ref, vs_refs, d0)
    sel = tuple(_flash_update(init, qas[c], kd, vd[groups[c]], bias_d) for c in range(n_chain))

    def body(kt, carries):
        k, v = tiles(ks_ref, vs_refs, pl.multiple_of(kt * tq, tq))
        return tuple(_flash_update(carries[c], qas[c], k, v[groups[c]], None) for c in range(n_chain))

    sel = lax.fori_loop(0, i, body, sel)
    kd, vd = tiles(kw_ref, vw_refs, d0)
    kp, vp = tiles(kw_ref, vw_refs, p0)
    o_heads = []
    for c in range(n_chain):
        g = groups[c]
        win = _flash_update(_flash_update(init, qcs[c], kd, vd[g], bias_d), qcs[c], kp, vp[g], bias_p)
        den = (1 - g) * HEAD_DIM
        o_s = sel[c][1] / sel[c][1][:, den:den + 1]
        o_w = win[1] / win[1][:, den:den + 1]
        for r in range(hpc):
            h = c * hpc + r
            rs = slice(r * tq, (r + 1) * tq)
            o_heads.append(o_s[rs] * gates[:, N_HEADS + h:N_HEADS + h + 1]
                           + o_w[rs] * gates[:, 2 * N_HEADS + h:2 * N_HEADS + h + 1])
    o_ref[...] = (_assemble_heads(o_heads) + oc_ref[...]).astype(o_ref.dtype)


def _sel_win_prompt(q, ksb, vs0, vs1, kwb, vw0, vw1, msel, gates, oc, batch, seq, tq):
    assert tq == WINDOW
    n_sel = seq // L_SEL
    tiles = seq // tq
    row = lambda b, i: (b * tiles + i, 0)
    whole = lambda b, i: (b, 0)
    return pl.pallas_call(
        functools.partial(_sel_win_kernel, n_sel=n_sel),
        grid=(batch, tiles),
        in_specs=[pl.BlockSpec((tq, QPAD_DIM), row),
                  pl.BlockSpec((seq, KV_HALF + n_sel), whole, pipeline_mode=pl.Buffered(1))]
        + [pl.BlockSpec((seq, KV_HALF), whole, pipeline_mode=pl.Buffered(1))] * 5
        + [pl.BlockSpec((tq, N_KV_HEADS * n_sel), row),
           pl.BlockSpec((tq, LANES), row),
           pl.BlockSpec((tq, NSA_DIM), row)],
        out_specs=pl.BlockSpec((tq, NSA_DIM), row),
        out_shape=jax.ShapeDtypeStruct((batch * seq, NSA_DIM), BF16),
        compiler_params=_params(2), name="sel_win_prompt",
    )(q, ksb, vs0, vs1, kwb, vw0, vw1, msel, gates, oc)


def _sample_cmp_kernel(q_ref, c_ref, gates_ref, oc_ref, idx_ref, *, n_blk, n_pick, past_len):
    n_sb = q_ref.shape[0]
    pos_rows = jnp.full((N_HEADS, 1), past_len, jnp.int32)
    imps = []
    for s in range(n_sb):
        comp = c_ref[s]
        o, p = _cmp_attention(q_ref[s], comp[:, :LANES], comp[:, LANES:], pos_rows, n_blk)
        oc_ref[s] = o * gates_ref[s][:, 0:1]
        imps += [jnp.sum(p[GROUP * g:GROUP * (g + 1)], axis=0, keepdims=True) for g in range(N_KV_HEADS)]
    imp = jnp.concatenate(imps, axis=0)
    imp = imp[:, :n_blk] + imp[:, n_blk:]
    cur = past_len // L_SEL
    blk = lax.broadcasted_iota(jnp.int32, imp.shape, 1)
    forced = (blk == 0) | (blk == cur) | (blk == cur - 1)
    score = jnp.where(forced, FORCED, jnp.where(blk < cur, imp, NEG))
    lane = lax.broadcasted_iota(jnp.int32, idx_ref.shape, 1)
    out = jnp.zeros(idx_ref.shape, jnp.int32)
    for k, idx in enumerate(_select_blocks(score, n_pick)):
        out = jnp.where(lane == k, idx, out)
    idx_ref[...] = out


def _sample_cmp(q3, comp, gates3, n_pick, past_len, seqs_per_step):
    n_seq = q3.shape[0]
    n_c = comp.shape[1]
    n_sb = int(np.gcd(seqs_per_step, n_seq))
    blk3 = lambda b: (b, 0, 0)
    return pl.pallas_call(
        functools.partial(_sample_cmp_kernel, n_blk=n_c // 2, n_pick=n_pick, past_len=past_len),
        grid=(n_seq // n_sb,),
        in_specs=[pl.BlockSpec((n_sb, N_HEADS, LANES), blk3),
                  pl.BlockSpec((n_sb, n_c, KV_DIM), blk3),
                  pl.BlockSpec((n_sb, N_HEADS, LANES), blk3)],
        out_specs=(pl.BlockSpec((n_sb, N_HEADS, LANES), blk3),
                   pl.BlockSpec((n_sb * N_KV_HEADS, LANES), lambda b: (b, 0))),
        out_shape=(jax.ShapeDtypeStruct((n_seq, N_HEADS, LANES), F32),
                   jax.ShapeDtypeStruct((n_seq * N_KV_HEADS, LANES), jnp.int32)),
        compiler_params=_params(1), name="sample_cmp",
    )(q3, comp, gates3)


def _softmax_with_new(s, valid, s_new):
    s = jnp.where(valid, s, NEG)
    m = jnp.maximum(jnp.max(s, axis=-1, keepdims=True), s_new)
    e = jnp.exp2(s - m) * valid.astype(F32)
    e_new = jnp.exp2(s_new - m)
    den = jnp.sum(e, axis=-1, keepdims=True) + e_new
    return e / den, e_new / den


def _sample_sel_win_kernel(pg_ref, hf_ref, q_ref, win_ref, kvs_ref, kvw_ref, gates_ref, oc_ref, cache_ref,
                           o_ref, buf_ref, sem_ref, *, n_pick):
    b = pl.program_id(0)
    n_b = pl.num_programs(0)
    n_dma = N_KV_HEADS * n_pick
    page = buf_ref.shape[-1]

    def page_copy(bb, slot, j):
        return pltpu.make_async_copy(cache_ref.at[pg_ref[bb * n_dma + j]], buf_ref.at[slot, j], sem_ref.at[slot])

    @pl.when(b == 0)
    def _():
        for j in range(n_dma):
            page_copy(0, 0, j).start()

    @pl.when(b + 1 < n_b)
    def _():
        for j in range(n_dma):
            page_copy(b + 1, (b + 1) % 2, j).start()

    q = q_ref[0]
    qf = q.astype(F32)
    gates = gates_ref[0]
    row = lax.broadcasted_iota(jnp.int32, (N_HEADS, LANES), 0)

    def new_key(kv_ref):
        kv = kv_ref[0].astype(BF16).astype(F32)
        return jnp.sum(qf * kv[:, :KV_HALF], axis=-1, keepdims=True), kv[:, KV_HALF:]

    win = win_ref[0]
    kt, vt = win[0].astype(BF16), win[1].astype(BF16)
    s_new, v_new = new_key(kvw_ref)
    lane_w = lax.broadcasted_iota(jnp.int32, (N_HEADS, kt.shape[1]), 1)
    p, p_new = _softmax_with_new(_dot(q, kt), lane_w >= 1, s_new)
    o_w = _dot_nt(p.astype(BF16), vt) + p_new.astype(BF16).astype(F32) * v_new

    slot = b % 2
    for j in range(n_dma):
        page_copy(b, slot, j).wait()
    s_new, v_new = new_key(kvs_ref)
    lane_p = lax.broadcasted_iota(jnp.int32, (N_HEADS, page), 1)
    o_s = jnp.zeros((N_HEADS, LANES), F32)
    for g in range(N_KV_HEADS):
        kts, vts, valids = [], [], []
        for k in range(n_pick):
            j = g * n_pick + k
            pg = buf_ref[slot, j]
            kts.append(pg[0].astype(BF16))
            vts.append(pg[1].astype(BF16))
            half = hf_ref[b * n_dma + j]
            valids.append((lane_p // L_SEL) == half)
        kt_all = jnp.concatenate(kts, axis=1)
        vt_all = jnp.concatenate(vts, axis=1)
        p, p_new = _softmax_with_new(_dot(q, kt_all), jnp.concatenate(valids, axis=1), s_new)
        o_g = _dot_nt(p.astype(BF16), vt_all) + p_new.astype(BF16).astype(F32) * v_new
        o_s = jnp.where(row // GROUP == g, o_g, o_s)
    o_ref[0] = oc_ref[0] + o_s * gates[:, 1:2] + o_w * gates[:, 2:3]


def _sample_sel_win(pages, halves, q3, win_t, kvs, kvw, gates3, oc, cache_t, n_pick):
    n_seq = q3.shape[0]
    n_dma = N_KV_HEADS * n_pick
    blk3 = lambda b, pg, hf: (b, 0, 0)
    grid_spec = pltpu.PrefetchScalarGridSpec(
        num_scalar_prefetch=2, grid=(n_seq,),
        in_specs=[pl.BlockSpec((1, N_HEADS, LANES), blk3),
                  pl.BlockSpec((1,) + win_t.shape[1:], lambda b, pg, hf: (b, 0, 0, 0)),
                  pl.BlockSpec((1, 1, KV_DIM), blk3),
                  pl.BlockSpec((1, 1, KV_DIM), blk3),
                  pl.BlockSpec((1, N_HEADS, LANES), blk3),
                  pl.BlockSpec((1, N_HEADS, LANES), blk3),
                  pl.BlockSpec(memory_space=pl.ANY)],
        out_specs=pl.BlockSpec((1, N_HEADS, LANES), blk3),
        scratch_shapes=[pltpu.VMEM((2, n_dma) + cache_t.shape[1:], F32), pltpu.SemaphoreType.DMA((2,))],
    )
    return pl.pallas_call(
        functools.partial(_sample_sel_win_kernel, n_pick=n_pick),
        grid_spec=grid_spec,
        out_shape=jax.ShapeDtypeStruct((n_seq, N_HEADS, LANES), F32),
        compiler_params=_params(1), name="sample_sel_win",
    )(pages, halves, q3, win_t, kvs, kvw, gates3, oc, cache_t)


def _tail_kernel(x_ref, attn_ref, d_ref, tri_ref, gmix_ref, wgm_ref, wpool_ref, pscale_ref,
                 wbn_ref, wbp_ref, wout_ref, gffn_ref, wr_ref, br_ref, h_ref, xn2_ref, route_ref, cout_ref, run_ref,
                 *, n_experts):
    i = pl.program_id(0)
    x = x_ref[...]
    attn = attn_ref[...]
    d = d_ref[...]
    tm = x.shape[0]
    xn = _rms(x, gmix_ref[...]).astype(BF16)
    gm = jax.nn.sigmoid(_dot(xn, wgm_ref[...]))
    pool = jnp.concatenate(
        [_dot(d[:, gi * POOL_GROUP_DIM:(gi + 1) * POOL_GROUP_DIM], wpool_ref[gi]) for gi in range(len(POOL_WINDOWS))],
        axis=1) * pscale_ref[...]
    merged = (gm[:, :D_MODEL] * _dot(attn, wbn_ref[...])
              + gm[:, D_MODEL:] * _dot(pool.astype(BF16), wbp_ref[...]))
    h = x + _dot(merged.astype(BF16), wout_ref[...])
    h_ref[...] = h
    xn2 = _rms(h, gffn_ref[...]).astype(BF16)
    xn2_ref[...] = xn2
    logits = _dot(xn2, wr_ref[...]) + br_ref[...]
    lane = lax.broadcasted_iota(jnp.int32, (tm, LANES), 1)
    work = jnp.where(lane < n_experts, logits, -jnp.inf)
    vals, onehots = [], []
    for _ in range(TOP_K):
        m = jnp.max(work, axis=-1, keepdims=True)
        idx = jnp.min(jnp.where(work == m, lane, LANES), axis=-1, keepdims=True)
        vals.append(m)
        onehots.append(lane == idx)
        work = jnp.where(lane == idx, -jnp.inf, work)
    es = [jnp.exp(v - vals[0]) for v in vals]
    tot = es[0] + es[1] + es[2] + es[3]

    @pl.when(i == 0)
    def _():
        run_ref[...] = jnp.zeros(run_ref.shape, F32)

    chosen = jnp.zeros((tm, LANES), F32)
    for k in range(TOP_K):
        chosen = jnp.where(onehots[k], 1.0, chosen)
    before = run_ref[...] + _dot(tri_ref[...], chosen.astype(BF16))
    route = jnp.zeros((tm, LANES), F32)
    lane_f = lane.astype(F32)
    for k in range(TOP_K):
        e_k = jnp.sum(jnp.where(onehots[k], lane_f, 0.0), axis=-1, keepdims=True)
        r_k = jnp.sum(jnp.where(onehots[k], before, 0.0), axis=-1, keepdims=True)
        route = jnp.where(lane == k, e_k, route)
        route = jnp.where(lane == TOP_K + k, es[k] / tot, route)
        route = jnp.where(lane == 2 * TOP_K + k, r_k, route)
    route_ref[...] = route
    run_ref[...] = run_ref[...] + jnp.sum(chosen, axis=0, keepdims=True)
    cout_ref[...] = run_ref[...]


def _tail(x, attn, d, weights, n_experts, tm):
    t = x.shape[0]
    row = lambda i: (i, 0)
    fixed = lambda i: (0, 0)
    tri = jnp.asarray(np.tril(np.ones((tm, tm), np.float32), -1), BF16)
    return pl.pallas_call(
        functools.partial(_tail_kernel, n_experts=n_experts),
        grid=(t // tm,),
        in_specs=[pl.BlockSpec((tm, D_MODEL), row), pl.BlockSpec((tm, NSA_DIM), row),
                  pl.BlockSpec((tm, POOL_DIM), row), pl.BlockSpec((tm, tm), fixed)]
        + _weight_specs([w.shape for w in weights]),
        out_specs=(pl.BlockSpec((tm, D_MODEL), row), pl.BlockSpec((tm, D_MODEL), row),
                   pl.BlockSpec((tm, LANES), row), pl.BlockSpec((1, LANES), fixed)),
        out_shape=(jax.ShapeDtypeStruct((t, D_MODEL), F32), jax.ShapeDtypeStruct((t, D_MODEL), BF16),
                   jax.ShapeDtypeStruct((t, LANES), F32), jax.ShapeDtypeStruct((1, LANES), F32)),
        scratch_shapes=[pltpu.VMEM((1, LANES), F32)],
        compiler_params=_params(1), name="tail",
    )(x, attn, d, tri, *weights)


def _moe_kernel(te_ref, tv_ref, tf_ref, x_ref, wg_ref, bg_ref, wu_ref, bu_ref, wd_ref, bd_ref, y_ref,
                wgb_ref, wub_ref, wdb_ref):
    t = pl.program_id(0)

    @pl.when(tf_ref[t] > 0)
    def _():
        wgb_ref[...] = wg_ref[0].astype(BF16)
        wub_ref[...] = wu_ref[0].astype(BF16)
        wdb_ref[...] = wd_ref[0].astype(BF16)

    @pl.when(tv_ref[t] > 0)
    def _():
        x = x_ref[...]
        gate = jnp.minimum(_dot(x, wgb_ref[...]) + bg_ref[0], SWIGLU_LIMIT)
        up = jnp.clip(_dot(x, wub_ref[...]) + bu_ref[0], -SWIGLU_LIMIT, SWIGLU_LIMIT)
        hid = (up + 1.0) * gate * jax.nn.sigmoid(SWIGLU_ALPHA * gate)
        y_ref[...] = (_dot(hid.astype(BF16), wdb_ref[...]) + bd_ref[0]).astype(y_ref.dtype)

    @pl.when(tv_ref[t] == 0)
    def _():
        y_ref[...] = jnp.zeros(y_ref.shape, y_ref.dtype)


def _moe(tile_e, tile_v, tile_f, xs, wg, bg, wu, bu, wd, bd, tme):
    p_pad = xs.shape[0]
    d_e = wg.shape[2]
    wspec = lambda shape: pl.BlockSpec((1,) + shape, lambda t, te, tv, tf: (te[t], 0, 0))
    grid_spec = pltpu.PrefetchScalarGridSpec(
        num_scalar_prefetch=3, grid=(p_pad // tme,),
        in_specs=[pl.BlockSpec((tme, D_MODEL), lambda t, te, tv, tf: (t, 0)),
                  wspec((D_MODEL, d_e)), wspec((1, d_e)), wspec((D_MODEL, d_e)), wspec((1, d_e)),
                  wspec((d_e, D_MODEL)), wspec((1, D_MODEL))],
        out_specs=pl.BlockSpec((tme, D_MODEL), lambda t, te, tv, tf: (t, 0)),
        scratch_shapes=[pltpu.VMEM((D_MODEL, d_e), BF16), pltpu.VMEM((D_MODEL, d_e), BF16),
                        pltpu.VMEM((d_e, D_MODEL), BF16)],
    )
    return pl.pallas_call(
        _moe_kernel, grid_spec=grid_spec,
        out_shape=jax.ShapeDtypeStruct((p_pad, D_MODEL), BF16),
        compiler_params=_params(1), name="moe_experts",
    )(tile_e, tile_v, tile_f, xs, wg, bg, wu, bu, wd, bd)


def _final_kernel(h_ref, y_ref, route_ref, g_ref, o_ref):
    route = route_ref[...]
    acc = h_ref[...]
    for k in range(TOP_K):
        acc = acc + y_ref[k].astype(F32) * route[:, TOP_K + k:TOP_K + k + 1]
    o_ref[...] = _rms(acc, g_ref[...])


def _final(h, yg, route, g_final, tm, first, n_tiles):
    src = lambda i: (i + first, 0)
    return pl.pallas_call(
        _final_kernel, grid=(n_tiles,),
        in_specs=[pl.BlockSpec((tm, D_MODEL), src), pl.BlockSpec((TOP_K, tm, D_MODEL), lambda i: (0, i + first, 0)),
                  pl.BlockSpec((tm, LANES), src), pl.BlockSpec((1, D_MODEL), lambda i: (0, 0))],
        out_specs=pl.BlockSpec((tm, D_MODEL), lambda i: (i, 0)),
        out_shape=jax.ShapeDtypeStruct((n_tiles * tm, D_MODEL), F32),
        compiler_params=_params(1), name="final_norm",
    )(h, yg, route, g_final)


def _rope_tables(pos):
    half = HEAD_DIM // 2
    inv = ROPE_THETA ** (-jnp.arange(half, dtype=F32) / half)
    ang = pos.astype(F32)[:, None] * inv[None, :]
    cos, sin = jnp.cos(ang), jnp.sin(ang)
    cos = jnp.concatenate([cos, cos] * (LANES // HEAD_DIM), axis=1)
    sin = jnp.concatenate([-sin, sin] * (LANES // HEAD_DIM), axis=1)
    return cos, sin


def _split_w_in(w_in):
    o_q = NSA_DIM
    o_g = o_q + 3 * KV_DIM
    o_u = o_g + 3 * N_HEADS
    o_m = o_u + POOL_DIM
    wq = w_in[:, :o_q].reshape(D_MODEL, N_HEADS, HEAD_DIM)
    zeros = jnp.zeros_like(wq)
    lo = jnp.concatenate([wq, zeros], axis=2)
    hi = jnp.concatenate([zeros, wq], axis=2)
    in_hi = (jnp.arange(N_HEADS) // GROUP == 1)[None, :, None]
    wq_pad = jnp.where(in_hi, hi, lo).reshape(D_MODEL, QPAD_DIM)
    wkv = w_in[:, o_q:o_g]
    wg = w_in[:, o_g:o_u].reshape(D_MODEL, N_HEADS, 3).transpose(0, 2, 1).reshape(D_MODEL, 3 * N_HEADS)
    wg = jnp.pad(wg, ((0, 0), (0, LANES - 3 * N_HEADS)))
    wu = w_in[:, o_u:o_m]
    wgm = w_in[:, o_m:]
    return [w.astype(BF16) for w in (wq_pad, wkv, wg, wu, wgm)]


def _compress_weights(cmp_pe, cmp_w1, cmp_w2):
    eye = jnp.eye(N_KV_HEADS, dtype=F32)
    w1 = jnp.einsum('cldh,gf->clgdfh', cmp_w1, eye).reshape(2, L_CMP, KV_HALF, N_KV_HEADS * CMP_HIDDEN)
    w2 = jnp.einsum('chd,gf->cghfd', cmp_w2, eye).reshape(2, N_KV_HEADS * CMP_HIDDEN, KV_HALF)
    pe = jnp.broadcast_to(cmp_pe.transpose(1, 0, 2)[:, :, None, :], (2, L_CMP, N_KV_HEADS, HEAD_DIM))
    return pe.reshape(2, L_CMP, 1, KV_HALF), w1.astype(BF16), w2.astype(BF16)


def _even_odd(comp, n_seq):
    n_c = comp.shape[0] // n_seq
    return comp.reshape(n_seq, n_c // 2, 2, KV_DIM).transpose(0, 2, 1, 3).reshape(n_seq, n_c, KV_DIM)


def _fix_sample_heads(o3):
    n_seq = o3.shape[0]
    o4 = o3.reshape(n_seq, N_KV_HEADS, GROUP, N_KV_HEADS, HEAD_DIM)
    return jnp.concatenate([o4[:, g, :, g, :] for g in range(N_KV_HEADS)], axis=1).reshape(n_seq, NSA_DIM)


def _gates_by_head(gates):
    g = gates[:, :3 * N_HEADS].reshape(-1, 3, N_HEADS).transpose(0, 2, 1)
    return jnp.pad(g, ((0, 0), (0, 0), (0, LANES - 3)))


def _key_minor(x):
    n, p = x.shape[:2]
    return x.transpose(0, 2, 3, 4, 1).reshape(n, 2, KV_HALF, p)


def _from_key_minor(xt):
    n, _, _, p = xt.shape
    return xt.reshape(n, 2, N_KV_HEADS, HEAD_DIM, p).transpose(0, 4, 1, 2, 3)


def _expert_order(route, counts, n_experts, tme):
    t = route.shape[0]
    n_pairs = t * TOP_K
    e = route[:, :TOP_K].astype(jnp.int32)
    rank = route[:, 2 * TOP_K:3 * TOP_K].astype(jnp.int32)
    counts = counts[0, :n_experts].astype(jnp.int32)
    padded = ((counts + tme - 1) // tme) * tme
    ends = jnp.cumsum(padded)
    off = ends - padded
    pad_before = off - (jnp.cumsum(counts) - counts)
    experts = jnp.arange(n_experts, dtype=jnp.int32)
    pair_slot = rank + jnp.sum(jnp.where(e[:, :, None] == experts, off, 0), axis=-1)
    n_tiles = -(-n_pairs // tme) + n_experts
    p_pad = n_tiles * tme
    tok = jnp.broadcast_to(jnp.arange(t, dtype=jnp.int32)[:, None], (t, TOP_K))
    _, tok_sorted = lax.sort((pair_slot.reshape(n_pairs), tok.reshape(n_pairs)), num_keys=1)
    slot = jnp.arange(p_pad, dtype=jnp.int32)
    slot_e = jnp.sum((slot[:, None] >= ends[None, :]).astype(jnp.int32), axis=-1)
    slot_e = jnp.minimum(slot_e, n_experts - 1)
    slot_pad = jnp.sum(jnp.where(slot_e[:, None] == experts, pad_before, 0), axis=-1)
    src_tok = tok_sorted[jnp.clip(slot - slot_pad, 0, n_pairs - 1)]
    tile_start = jnp.arange(n_tiles, dtype=jnp.int32) * tme
    tile_e = jnp.minimum(jnp.sum((tile_start[:, None] >= ends[None, :]).astype(jnp.int32), axis=-1), n_experts - 1)
    tile_v = (tile_start < ends[-1]).astype(jnp.int32)
    tile_f = jnp.concatenate([jnp.ones((1,), jnp.int32), (tile_e[1:] != tile_e[:-1]).astype(jnp.int32)])
    return pair_slot, src_tok, tile_e, tile_v, tile_f


def kernel(x_prompt, x_sample, cache_kv_cmp, cache_kv_sel, state_kv_win, state_pool, page_table, g_mix, w_in, cmp_pe, cmp_w1, cmp_w2, w_pool, pool_scale, w_br_nsa, w_br_pool, w_out, g_ffn, w_router, b_router, w_gate, b_gate, w_up, b_up, w_down, b_down, g_final):
    batch, seq, _ = x_prompt.shape
    n_seq, dec_seq, _ = x_sample.shape
    depth = g_mix.shape[0]
    assert depth == 1 and dec_seq == 1
    page_size = cache_kv_cmp.shape[2]
    n_pages = page_table.shape[1]
    past_len = n_pages * page_size
    n_experts = w_router.shape[2]
    t_p = batch * seq
    tm = min(ROW_TILE, seq)

    wq, wkv, wg, wu, wgm = _split_w_in(w_in[0])
    gmix = g_mix[0][None, :]
    pe, w1, w2 = _compress_weights(cmp_pe[0], cmp_w1[0], cmp_w2[0])
    cos_p, sin_p = _rope_tables(jnp.arange(seq, dtype=jnp.int32))
    cos_s, sin_s = _rope_tables(jnp.full((n_seq,), past_len, jnp.int32))
    wr = jnp.pad(w_router[0], ((0, 0), (0, LANES - n_experts))).astype(BF16)
    br = jnp.pad(b_router[0], (0, LANES - n_experts))[None, :]
    tail_w = [gmix, wgm, w_pool[0].astype(BF16), pool_scale[0][None, :], w_br_nsa[0].astype(BF16),
              w_br_pool[0].astype(BF16), w_out[0].astype(BF16), g_ffn[0][None, :], wr, br]

    xp = x_prompt.reshape(t_p, D_MODEL)
    (q_p, kvc_p, kvct_p, kvst_p, kvwt_p, ksb_p, vs0_p, vs1_p, kwb_p, vw0_p, vw1_p, gates_p, u_p, d_p) = _inproj_prompt(
        xp, cos_p, sin_p, gmix, wq, wkv, wg, wu, batch, seq, tm)
    comp_p = _even_odd(_compress(kvc_p, pe, w1, w2, CMP_BLOCK_TILE), batch)
    tq = min(CMP_Q_TILE, seq)
    oc_p, msel_p = _cmp_topk_prompt(q_p, comp_p, gates_p, batch, seq, tq)
    attn_p = _sel_win_prompt(q_p, ksb_p, vs0_p, vs1_p, kwb_p, vw0_p, vw1_p, msel_p, gates_p, oc_p, batch, seq, WINDOW)

    def ffn(x, attn, d, tile, tme, after=None):
        h, xn2, route, counts = _tail(x, attn, d, tail_w, n_experts, tile)
        pair_slot, src_tok, tile_e, tile_v, tile_f = _expert_order(route, counts, n_experts, tme)
        tied = after(src_tok) if after is not None else None
        y_sorted = _moe(tile_e, tile_v, tile_f, xn2[src_tok], w_gate[0], b_gate[0][:, None, :],
                        w_up[0], b_up[0][:, None, :], w_down[0], b_down[0][:, None, :], tme)
        y = _final(h, y_sorted[pair_slot.T], route, g_final[None, :], tile, 0, x.shape[0] // tile)
        return y, tied

    tie = lambda order: lax.optimization_barrier((x_sample.reshape(n_seq, D_MODEL), page_table, order))[:2]
    y_p, (xs, page_table) = ffn(xp, attn_p, d_p, tm, MOE_TILE, after=tie)

    sp_t = state_pool[0].transpose(1, 0, 2)
    (q_s, kvc_s, _, kvst_s, kvwt_s, _, _, _, _, _, _, gates_s, u_s, d_s) = _inproj_sample(
        xs, sp_t, cos_s, sin_s, gmix, wq, wkv, wg, wu, past_len)
    kvs_s = kvst_s.reshape(KV_DIM, n_seq).T
    kvw_s = kvwt_s.reshape(KV_DIM, n_seq).T
    comp_s = _even_odd(_compress_pages(_key_minor(cache_kv_cmp[0]), page_table, pe, w1, w2), n_seq)
    q3 = q_s.reshape(n_seq, N_HEADS, LANES)
    gates3 = _gates_by_head(gates_s)
    n_blk = past_len // L_SEL
    n_pick = min(TOP_N, n_blk + 1) - 1
    oc_s, idx_s = _sample_cmp(q3, comp_s, gates3, n_pick, past_len, SAMPLE_SEQS_PER_STEP)
    top_idx = idx_s.reshape(n_seq, N_KV_HEADS, LANES)[:, :, :n_pick]
    sel_per_page = page_size // L_SEL
    pages_b = jnp.broadcast_to(page_table[:, None, :], (n_seq, N_KV_HEADS, n_pages))
    phys = jnp.take_along_axis(pages_b, top_idx // sel_per_page, axis=2).reshape(-1)
    halves = (top_idx % sel_per_page).reshape(-1)
    w_buf = state_kv_win.shape[2]
    assert w_buf == WINDOW
    win_t = _key_minor(state_kv_win[0])
    o3 = _sample_sel_win(phys, halves, q3, win_t, kvs_s[:, None, :], kvw_s[:, None, :], gates3, oc_s,
                         _key_minor(cache_kv_sel[0]), n_pick)
    attn_s = _fix_sample_heads(o3).astype(BF16)
    y_s, _ = ffn(xs, attn_s, d_s, n_seq, SAMPLE_MOE_TILE)

    w_p = min(WINDOW, seq)
    kv_row = (2, N_KV_HEADS, HEAD_DIM)
    new_col = kvwt_s.reshape(2, KV_HALF, n_seq).transpose(2, 0, 1)[..., None]
    win_new_t = jnp.concatenate([win_t[..., 1:], new_col], axis=-1)
    return (
        y_p.reshape(batch, seq, D_MODEL),
        y_s.reshape(n_seq, 1, D_MODEL),
        _from_key_minor(kvct_p)[None],
        kvc_s.reshape((1, n_seq, 1) + kv_row),
        _from_key_minor(kvst_p)[None],
        kvs_s.reshape((1, n_seq, 1) + kv_row),
        _from_key_minor(kvwt_p[..., seq - w_p:])[None],
        _from_key_minor(win_new_t)[None],
        u_p.reshape(batch, seq, POOL_DIM)[None, :, seq - POOL_BUF:],
        jnp.concatenate([state_pool[0], u_s[:, None, :]], axis=1)[None, :, 1:],
    )
```
